```python
import jax, jax.numpy as jnp
from jax import lax
import numpy as np

D_MODEL = 2048
BATCH = 4
SEQ = 4096
DEPTH = 1

HEAD_DIM = 128
HEADS_PER_GROUP = 4
DILATED_GROUPS = ((128, 1), (512, 4), (2048, 16))
N_ATT_GROUPS = 3
N_ATT_HEADS = N_ATT_GROUPS * HEADS_PER_GROUP
ATT_WIDTH = N_ATT_HEADS * HEAD_DIM
ATT_OUT_WIDTH = HEADS_PER_GROUP * HEAD_DIM
ROPE_DIM = HEAD_DIM // 4
ROPE_THETA = 500000.0
ATT_BLOCK = 128

SG_CHUNK = 128
SG_GROUPS = 12
SG_GROUP_DIM = 128
SG_WIDTH = SG_GROUPS * SG_GROUP_DIM

D_FF = 5632

NORM_EPS = 1e-6
LN_EPS = 1e-5
IN_WIDTH = 3 * ATT_WIDTH + 2 * SG_WIDTH + 2 * D_MODEL

kernel_name = "dilated_attn_gmlp_gated_macaron_layer"


def rmsnorm(x, g):
    x32 = x.astype(jnp.float32)
    r = x32 * lax.rsqrt(jnp.mean(x32 * x32, axis=-1, keepdims=True) + NORM_EPS)
    return (r * g.astype(jnp.float32)).astype(x.dtype)


def layernorm(x, g, b):
    x32 = x.astype(jnp.float32)
    mu = jnp.mean(x32, axis=-1, keepdims=True)
    var = jnp.mean(jnp.square(x32 - mu), axis=-1, keepdims=True)
    y = (x32 - mu) * lax.rsqrt(var + LN_EPS)
    return (y * g.astype(jnp.float32) + b.astype(jnp.float32)).astype(x.dtype)


def swiglu(x, w_gate, w_up, w_down):
    return (jax.nn.silu(x @ w_gate) * (x @ w_up)) @ w_down


def partial_rope(t, pos):
    half = ROPE_DIM // 2
    inv_freq = ROPE_THETA ** (-jnp.arange(0, ROPE_DIM, 2, dtype=jnp.float32) / ROPE_DIM)
    ang = pos.astype(jnp.float32)[:, None] * inv_freq[None, :]
    ang = ang.reshape((1, t.shape[1]) + (1,) * (t.ndim - 3) + (half,))
    cos = jnp.cos(ang).astype(t.dtype)
    sin = jnp.sin(ang).astype(t.dtype)
    x1 = t[..., :half]
    x2 = t[..., half:ROPE_DIM]
    return jnp.concatenate([x1 * cos - x2 * sin, x2 * cos + x1 * sin, t[..., ROPE_DIM:]], axis=-1)


def dilated_window_attention(q, k, v, window, dilation):
    b, s, h, hd = q.shape
    L = s // dilation
    n_blk = -(-L // ATT_BLOCK)
    Lp = n_blk * ATT_BLOCK
    reach = window // dilation

    def by_residue(t):
        t = t.reshape(b, L, dilation, h, hd).transpose(0, 3, 2, 1, 4)
        t = jnp.pad(t, ((0, 0), (0, 0), (0, 0), (0, Lp - L), (0, 0)))
        return t.reshape(b, h, dilation, n_blk, ATT_BLOCK, hd)

    qb, kb, vb = by_residue(q), by_residue(k), by_residue(v)

    def with_prev(t):
        prev = jnp.pad(t[:, :, :, :-1], ((0, 0), (0, 0), (0, 0), (1, 0), (0, 0), (0, 0)))
        return jnp.concatenate([prev, t], axis=4)

    kw, vw = with_prev(kb), with_prev(vb)
    scores = jnp.einsum('bhrnqd,bhrnkd->bhrnqk', qb, kw).astype(jnp.float32) * (hd ** -0.5)
    qi = jnp.arange(ATT_BLOCK)[None, :, None]
    kj = jnp.arange(2 * ATT_BLOCK)[None, None, :]
    blk = jnp.arange(n_blk)[:, None, None]
    diff = qi + ATT_BLOCK - kj
    key_idx = blk * ATT_BLOCK - ATT_BLOCK + kj
    valid = (diff >= 0) & (diff <= reach) & (key_idx >= 0)
    scores = jnp.where(valid[None, None, None], scores, -jnp.inf)
    m = jnp.max(scores, axis=-1, keepdims=True)
    p = jnp.exp(scores - m)
    l = jnp.sum(p, axis=-1, keepdims=True)
    o = jnp.einsum('bhrnqk,bhrnkd->bhrnqd', p.astype(vw.dtype), vw).astype(jnp.float32) / l

    o = o.reshape(b, h, dilation, Lp, hd)[:, :, :, :L].transpose(0, 3, 2, 1, 4).reshape(b, s, h, hd)

    def stat_back(t):
        t = t.reshape(b, h, dilation, Lp)[:, :, :, :L]
        return t.transpose(0, 3, 2, 1).reshape(b, s, h)

    return o, stat_back(m), stat_back(l)


def hybrid_mixer(h, w_in, sg_ln_g, sg_ln_b, sg_w, sg_b, w_att_out, w_sg_out, w_out):
    b, s, _ = h.shape
    proj = h @ w_in
    splits = np.cumsum([ATT_WIDTH, ATT_WIDTH, ATT_WIDTH, SG_WIDTH, SG_WIDTH, D_MODEL]).tolist()
    q, k, v, u, vs, g_att, g_sg = jnp.split(proj, splits, axis=-1)

    pos = jnp.arange(s)
    q = partial_rope(q.reshape(b, s, N_ATT_GROUPS, HEADS_PER_GROUP, HEAD_DIM), pos)
    k = partial_rope(k.reshape(b, s, N_ATT_GROUPS, HEADS_PER_GROUP, HEAD_DIM), pos)
    v = v.reshape(b, s, N_ATT_GROUPS, HEADS_PER_GROUP, HEAD_DIM)
    outs, maxes, dens = [], [], []
    for gi, (window, dilation) in enumerate(DILATED_GROUPS):
        o_g, m_g, l_g = dilated_window_attention(q[:, :, gi], k[:, :, gi], v[:, :, gi], window, dilation)
        outs.append(o_g)
        maxes.append(m_g)
        dens.append(l_g)
    o_all = jnp.stack(outs)
    m_all = jnp.stack(maxes)
    l_all = jnp.stack(dens)
    w_den = l_all * jnp.exp(m_all - jnp.max(m_all, axis=0, keepdims=True))
    o_att = jnp.sum(w_den[..., None] * o_all, axis=0) / jnp.sum(w_den, axis=0)[..., None]
    y_att = o_att.astype(h.dtype).reshape(b, s, ATT_OUT_WIDTH) @ w_att_out

    u = jax.nn.gelu(u, approximate=False)
    vs = layernorm(jax.nn.gelu(vs, approximate=False), sg_ln_g, sg_ln_b)
    vc = vs.reshape(b, s // SG_CHUNK, SG_CHUNK, SG_GROUPS, SG_GROUP_DIM)
    causal = jnp.tril(jnp.ones((SG_CHUNK, SG_CHUNK), dtype=sg_w.dtype))
    w_sp = sg_w * causal[None]
    spatial = jnp.einsum('gts,bcsgd->bctgd', w_sp, vc) + sg_b.T[None, None, :, :, None]
    y_sg = (u * spatial.reshape(b, s, SG_WIDTH)) @ w_sg_out

    merged = jax.nn.sigmoid(g_att) * y_att + jax.nn.sigmoid(g_sg) * y_sg
    return merged @ w_out


def setup_inputs(seed: int = 0) -> dict:
    key = jax.random.key(seed)
    ks = jax.random.split(key, 24)
    f32 = jnp.float32

    def nrm(k, shape, fan_in):
        return jax.random.normal(k, shape, f32) * (fan_in ** -0.5)

    def gain(k, shape):
        return 1.0 + 0.05 * jax.random.normal(k, shape, f32)

    return {
        "x": jax.random.normal(ks[0], (BATCH, SEQ, D_MODEL), f32),
        "ffn1_norm": gain(ks[1], (DEPTH, D_MODEL)),
        "ffn1_w_gate": nrm(ks[2], (DEPTH, D_MODEL, D_FF), D_MODEL),
        "ffn1_w_up": nrm(ks[3], (DEPTH, D_MODEL, D_FF), D_MODEL),
        "ffn1_w_down": nrm(ks[4], (DEPTH, D_FF, D_MODEL), D_FF),
        "mix_norm": gain(ks[5], (DEPTH, D_MODEL)),
        "w_in": nrm(ks[6], (DEPTH, D_MODEL, IN_WIDTH), D_MODEL),
        "sg_ln_g": gain(ks[7], (DEPTH, SG_WIDTH)),
        "sg_ln_b": 0.02 * jax.random.normal(ks[8], (DEPTH, SG_WIDTH), f32),
        "sg_w": nrm(ks[9], (DEPTH, SG_GROUPS, SG_CHUNK, SG_CHUNK), SG_CHUNK),
        "sg_b": gain(ks[10], (DEPTH, SG_GROUPS, SG_CHUNK)),
        "w_att_out": nrm(ks[11], (DEPTH, ATT_OUT_WIDTH, D_MODEL), ATT_OUT_WIDTH),
        "w_sg_out": nrm(ks[12], (DEPTH, SG_WIDTH, D_MODEL), SG_WIDTH),
        "w_out": nrm(ks[13], (DEPTH, D_MODEL, D_MODEL), D_MODEL),
        "ffn2_norm": gain(ks[14], (DEPTH, D_MODEL)),
        "ffn2_w_gate": nrm(ks[15], (DEPTH, D_MODEL, D_FF), D_MODEL),
        "ffn2_w_up": nrm(ks[16], (DEPTH, D_MODEL, D_FF), D_MODEL),
        "ffn2_w_down": nrm(ks[17], (DEPTH, D_FF, D_MODEL), D_FF),
        "final_norm": gain(ks[18], (D_MODEL,)),
    }


def reference(x, ffn1_norm, ffn1_w_gate, ffn1_w_up, ffn1_w_down, mix_norm, w_in, sg_ln_g, sg_ln_b,
              sg_w, sg_b, w_att_out, w_sg_out, w_out, ffn2_norm, ffn2_w_gate, ffn2_w_up, ffn2_w_down,
              final_norm):
    for i in range(DEPTH):
        x = x + 0.5 * swiglu(rmsnorm(x, ffn1_norm[i]), ffn1_w_gate[i], ffn1_w_up[i], ffn1_w_down[i])
        x = x + hybrid_mixer(rmsnorm(x, mix_norm[i]), w_in[i], sg_ln_g[i], sg_ln_b[i], sg_w[i], sg_b[i],
                             w_att_out[i], w_sg_out[i], w_out[i])
        x = x + 0.5 * swiglu(rmsnorm(x, ffn2_norm[i]), ffn2_w_gate[i], ffn2_w_up[i], ffn2_w_down[i])
    return rmsnorm(x, final_norm)
```

```python
import functools

import jax
import jax.numpy as jnp
from jax import lax
from jax.experimental import pallas as pl
from jax.experimental.pallas import tpu as pltpu

F32 = jnp.float32
BF16 = jnp.bfloat16

D_MODEL = 2048
SEQ = 4096
HEAD_DIM = 128
HEADS_PER_GROUP = 4
DILATIONS = (1, 4, 16)
N_GROUPS = len(DILATIONS)
GROUP_WIDTH = HEADS_PER_GROUP * HEAD_DIM
ATT_WIDTH = N_GROUPS * GROUP_WIDTH
ATT_BLOCK = 128
ROPE_DIM = HEAD_DIM // 4
ROPE_HALF = ROPE_DIM // 2
ROPE_THETA = 500000.0
SG_CHUNK = 128
SG_GROUPS = 12
SG_GROUP_DIM = 128
SG_WIDTH = SG_GROUPS * SG_GROUP_DIM
NORM_EPS = 1e-6
LN_EPS = 1e-5

V7X_VMEM_BYTES = 64 * 1024 * 1024
V7X_LANES = 128
VMEM_LIMIT = 52 * 1024 * 1024


def _params(semantics):
    return pltpu.CompilerParams(dimension_semantics=semantics, vmem_limit_bytes=VMEM_LIMIT)


def _rms(x, g):
    return x * lax.rsqrt(jnp.mean(x * x, axis=-1, keepdims=True) + NORM_EPS) * g


def _rope_table_kernel(freq_ref, cos_ref, sin_lo_ref, sin_hi_ref):
    rows = cos_ref.shape[0]
    pos = (pl.program_id(0) * rows + lax.broadcasted_iota(jnp.int32, (rows, V7X_LANES), 0)).astype(F32)
    lane = lax.broadcasted_iota(jnp.int32, (rows, V7X_LANES), 1)
    ang = pos * freq_ref[...]
    c = jnp.cos(ang)
    s = jnp.sin(ang)
    cos_ref[...] = jnp.where(lane < ROPE_DIM, c, 1.0)
    sin_lo_ref[...] = jnp.where(lane < ROPE_HALF, -s, 0.0)
    sin_hi_ref[...] = jnp.where((lane >= ROPE_HALF) & (lane < ROPE_DIM), s, 0.0)


def _rope_tables():
    inv_freq = ROPE_THETA ** (-jnp.arange(0, ROPE_DIM, 2, dtype=F32) / ROPE_DIM)
    freq_lane = jnp.concatenate([inv_freq, inv_freq, jnp.zeros((V7X_LANES - ROPE_DIM,), F32)])[None, :]
    rows = 512
    tab = jax.ShapeDtypeStruct((SEQ, V7X_LANES), F32)
    spec = pl.BlockSpec((rows, V7X_LANES), lambda i: (i, 0))
    return pl.pallas_call(
        _rope_table_kernel,
        grid=(SEQ // rows,),
        in_specs=[pl.BlockSpec((1, V7X_LANES), lambda i: (0, 0))],
        out_specs=[spec, spec, spec],
        out_shape=[tab, tab, tab],
        compiler_params=_params(("parallel",)),
        name="rope_table",
    )(freq_lane)


def _ffn_kernel(x_ref, g_ref, wg_ref, wu_ref, wd_ref, pg_ref, *refs, post):
    if post == "hn":
        o_ref, hn_ref, xn_ref = refs
    else:
        o_ref, xn_ref = refs
    j = pl.program_id(1)

    @pl.when(j == 0)
    def _():
        xn_ref[...] = _rms(x_ref[...], g_ref[...]).astype(BF16)
        o_ref[...] = jnp.zeros_like(o_ref)

    xn = xn_ref[...]
    h = jnp.dot(xn, wg_ref[...], preferred_element_type=F32)
    u = jnp.dot(xn, wu_ref[...], preferred_element_type=F32)
    a = (jax.nn.silu(h) * u).astype(BF16)
    o_ref[...] += jnp.dot(a, wd_ref[...], preferred_element_type=F32)

    @pl.when(j == pl.num_programs(1) - 1)
    def _():
        y = x_ref[...] + 0.5 * o_ref[...]
        if post == "hn":
            o_ref[...] = y
            hn_ref[...] = _rms(y, pg_ref[...]).astype(BF16)
        elif post == "final":
            o_ref[...] = _rms(y, pg_ref[...])
        else:
            o_ref[...] = y


def _ffn(x, g, wg, wu, wd, post_g, *, post, tm=512, tf=512):
    t, d = x.shape
    f = wg.shape[1]
    row = pl.BlockSpec((tm, d), lambda i, j: (i, 0))
    vec = pl.BlockSpec((1, d), lambda i, j: (0, 0))
    out_shape = [jax.ShapeDtypeStruct((t, d), F32)]
    out_specs = [row]
    if post == "hn":
        out_shape.append(jax.ShapeDtypeStruct((t, d), BF16))
        out_specs.append(row)
    return pl.pallas_call(
        functools.partial(_ffn_kernel, post=post),
        grid=(t // tm, f // tf),
        in_specs=[
            row,
            vec,
            pl.BlockSpec((d, tf), lambda i, j: (0, j)),
            pl.BlockSpec((d, tf), lambda i, j: (0, j)),
            pl.BlockSpec((tf, d), lambda i, j: (j, 0)),
            vec,
        ],
        out_specs=out_specs,
        out_shape=out_shape,
        scratch_shapes=[pltpu.VMEM((tm, d), BF16)],
        compiler_params=_params(("parallel", "arbitrary")),
        name="ffn_" + post,
    )(x, g, wg, wu, wd, post_g)


def _proj_kernel(h_ref, w_ref, *refs, epilogue, n_q_tiles):
    acc = jnp.dot(h_ref[...], w_ref[...], preferred_element_type=F32)
    if epilogue == "rope":
        cos_ref, sin_lo_ref, sin_hi_ref, o_ref = refs
        scale = jnp.where(pl.program_id(1) < n_q_tiles, HEAD_DIM ** -0.5, 1.0).astype(F32)
        cos = cos_ref[...] * scale
        sin_lo = sin_lo_ref[...] * scale
        sin_hi = sin_hi_ref[...] * scale
        for h in range(acc.shape[1] // HEAD_DIM):
            t = acc[:, h * HEAD_DIM:(h + 1) * HEAD_DIM]
            rot = (t * cos
                   + pltpu.roll(t, HEAD_DIM - ROPE_HALF, 1) * sin_lo
                   + pltpu.roll(t, ROPE_HALF, 1) * sin_hi)
            o_ref[:, h * HEAD_DIM:(h + 1) * HEAD_DIM] = rot.astype(o_ref.dtype)
    else:
        (o_ref,) = refs
        if epilogue == "gelu":
            acc = 0.5 * acc * (1.0 + lax.erf(acc * (0.5 ** 0.5)))
        elif epilogue == "sigmoid":
            acc = jax.nn.sigmoid(acc)
        o_ref[...] = acc.astype(o_ref.dtype)


def _proj(hn, w, col0, ncols, *, epilogue, tables=(), n_q_tiles=0, tm=1024, tn=512):
    t, d = hn.shape
    c0 = col0 // tn
    in_specs = [
        pl.BlockSpec((tm, d), lambda i, j: (i, 0)),
        pl.BlockSpec((d, tn), lambda i, j: (0, c0 + j)),
    ]
    seq_blocks = SEQ // tm
    for _ in tables:
        in_specs.append(pl.BlockSpec((tm, V7X_LANES), lambda i, j: (i % seq_blocks, 0)))
    return pl.pallas_call(
        functools.partial(_proj_kernel, epilogue=epilogue, n_q_tiles=n_q_tiles),
        grid=(t // tm, ncols // tn),
        in_specs=in_specs,
        out_specs=pl.BlockSpec((tm, tn), lambda i, j: (i, j)),
        out_shape=jax.ShapeDtypeStruct((t, ncols), BF16),
        compiler_params=_params(("parallel", "arbitrary")),
        name="proj_" + epilogue,
    )(hn, w, *tables)


def _attn_kernel(q_ref, kp_ref, kc_ref, vp_ref, vc_ref, o_ref, st_ref, *, n_qb):
    i = pl.program_id(2)
    qi = lax.broadcasted_iota(jnp.int32, (ATT_BLOCK, 2 * ATT_BLOCK), 0)
    kj = lax.broadcasted_iota(jnp.int32, (ATT_BLOCK, 2 * ATT_BLOCK), 1)
    diff = qi + ATT_BLOCK - kj
    band = (diff >= 0) & (diff <= ATT_BLOCK)
    band_first = band & ((kj >= ATT_BLOCK) | (i > 0))
    lane = lax.broadcasted_iota(jnp.int32, (ATT_BLOCK, V7X_LANES), 1)
    ones = jnp.ones((2 * ATT_BLOCK, HEAD_DIM), BF16)

    for qb in range(n_qb):
        rows = slice(qb * ATT_BLOCK, (qb + 1) * ATT_BLOCK)
        prev_rows = slice((qb - 1) * ATT_BLOCK, qb * ATT_BLOCK)
        valid = band_first if qb == 0 else band
        st = jnp.zeros((ATT_BLOCK, V7X_LANES), F32)
        for h in range(HEADS_PER_GROUP):
            cols = slice(h * HEAD_DIM, (h + 1) * HEAD_DIM)
            q = q_ref[rows, cols]
            if qb == 0:
                k_prev, v_prev = kp_ref[:, cols], vp_ref[:, cols]
            else:
                k_prev, v_prev = kc_ref[prev_rows, cols], vc_ref[prev_rows, cols]
            k_win = jnp.concatenate([k_prev, kc_ref[rows, cols]], axis=0)
            v_win = jnp.concatenate([v_prev, vc_ref[rows, cols]], axis=0)
            s = lax.dot_general(q, k_win, (((1,), (1,)), ((), ())), preferred_element_type=F32)
            s = jnp.where(valid, s, -jnp.inf)
            m = jnp.max(s, axis=-1, keepdims=True)
            p = jnp.exp(s - m).astype(BF16)
            pv = jnp.dot(p, jnp.concatenate([v_win, ones], axis=1), preferred_element_type=F32)
            l_b = pv[:, HEAD_DIM:]
            o_ref[rows, cols] = pv[:, :HEAD_DIM] / l_b
            st = jnp.where(lane == h, m, st)
            st = jnp.where(lane == HEADS_PER_GROUP + h, l_b, st)
        st_ref[rows, :] = st


def _attention(qk, v, group, batch):
    d = DILATIONS[group]
    length = SEQ // d
    tl = min(length, 512)
    n_qb = tl // ATT_BLOCK
    qk_blocks = qk.shape[1] // GROUP_WIDTH
    v_blocks = v.shape[1] // GROUP_WIDTH
    qk_v = qk.reshape(batch, length, d * qk.shape[1])
    v_v = v.reshape(batch, length, d * v.shape[1])

    def prev_blk(i):
        return jnp.maximum(i * n_qb - 1, 0)

    q_spec = pl.BlockSpec((None, tl, GROUP_WIDTH), lambda b, r, i: (b, i, r * qk_blocks + group))
    kc_spec = pl.BlockSpec((None, tl, GROUP_WIDTH), lambda b, r, i: (b, i, r * qk_blocks + N_GROUPS + group))
    kp_spec = pl.BlockSpec((None, ATT_BLOCK, GROUP_WIDTH),
                           lambda b, r, i: (b, prev_blk(i), r * qk_blocks + N_GROUPS + group))
    vc_spec = pl.BlockSpec((None, tl, GROUP_WIDTH), lambda b, r, i: (b, i, r * v_blocks + group))
    vp_spec = pl.BlockSpec((None, ATT_BLOCK, GROUP_WIDTH), lambda b, r, i: (b, prev_blk(i), r * v_blocks + group))
    o, st = pl.pallas_call(
        functools.partial(_attn_kernel, n_qb=n_qb),
        grid=(batch, d, length // tl),
        in_specs=[q_spec, kp_spec, kc_spec, vp_spec, vc_spec],
        out_specs=[
            pl.BlockSpec((None, tl, GROUP_WIDTH), lambda b, r, i: (b, i, r)),
            pl.BlockSpec((None, tl, V7X_LANES), lambda b, r, i: (b, i, r)),
        ],
        out_shape=[
            jax.ShapeDtypeStruct((batch, length, d * GROUP_WIDTH), F32),
            jax.ShapeDtypeStruct((batch, length, d * V7X_LANES), F32),
        ],
        compiler_params=_params(("parallel", "parallel", "arbitrary")),
        name=f"attn_d{d}",
    )(qk_v, qk_v, qk_v, v_v, v_v)
    t = batch * SEQ
    return o.reshape(t, GROUP_WIDTH), st.reshape(t, V7X_LANES)


def _mix_kernel(o0_ref, o1_ref, o2_ref, s0_ref, s1_ref, s2_ref, u_ref, vs_ref, ga_ref, gs_ref,
                lng_ref, lnb_ref, sgw_ref, sgb_ref, wa_ref, ws_ref, out_ref, oatt_ref, gated_ref):
    tm = out_ref.shape[0]
    o_refs = (o0_ref, o1_ref, o2_ref)
    stats = (s0_ref[...], s1_ref[...], s2_ref[...])

    for h in range(HEADS_PER_GROUP):
        cols = slice(h * HEAD_DIM, (h + 1) * HEAD_DIM)
        ms = [s[:, h:h + 1] for s in stats]
        ls = [s[:, HEADS_PER_GROUP + h:HEADS_PER_GROUP + h + 1] for s in stats]
        m_all = jnp.maximum(jnp.maximum(ms[0], ms[1]), ms[2])
        ws = [l * jnp.exp(m - m_all) for l, m in zip(ls, ms)]
        num = ws[0] * o_refs[0][:, cols] + ws[1] * o_refs[1][:, cols] + ws[2] * o_refs[2][:, cols]
        oatt_ref[:, cols] = (num / (ws[0] + ws[1] + ws[2])).astype(BF16)
    y_att = jnp.dot(oatt_ref[...], wa_ref[...], preferred_element_type=F32)

    vs = vs_ref[...].astype(F32)
    mu = jnp.mean(vs, axis=-1, keepdims=True)
    var = jnp.mean(jnp.square(vs - mu), axis=-1, keepdims=True)
    vn = ((vs - mu) * lax.rsqrt(var + LN_EPS) * lng_ref[...] + lnb_ref[...]).astype(BF16)
    ti = lax.broadcasted_iota(jnp.int32, (SG_CHUNK, SG_CHUNK), 0)
    si = lax.broadcasted_iota(jnp.int32, (SG_CHUNK, SG_CHUNK), 1)
    causal = si <= ti
    for g in range(SG_GROUPS):
        cols = slice(g * SG_GROUP_DIM, (g + 1) * SG_GROUP_DIM)
        w_sp = jnp.where(causal, sgw_ref[g], 0.0).astype(BF16)
        bias = sgb_ref[:, g:g + 1]
        for c in range(tm // SG_CHUNK):
            rows = slice(c * SG_CHUNK, (c + 1) * SG_CHUNK)
            spatial = jnp.dot(w_sp, vn[rows, cols], preferred_element_type=F32) + bias
            gated_ref[rows, cols] = (u_ref[rows, cols].astype(F32) * spatial).astype(BF16)
    y_sg = jnp.dot(gated_ref[...], ws_ref[...], preferred_element_type=F32)

    merged = ga_ref[...].astype(F32) * y_att + gs_ref[...].astype(F32) * y_sg
    out_ref[...] = merged.astype(out_ref.dtype)


def _mix(o_list, st_list, uv, gates, ln_g, ln_b, sg_w, sg_b_t, w_att, w_sg, *, tm=512):
    t = uv.shape[0]
    row = lambda width, blk: pl.BlockSpec((tm, width), lambda i: (i, blk))
    const2 = lambda shape: pl.BlockSpec(shape, lambda i: (0, 0), pipeline_mode=pl.Buffered(1))
    in_specs = (
        [row(GROUP_WIDTH, 0)] * 3 + [row(V7X_LANES, 0)] * 3
        + [row(SG_WIDTH, 0), row(SG_WIDTH, 1), row(D_MODEL, 0), row(D_MODEL, 1)]
        + [const2((1, SG_WIDTH)), const2((1, SG_WIDTH)),
           pl.BlockSpec(sg_w.shape, lambda i: (0, 0, 0), pipeline_mode=pl.Buffered(1)),
           const2(sg_b_t.shape), const2(w_att.shape), const2(w_sg.shape)]
    )
    return pl.pallas_call(
        _mix_kernel,
        grid=(t // tm,),
        in_specs=in_specs,
        out_specs=pl.BlockSpec((tm, D_MODEL), lambda i: (i, 0)),
        out_shape=jax.ShapeDtypeStruct((t, D_MODEL), BF16),
        scratch_shapes=[pltpu.VMEM((tm, GROUP_WIDTH), BF16), pltpu.VMEM((tm, SG_WIDTH), BF16)],
        compiler_params=_params(("parallel",)),
        name="mix",
    )(*o_list, *st_list, uv, uv, gates, gates, ln_g, ln_b, sg_w, sg_b_t, w_att, w_sg)


def _out_proj_kernel(m_ref, w_ref, x_ref, o_ref):
    o_ref[...] = x_ref[...] + jnp.dot(m_ref[...], w_ref[...], preferred_element_type=F32)


def _out_proj(merged, w, x, *, tm=1024, tn=512):
    t, d = merged.shape
    n = w.shape[1]
    return pl.pallas_call(
        _out_proj_kernel,
        grid=(t // tm, n // tn),
        in_specs=[
            pl.BlockSpec((tm, d), lambda i, j: (i, 0)),
            pl.BlockSpec((d, tn), lambda i, j: (0, j)),
            pl.BlockSpec((tm, tn), lambda i, j: (i, j)),
        ],
        out_specs=pl.BlockSpec((tm, tn), lambda i, j: (i, j)),
        out_shape=jax.ShapeDtypeStruct((t, n), F32),
        compiler_params=_params(("parallel", "arbitrary")),
        name="out_proj",
    )(merged, w, x)


def kernel(x, ffn1_norm, ffn1_w_gate, ffn1_w_up, ffn1_w_down, mix_norm, w_in, sg_ln_g, sg_ln_b, sg_w, sg_b, w_att_out, w_sg_out, w_out, ffn2_norm, ffn2_w_gate, ffn2_w_up, ffn2_w_down, final_norm):
    batch, seq, d = x.shape
    assert (seq, d) == (SEQ, D_MODEL)
    t = batch * seq
    depth = ffn1_norm.shape[0]
    xt = x.reshape(t, d)
    cos_t, sin_lo_t, sin_hi_t = _rope_tables()
    bf = lambda a: a.astype(BF16)
    for i in range(depth):
        last = i == depth - 1
        x1, hn = _ffn(xt, ffn1_norm[i][None], bf(ffn1_w_gate[i]), bf(ffn1_w_up[i]), bf(ffn1_w_down[i]),
                      mix_norm[i][None], post="hn")
        w_in_b = bf(w_in[i])
        qk = _proj(hn, w_in_b, 0, 2 * ATT_WIDTH, epilogue="rope", tables=(cos_t, sin_lo_t, sin_hi_t),
                   n_q_tiles=ATT_WIDTH // 512)
        v = _proj(hn, w_in_b, 2 * ATT_WIDTH, ATT_WIDTH, epilogue="none")
        uv = _proj(hn, w_in_b, 3 * ATT_WIDTH, 2 * SG_WIDTH, epilogue="gelu")
        gates = _proj(hn, w_in_b, 3 * ATT_WIDTH + 2 * SG_WIDTH, 2 * D_MODEL, epilogue="sigmoid")
        o_list, st_list = [], []
        for g in range(N_GROUPS):
            o_g, st_g = _attention(qk, v, g, batch)
            o_list.append(o_g)
            st_list.append(st_g)
        merged = _mix(o_list, st_list, uv, gates, sg_ln_g[i][None], sg_ln_b[i][None], sg_w[i], sg_b[i].T,
                      bf(w_att_out[i]), bf(w_sg_out[i]))
        x2 = _out_proj(merged, bf(w_out[i]), x1)
        (xt,) = _ffn(x2, ffn2_norm[i][None], bf(ffn2_w_gate[i]), bf(ffn2_w_up[i]), bf(ffn2_w_down[i]),
                     final_norm[None], post="final" if last else "plain")
    return xt.reshape(batch, seq, d)
```

```python
import functools

import jax
import jax.numpy as jnp
from jax import lax
from jax.experimental import pallas as pl
from jax.experimental.pallas import tpu as pltpu

F32 = jnp.float32
BF16 = jnp.bfloat16

D_MODEL = 2048
SEQ = 4096
HEAD_DIM = 128
HEADS_PER_GROUP = 4
DILATIONS = (1, 4, 16)
N_GROUPS = len(DILATIONS)
N_ATT_HEADS = N_GROUPS * HEADS_PER_GROUP
ATT_WIDTH = N_ATT_HEADS * HEAD_DIM
ATT_BLOCK = 128
ATT_TILE = 2048
ROPE_DIM = HEAD_DIM // 4
ROPE_HALF = ROPE_DIM // 2
ROPE_THETA = 500000.0
SG_CHUNK = 128
SG_GROUPS = 12
SG_GROUP_DIM = 128
SG_WIDTH = SG_GROUPS * SG_GROUP_DIM
NORM_EPS = 1e-6
LN_EPS = 1e-5

V7X_VMEM_BYTES = 64 * 1024 * 1024
V7X_LANES = 128
VMEM_LIMIT = 52 * 1024 * 1024
FFN_VMEM_LIMIT = 60 * 1024 * 1024


def _params(semantics, vmem_limit=VMEM_LIMIT):
    return pltpu.CompilerParams(dimension_semantics=semantics, vmem_limit_bytes=vmem_limit)


def _rms(x, g):
    return x * lax.rsqrt(jnp.mean(x * x, axis=-1, keepdims=True) + NORM_EPS) * g


def _rope_table_kernel(freq_ref, cos_ref, sin_lo_ref, sin_hi_ref):
    rows = cos_ref.shape[0]
    pos = (pl.program_id(0) * rows + lax.broadcasted_iota(jnp.int32, (rows, V7X_LANES), 0)).astype(F32)
    lane = lax.broadcasted_iota(jnp.int32, (rows, V7X_LANES), 1)
    ang = pos * freq_ref[...]
    c = jnp.cos(ang)
    s = jnp.sin(ang)
    cos_ref[...] = jnp.where(lane < ROPE_DIM, c, 1.0)
    sin_lo_ref[...] = jnp.where(lane < ROPE_HALF, -s, 0.0)
    sin_hi_ref[...] = jnp.where((lane >= ROPE_HALF) & (lane < ROPE_DIM), s, 0.0)


def _rope_tables():
    inv_freq = ROPE_THETA ** (-jnp.arange(0, ROPE_DIM, 2, dtype=F32) / ROPE_DIM)
    freq_lane = jnp.concatenate([inv_freq, inv_freq, jnp.zeros((V7X_LANES - ROPE_DIM,), F32)])[None, :]
    rows = 512
    tab = jax.ShapeDtypeStruct((SEQ, V7X_LANES), F32)
    spec = pl.BlockSpec((rows, V7X_LANES), lambda i: (i, 0))
    return pl.pallas_call(
        _rope_table_kernel,
        grid=(SEQ // rows,),
        in_specs=[pl.BlockSpec((1, V7X_LANES), lambda i: (0, 0))],
        out_specs=[spec, spec, spec],
        out_shape=[tab, tab, tab],
        compiler_params=_params(("parallel",)),
        name="rope_table",
    )(freq_lane)


def _ffn_kernel(x_ref, g_ref, wg_ref, wu_ref, wd_ref, pg_ref, *refs, post):
    if post == "hn":
        o_ref, hn_ref, xn_ref = refs
    else:
        o_ref, xn_ref = refs
    j = pl.program_id(1)

    @pl.when(j == 0)
    def _():
        xn_ref[...] = _rms(x_ref[...], g_ref[...]).astype(BF16)
        o_ref[...] = jnp.zeros_like(o_ref)

    xn = xn_ref[...]
    h = jnp.dot(xn, wg_ref[...], preferred_element_type=F32)
    u = jnp.dot(xn, wu_ref[...], preferred_element_type=F32)
    a = (jax.nn.silu(h) * u).astype(BF16)
    o_ref[...] += jnp.dot(a, wd_ref[...], preferred_element_type=F32)

    @pl.when(j == pl.num_programs(1) - 1)
    def _():
        y = x_ref[...] + 0.5 * o_ref[...]
        if post == "hn":
            o_ref[...] = y
            hn_ref[...] = _rms(y, pg_ref[...]).astype(BF16)
        elif post == "final":
            o_ref[...] = _rms(y, pg_ref[...])
        else:
            o_ref[...] = y


def _ffn(x, g, wg, wu, wd, post_g, *, post, tm=1024, tf=512, vmem_limit=FFN_VMEM_LIMIT):
    t, d = x.shape
    f = wg.shape[1]
    row = pl.BlockSpec((tm, d), lambda i, j: (i, 0))
    vec = pl.BlockSpec((1, d), lambda i, j: (0, 0))
    out_shape = [jax.ShapeDtypeStruct((t, d), F32)]
    out_specs = [row]
    if post == "hn":
        out_shape.append(jax.ShapeDtypeStruct((t, d), BF16))
        out_specs.append(pl.BlockSpec((tm, d), lambda i, j: (i, 0), pipeline_mode=pl.Buffered(1)))
    return pl.pallas_call(
        functools.partial(_ffn_kernel, post=post),
        grid=(t // tm, f // tf),
        in_specs=[
            pl.BlockSpec((tm, d), lambda i, j: (i, 0), pipeline_mode=pl.Buffered(1)),
            vec,
            pl.BlockSpec((d, tf), lambda i, j: (0, j)),
            pl.BlockSpec((d, tf), lambda i, j: (0, j)),
            pl.BlockSpec((tf, d), lambda i, j: (j, 0)),
            vec,
        ],
        out_specs=out_specs,
        out_shape=out_shape,
        scratch_shapes=[pltpu.VMEM((tm, d), BF16)],
        compiler_params=_params(("parallel", "arbitrary"), vmem_limit),
        name="ffn_" + post,
    )(x, g, wg, wu, wd, post_g)


def _proj_kernel(h_ref, w_ref, *refs, epilogue):
    acc = jnp.dot(h_ref[...], w_ref[...], preferred_element_type=F32)
    n_heads = acc.shape[1] // HEAD_DIM
    if epilogue == "rope":
        cos_ref, sin_lo_ref, sin_hi_ref, o_ref = refs
        scale = jnp.where(pl.program_id(1) == 0, HEAD_DIM ** -0.5, 1.0).astype(F32)
        cos = cos_ref[...] * scale
        sin_lo = sin_lo_ref[...] * scale
        sin_hi = sin_hi_ref[...] * scale
        for h in range(n_heads):
            t = acc[:, h * HEAD_DIM:(h + 1) * HEAD_DIM]
            o_ref[h] = (t * cos
                        + pltpu.roll(t, HEAD_DIM - ROPE_HALF, 1) * sin_lo
                        + pltpu.roll(t, ROPE_HALF, 1) * sin_hi)
    elif epilogue == "heads":
        (o_ref,) = refs
        for h in range(n_heads):
            o_ref[h] = acc[:, h * HEAD_DIM:(h + 1) * HEAD_DIM]
    else:
        (o_ref,) = refs
        if epilogue == "gelu":
            acc = 0.5 * acc * (1.0 + lax.erf(acc * (0.5 ** 0.5)))
        elif epilogue == "sigmoid":
            acc = jax.nn.sigmoid(acc)
        o_ref[...] = acc.astype(o_ref.dtype)


def _proj(hn, w, *, epilogue, tn, tables=(), tm=1024):
    t, d = hn.shape
    ncols = w.shape[1]
    in_specs = [
        pl.BlockSpec((tm, d), lambda i, j: (i, 0)),
        pl.BlockSpec((d, tn), lambda i, j: (0, j)),
    ]
    seq_blocks = SEQ // tm
    for _ in tables:
        in_specs.append(pl.BlockSpec((tm, V7X_LANES), lambda i, j: (i % seq_blocks, 0)))
    if epilogue in ("rope", "heads"):
        heads = tn // HEAD_DIM
        out_spec = pl.BlockSpec((heads, tm, HEAD_DIM), lambda i, j: (j, i, 0))
        out_shape = jax.ShapeDtypeStruct((ncols // HEAD_DIM, t, HEAD_DIM), F32)
    else:
        out_spec = pl.BlockSpec((tm, tn), lambda i, j: (i, j))
        out_shape = jax.ShapeDtypeStruct((t, ncols), BF16)
    return pl.pallas_call(
        functools.partial(_proj_kernel, epilogue=epilogue),
        grid=(t // tm, ncols // tn),
        in_specs=in_specs,
        out_specs=out_spec,
        out_shape=out_shape,
        compiler_params=_params(("parallel", "arbitrary")),
        name="proj_" + epilogue,
    )(hn, w, *tables)


def _attn_kernel(q_ref, kp_ref, kc_ref, vp_ref, vc_ref, o_ref, st_ref, *, dil):
    tile = pl.program_id(1)
    head = pl.program_id(2)
    n_qb = ATT_TILE // (ATT_BLOCK * dil)

    def rows(start):
        return pl.ds(start, ATT_BLOCK, stride=dil) if dil > 1 else pl.ds(start, ATT_BLOCK)

    qi = lax.broadcasted_iota(jnp.int32, (ATT_BLOCK, 2 * ATT_BLOCK), 0)
    kj = lax.broadcasted_iota(jnp.int32, (ATT_BLOCK, 2 * ATT_BLOCK), 1)
    diff = qi + ATT_BLOCK - kj
    band = (diff >= 0) & (diff <= ATT_BLOCK)
    band_first = band & ((kj >= ATT_BLOCK) | (tile > 0))
    lane = lax.broadcasted_iota(jnp.int32, (ATT_BLOCK, V7X_LANES), 1)
    ones = jnp.ones((2 * ATT_BLOCK, HEAD_DIM), BF16)

    @pl.when(head == 0)
    def _():
        st_ref[...] = jnp.zeros_like(st_ref)

    for r in range(dil):
        k_prev = kp_ref[rows(r), :].astype(BF16)
        v_prev = vp_ref[rows(r), :].astype(BF16)
        for qb in range(n_qb):
            sel = rows(qb * ATT_BLOCK * dil + r)
            q = q_ref[sel, :].astype(BF16)
            k_cur = kc_ref[sel, :].astype(BF16)
            v_cur = vc_ref[sel, :].astype(BF16)
            k_win = jnp.concatenate([k_prev, k_cur], axis=0)
            v_win = jnp.concatenate([v_prev, v_cur], axis=0)
            s = lax.dot_general(q, k_win, (((1,), (1,)), ((), ())), preferred_element_type=F32)
            s = jnp.where(band_first if qb == 0 else band, s, -jnp.inf)
            m = jnp.max(s, axis=-1, keepdims=True)
            p = jnp.exp(s - m).astype(BF16)
            pv = jnp.dot(p, jnp.concatenate([v_win, ones], axis=1), preferred_element_type=F32)
            l_b = pv[:, HEAD_DIM:]
            o_ref[sel, :] = pv[:, :HEAD_DIM] / l_b
            st = st_ref[sel, :]
            st = jnp.where(lane == head, m, st)
            st_ref[sel, :] = jnp.where(lane == HEADS_PER_GROUP + head, l_b, st)
            k_prev, v_prev = k_cur, v_cur


def _attention(qk, v, group, batch):
    dil = DILATIONS[group]
    t = qk.shape[1]
    n_tiles = SEQ // ATT_TILE
    prev_rows = ATT_BLOCK * dil
    prev_per_tile = ATT_TILE // prev_rows
    h0 = group * HEADS_PER_GROUP

    def cur(slab0):
        return pl.BlockSpec((None, ATT_TILE, HEAD_DIM), lambda b, i, h: (slab0 + h, b * n_tiles + i, 0))

    def prev(slab0):
        return pl.BlockSpec((None, prev_rows, HEAD_DIM),
                            lambda b, i, h: (slab0 + h, jnp.maximum((b * n_tiles + i) * prev_per_tile - 1, 0), 0))

    return pl.pallas_call(
        functools.partial(_attn_kernel, dil=dil),
        grid=(batch, n_tiles, HEADS_PER_GROUP),
        in_specs=[cur(h0), prev(N_ATT_HEADS + h0), cur(N_ATT_HEADS + h0), prev(h0), cur(h0)],
        out_specs=[
            pl.BlockSpec((None, ATT_TILE, HEAD_DIM), lambda b, i, h: (h, b * n_tiles + i, 0)),
            pl.BlockSpec((ATT_TILE, V7X_LANES), lambda b, i, h: (b * n_tiles + i, 0)),
        ],
        out_shape=[
            jax.ShapeDtypeStruct((HEADS_PER_GROUP, t, HEAD_DIM), F32),
            jax.ShapeDtypeStruct((t, V7X_LANES), F32),
        ],
        compiler_params=_params(("parallel", "parallel", "arbitrary")),
        name=f"attn_d{dil}",
    )(qk, qk, qk, v, v)


def _mix_kernel(o0_ref, o1_ref, o2_ref, s0_ref, s1_ref, s2_ref, u_ref, vs_ref, ga_ref, gs_ref,
                lng_ref, lnb_ref, sgw_ref, sgb_ref, wa_ref, ws_ref, out_ref, oatt_ref, gated_ref):
    tm = out_ref.shape[0]
    o_refs = (o0_ref, o1_ref, o2_ref)
    stats = (s0_ref[...], s1_ref[...], s2_ref[...])

    for h in range(HEADS_PER_GROUP):
        ms = [s[:, h:h + 1] for s in stats]
        ls = [s[:, HEADS_PER_GROUP + h:HEADS_PER_GROUP + h + 1] for s in stats]
        m_all = jnp.maximum(jnp.maximum(ms[0], ms[1]), ms[2])
        ws = [l * jnp.exp(m - m_all) for l, m in zip(ls, ms)]
        num = ws[0] * o_refs[0][h] + ws[1] * o_refs[1][h] + ws[2] * o_refs[2][h]
        oatt_ref[:, h * HEAD_DIM:(h + 1) * HEAD_DIM] = (num / (ws[0] + ws[1] + ws[2])).astype(BF16)
    y_att = jnp.dot(oatt_ref[...], wa_ref[...], preferred_element_type=F32)

    vs = vs_ref[...].astype(F32)
    mu = jnp.mean(vs, axis=-1, keepdims=True)
    var = jnp.mean(jnp.square(vs - mu), axis=-1, keepdims=True)
    vn = ((vs - mu) * lax.rsqrt(var + LN_EPS) * lng_ref[...] + lnb_ref[...]).astype(BF16)
    ti = lax.broadcasted_iota(jnp.int32, (SG_CHUNK, SG_CHUNK), 0)
    si = lax.broadcasted_iota(jnp.int32, (SG_CHUNK, SG_CHUNK), 1)
    causal = si <= ti
    for g in range(SG_GROUPS):
        cols = slice(g * SG_GROUP_DIM, (g + 1) * SG_GROUP_DIM)
        w_sp = jnp.where(causal, sgw_ref[g], 0.0).astype(BF16)
        bias = sgb_ref[:, g:g + 1]
        for c in range(tm // SG_CHUNK):
            rows = slice(c * SG_CHUNK, (c + 1) * SG_CHUNK)
            spatial = jnp.dot(w_sp, vn[rows, cols], preferred_element_type=F32) + bias
            gated_ref[rows, cols] = (u_ref[rows, cols].astype(F32) * spatial).astype(BF16)
    y_sg = jnp.dot(gated_ref[...], ws_ref[...], preferred_element_type=F32)

    merged = ga_ref[...].astype(F32) * y_att + gs_ref[...].astype(F32) * y_sg
    out_ref[...] = merged.astype(out_ref.dtype)


def _mix(o_list, st_list, uv, gates, ln_g, ln_b, sg_w, sg_b_t, w_att, w_sg, *, tm=512):
    t = uv.shape[0]
    row = lambda width, blk: pl.BlockSpec((tm, width), lambda i: (i, blk))
    const2 = lambda shape: pl.BlockSpec(shape, lambda i: (0, 0), pipeline_mode=pl.Buffered(1))
    in_specs = (
        [pl.BlockSpec((HEADS_PER_GROUP, tm, HEAD_DIM), lambda i: (0, i, 0))] * 3 + [row(V7X_LANES, 0)] * 3
        + [row(SG_WIDTH, 0), row(SG_WIDTH, 1), row(D_MODEL, 0), row(D_MODEL, 1)]
        + [const2((1, SG_WIDTH)), const2((1, SG_WIDTH)),
           pl.BlockSpec(sg_w.shape, lambda i: (0, 0, 0), pipeline_mode=pl.Buffered(1)),
           const2(sg_b_t.shape), const2(w_att.shape), const2(w_sg.shape)]
    )
    return pl.pallas_call(
        _mix_kernel,
        grid=(t // tm,),
        in_specs=in_specs,
        out_specs=pl.BlockSpec((tm, D_MODEL), lambda i: (i, 0)),
        out_shape=jax.ShapeDtypeStruct((t, D_MODEL), BF16),
        scratch_shapes=[pltpu.VMEM((tm, HEADS_PER_GROUP * HEAD_DIM), BF16), pltpu.VMEM((tm, SG_WIDTH), BF16)],
        compiler_params=_params(("parallel",)),
        name="mix",
    )(*o_list, *st_list, uv, uv, gates, gates, ln_g, ln_b, sg_w, sg_b_t, w_att, w_sg)


def _out_proj_kernel(m_ref, w_ref, x_ref, o_ref):
    o_ref[...] = x_ref[...] + jnp.dot(m_ref[...], w_ref[...], preferred_element_type=F32)


def _out_proj(merged, w, x, *, tm=1024, tn=1024):
    t, d = merged.shape
    n = w.shape[1]
    return pl.pallas_call(
        _out_proj_kernel,
        grid=(t // tm, n // tn),
        in_specs=[
            pl.BlockSpec((tm, d), lambda i, j: (i, 0)),
            pl.BlockSpec((d, tn), lambda i, j: (0, j)),
            pl.BlockSpec((tm, tn), lambda i, j: (i, j)),
        ],
        out_specs=pl.BlockSpec((tm, tn), lambda i, j: (i, j)),
        out_shape=jax.ShapeDtypeStruct((t, n), F32),
        compiler_params=_params(("parallel", "arbitrary")),
        name="out_proj",
    )(merged, w, x)


def kernel(x, ffn1_norm, ffn1_w_gate, ffn1_w_up, ffn1_w_down, mix_norm, w_in, sg_ln_g, sg_ln_b, sg_w, sg_b, w_att_out, w_sg_out, w_out, ffn2_norm, ffn2_w_gate, ffn2_w_up, ffn2_w_down, final_norm):
    batch, seq, d = x.shape
    assert (seq, d) == (SEQ, D_MODEL)
    t = batch * seq
    depth = ffn1_norm.shape[0]
    xt = x.reshape(t, d)
    tables = _rope_tables()
    bf = lambda a: a.astype(BF16)
    for i in range(depth):
        last = i == depth - 1
        x1, hn = _ffn(xt, ffn1_norm[i][None], bf(ffn1_w_gate[i]), bf(ffn1_w_up[i]), bf(ffn1_w_down[i]),
                      mix_norm[i][None], post="hn")
        c_v, c_u, c_g = 2 * ATT_WIDTH, 3 * ATT_WIDTH, 3 * ATT_WIDTH + 2 * SG_WIDTH
        qk = _proj(hn, bf(w_in[i][:, :c_v]), epilogue="rope", tn=ATT_WIDTH, tables=tables)
        v = _proj(hn, bf(w_in[i][:, c_v:c_u]), epilogue="heads", tn=ATT_WIDTH)
        uv = _proj(hn, bf(w_in[i][:, c_u:c_g]), epilogue="gelu", tn=SG_WIDTH)
        gates = _proj(hn, bf(w_in[i][:, c_g:]), epilogue="sigmoid", tn=D_MODEL)
        o_list, st_list = [], []
        for g in range(N_GROUPS):
            o_g, st_g = _attention(qk, v, g, batch)
            o_list.append(o_g)
            st_list.append(st_g)
        merged = _mix(o_list, st_list, uv, gates, sg_ln_g[i][None], sg_ln_b[i][None], sg_w[i], sg_b[i].T,
                      bf(w_att_out[i]), bf(w_sg_out[i]))
        x2 = _out_proj(merged, bf(w_out[i]), x1)
        (xt,) = _ffn(x2, ffn2_norm[i][None], bf(ffn2_w_gate[i]), bf(ffn2_w_up[i]), bf(ffn2_w_down[i]),
                     final_norm[None], post="final" if last else "plain")
    return xt.reshape(batch, seq, d)
```

```python
import functools

import jax
import jax.numpy as jnp
from jax import lax
from jax.experimental import pallas as pl
from jax.experimental.pallas import tpu as pltpu

F32 = jnp.float32
BF16 = jnp.bfloat16

D_MODEL = 2048
SEQ = 4096
HEAD_DIM = 128
HEADS_PER_GROUP = 4
DILATIONS = (1, 4, 16)
N_GROUPS = len(DILATIONS)
N_ATT_HEADS = N_GROUPS * HEADS_PER_GROUP
ATT_WIDTH = N_ATT_HEADS * HEAD_DIM
ATT_BLOCK = 128
ROPE_DIM = HEAD_DIM // 4
ROPE_HALF = ROPE_DIM // 2
ROPE_THETA = 500000.0
SG_CHUNK = 128
SG_GROUPS = 12
SG_GROUP_DIM = 128
SG_WIDTH = SG_GROUPS * SG_GROUP_DIM
NORM_EPS = 1e-6
LN_EPS = 1e-5

V7X_VMEM_BYTES = 64 * 1024 * 1024
V7X_LANES = 128
BF16_ROWS = 16
VMEM_LIMIT = 52 * 1024 * 1024


def _params(semantics, vmem_limit=VMEM_LIMIT):
    return pltpu.CompilerParams(dimension_semantics=semantics, vmem_limit_bytes=vmem_limit)


def _rms(x, g):
    return x * lax.rsqrt(jnp.mean(x * x, axis=-1, keepdims=True) + NORM_EPS) * g


def _rope_table_kernel(freq_ref, cos_ref, sin_lo_ref, sin_hi_ref):
    rows = cos_ref.shape[0]
    pos = (pl.program_id(0) * rows + lax.broadcasted_iota(jnp.int32, (rows, V7X_LANES), 0)).astype(F32)
    lane = lax.broadcasted_iota(jnp.int32, (rows, V7X_LANES), 1)
    ang = pos * freq_ref[...]
    c = jnp.cos(ang)
    s = jnp.sin(ang)
    cos_ref[...] = jnp.where(lane < ROPE_DIM, c, 1.0)
    sin_lo_ref[...] = jnp.where(lane < ROPE_HALF, -s, 0.0)
    sin_hi_ref[...] = jnp.where((lane >= ROPE_HALF) & (lane < ROPE_DIM), s, 0.0)


def _rope_tables():
    inv_freq = ROPE_THETA ** (-jnp.arange(0, ROPE_DIM, 2, dtype=F32) / ROPE_DIM)
    freq_lane = jnp.concatenate([inv_freq, inv_freq, jnp.zeros((V7X_LANES - ROPE_DIM,), F32)])[None, :]
    rows = 512
    tab = jax.ShapeDtypeStruct((SEQ, V7X_LANES), F32)
    spec = pl.BlockSpec((rows, V7X_LANES), lambda i: (i, 0))
    return pl.pallas_call(
        _rope_table_kernel,
        grid=(SEQ // rows,),
        in_specs=[pl.BlockSpec((1, V7X_LANES), lambda i: (0, 0))],
        out_specs=[spec, spec, spec],
        out_shape=[tab, tab, tab],
        compiler_params=_params(("parallel",)),
        name="rope_table",
    )(freq_lane)


def _ffn_kernel(x_ref, g_ref, wg_ref, wu_ref, wd_ref, pg_ref, *refs, post):
    if post == "hn":
        o_ref, hn_ref, xn_ref = refs
    else:
        o_ref, xn_ref = refs
    j = pl.program_id(1)

    @pl.when(j == 0)
    def _():
        xn_ref[...] = _rms(x_ref[...], g_ref[...]).astype(BF16)
        o_ref[...] = jnp.zeros_like(o_ref)

    xn = xn_ref[...]
    h = jnp.dot(xn, wg_ref[...], preferred_element_type=F32)
    u = jnp.dot(xn, wu_ref[...], preferred_element_type=F32)
    a = (jax.nn.silu(h) * u).astype(BF16)
    o_ref[...] += jnp.dot(a, wd_ref[...], preferred_element_type=F32)

    @pl.when(j == pl.num_programs(1) - 1)
    def _():
        y = x_ref[...] + 0.5 * o_ref[...]
        if post == "hn":
            o_ref[...] = y
            hn_ref[...] = _rms(y, pg_ref[...]).astype(BF16)
        elif post == "final":
            o_ref[...] = _rms(y, pg_ref[...])
        else:
            o_ref[...] = y


def _ffn(x, g, wg, wu, wd, post_g, *, post, tm=512, tf=512):
    t, d = x.shape
    f = wg.shape[1]
    row = pl.BlockSpec((tm, d), lambda i, j: (i, 0))
    vec = pl.BlockSpec((1, d), lambda i, j: (0, 0))
    out_shape = [jax.ShapeDtypeStruct((t, d), F32)]
    out_specs = [row]
    if post == "hn":
        out_shape.append(jax.ShapeDtypeStruct((t, d), BF16))
        out_specs.append(row)
    return pl.pallas_call(
        functools.partial(_ffn_kernel, post=post),
        grid=(t // tm, f // tf),
        in_specs=[
            row,
            vec,
            pl.BlockSpec((d, tf), lambda i, j: (0, j)),
            pl.BlockSpec((d, tf), lambda i, j: (0, j)),
            pl.BlockSpec((tf, d), lambda i, j: (j, 0)),
            vec,
        ],
        out_specs=out_specs,
        out_shape=out_shape,
        scratch_shapes=[pltpu.VMEM((tm, d), BF16)],
        compiler_params=_params(("parallel", "arbitrary")),
        name="ffn_" + post,
    )(x, g, wg, wu, wd, post_g)


def _proj_kernel(h_ref, w_ref, *refs, epilogue):
    acc = jnp.dot(h_ref[...], w_ref[...], preferred_element_type=F32)
    n_heads = acc.shape[1] // HEAD_DIM
    if epilogue == "rope":
        cos_ref, sin_lo_ref, sin_hi_ref, o_ref = refs
        scale = jnp.where(pl.program_id(1) == 0, HEAD_DIM ** -0.5, 1.0).astype(F32)
        cos = cos_ref[...] * scale
        sin_lo = sin_lo_ref[...] * scale
        sin_hi = sin_hi_ref[...] * scale
        for h in range(n_heads):
            t = acc[:, h * HEAD_DIM:(h + 1) * HEAD_DIM]
            rot = (t * cos
                   + pltpu.roll(t, HEAD_DIM - ROPE_HALF, 1) * sin_lo
                   + pltpu.roll(t, ROPE_HALF, 1) * sin_hi)
            o_ref[h] = pltpu.bitcast(rot.astype(BF16), jnp.uint32)
    elif epilogue == "heads":
        (o_ref,) = refs
        for h in range(n_heads):
            o_ref[h] = pltpu.bitcast(acc[:, h * HEAD_DIM:(h + 1) * HEAD_DIM].astype(BF16), jnp.uint32)
    else:
        (o_ref,) = refs
        if epilogue == "gelu":
            acc = 0.5 * acc * (1.0 + lax.erf(acc * (0.5 ** 0.5)))
        elif epilogue == "sigmoid":
            acc = jax.nn.sigmoid(acc)
        o_ref[...] = acc.astype(o_ref.dtype)


def _proj(hn, w, *, epilogue, tn, tables=(), tm=1024):
    t, d = hn.shape
    ncols = w.shape[1]
    in_specs = [
        pl.BlockSpec((tm, d), lambda i, j: (i, 0)),
        pl.BlockSpec((d, tn), lambda i, j: (0, j)),
    ]
    seq_blocks = SEQ // tm
    for _ in tables:
        in_specs.append(pl.BlockSpec((tm, V7X_LANES), lambda i, j: (i % seq_blocks, 0)))
    if epilogue in ("rope", "heads"):
        heads = tn // HEAD_DIM
        out_spec = pl.BlockSpec((heads, tm // 2, HEAD_DIM), lambda i, j: (j, i, 0))
        out_shape = jax.ShapeDtypeStruct((ncols // HEAD_DIM, t // 2, HEAD_DIM), jnp.uint32)
    else:
        out_spec = pl.BlockSpec((tm, tn), lambda i, j: (i, j))
        out_shape = jax.ShapeDtypeStruct((t, ncols), BF16)
    return pl.pallas_call(
        functools.partial(_proj_kernel, epilogue=epilogue),
        grid=(t // tm, ncols // tn),
        in_specs=in_specs,
        out_specs=out_spec,
        out_shape=out_shape,
        compiler_params=_params(("parallel", "arbitrary")),
        name="proj_" + epilogue,
    )(hn, w, *tables)


def _attn_kernel(q_ref, k_ref, v_ref, o_ref, st_ref, *, dil):
    head = pl.program_id(1)
    n_qb = SEQ // (ATT_BLOCK * dil)

    qi = lax.broadcasted_iota(jnp.int32, (ATT_BLOCK, 2 * ATT_BLOCK), 0)
    kj = lax.broadcasted_iota(jnp.int32, (ATT_BLOCK, 2 * ATT_BLOCK), 1)
    diff = qi + ATT_BLOCK - kj
    band = (diff >= 0) & (diff <= ATT_BLOCK)
    band_first = band & (kj >= ATT_BLOCK)
    lane = lax.broadcasted_iota(jnp.int32, (ATT_BLOCK, V7X_LANES), 1)
    ones = jnp.ones((2 * ATT_BLOCK, HEAD_DIM), BF16)
    zeros = jnp.zeros((ATT_BLOCK, HEAD_DIM), BF16)

    @pl.when(head == 0)
    def _():
        st_ref[...] = jnp.zeros_like(st_ref)

    if dil == 1:
        n_res = 1

        def load(ref, qb, r):
            return (pltpu.bitcast(ref[pl.ds(qb * ATT_BLOCK // 2, ATT_BLOCK // 2), :], BF16),)
    else:
        n_res = 2

        def load(ref, qb, r):
            w = ref[pl.ds((qb * ATT_BLOCK * dil + r) // 2, ATT_BLOCK, stride=dil // 2), :]
            even = pltpu.bitcast(w << 16, F32)
            odd = pltpu.bitcast(w & jnp.uint32(0xFFFF0000), F32)
            return even.astype(BF16), odd.astype(BF16)

    for r in range(0, dil, n_res):
        k_prev = [zeros] * n_res
        v_prev = [zeros] * n_res
        for qb in range(n_qb):
            qs, ks, vs = load(q_ref, qb, r), load(k_ref, qb, r), load(v_ref, qb, r)
            for e in range(n_res):
                sel = (pl.ds(qb * ATT_BLOCK * dil + r + e, ATT_BLOCK, stride=dil) if dil > 1
                       else pl.ds(qb * ATT_BLOCK, ATT_BLOCK))
                k_win = jnp.concatenate([k_prev[e], ks[e]], axis=0)
                v_win = jnp.concatenate([v_prev[e], vs[e]], axis=0)
                s = lax.dot_general(qs[e], k_win, (((1,), (1,)), ((), ())), preferred_element_type=F32)
                s = jnp.where(band_first if qb == 0 else band, s, -jnp.inf)
                m = jnp.max(s, axis=-1, keepdims=True)
                p = jnp.exp(s - m).astype(BF16)
                pv = jnp.dot(p, jnp.concatenate([v_win, ones], axis=1), preferred_element_type=F32)
                l_b = pv[:, HEAD_DIM:]
                o_ref[sel, :] = pv[:, :HEAD_DIM] / l_b
                st = st_ref[sel, :]
                st = jnp.where(lane == head, m, st)
                st_ref[sel, :] = jnp.where(lane == HEADS_PER_GROUP + head, l_b, st)
            k_prev, v_prev = ks, vs


def _attention(qk, v, group, batch):
    dil = DILATIONS[group]
    t = 2 * qk.shape[1]
    h0 = group * HEADS_PER_GROUP

    def seq(slab0):
        return pl.BlockSpec((None, SEQ // 2, HEAD_DIM), lambda b, h: (slab0 + h, b, 0))

    return pl.pallas_call(
        functools.partial(_attn_kernel, dil=dil),
        grid=(batch, HEADS_PER_GROUP),
        in_specs=[seq(h0), seq(N_ATT_HEADS + h0), seq(h0)],
        out_specs=[
            pl.BlockSpec((None, SEQ, HEAD_DIM), lambda b, h: (h, b, 0)),
            pl.BlockSpec((SEQ, V7X_LANES), lambda b, h: (b, 0)),
        ],
        out_shape=[
            jax.ShapeDtypeStruct((HEADS_PER_GROUP, t, HEAD_DIM), F32),
            jax.ShapeDtypeStruct((t, V7X_LANES), F32),
        ],
        compiler_params=_params(("parallel", "arbitrary")),
        name=f"attn_d{dil}",
    )(qk, qk, v)


def _mix_prepare(o_refs, stats, u_ref, vs_ref, lng_ref, lnb_ref, sgw_ref, sgb_ref, oatt_ref, gated_ref):
    tm = oatt_ref.shape[0]
    for h in range(HEADS_PER_GROUP):
        ms = [s[:, h:h + 1] for s in stats]
        ls = [s[:, HEADS_PER_GROUP + h:HEADS_PER_GROUP + h + 1] for s in stats]
        m_all = jnp.maximum(jnp.maximum(ms[0], ms[1]), ms[2])
        ws = [l * jnp.exp(m - m_all) for l, m in zip(ls, ms)]
        num = ws[0] * o_refs[0][h] + ws[1] * o_refs[1][h] + ws[2] * o_refs[2][h]
        oatt_ref[:, h * HEAD_DIM:(h + 1) * HEAD_DIM] = (num / (ws[0] + ws[1] + ws[2])).astype(BF16)

    vs = vs_ref[...].astype(F32)
    mu = jnp.mean(vs, axis=-1, keepdims=True)
    var = jnp.mean(jnp.square(vs - mu), axis=-1, keepdims=True)
    vn = ((vs - mu) * lax.rsqrt(var + LN_EPS) * lng_ref[...] + lnb_ref[...]).astype(BF16)
    ti = lax.broadcasted_iota(jnp.int32, (SG_CHUNK, SG_CHUNK), 0)
    si = lax.broadcasted_iota(jnp.int32, (SG_CHUNK, SG_CHUNK), 1)
    causal = si <= ti
    for g in range(SG_GROUPS):
        cols = slice(g * SG_GROUP_DIM, (g + 1) * SG_GROUP_DIM)
        w_sp = jnp.where(causal, sgw_ref[g], 0.0).astype(BF16)
        bias = sgb_ref[:, g:g + 1]
        for c in range(tm // SG_CHUNK):
            rows = slice(c * SG_CHUNK, (c + 1) * SG_CHUNK)
            spatial = jnp.dot(w_sp, vn[rows, cols], preferred_element_type=F32) + bias
            gated_ref[rows, cols] = (u_ref[rows, cols].astype(F32) * spatial).astype(BF16)


def _mix_kernel(o0_ref, o1_ref, o2_ref, s0_ref, s1_ref, s2_ref, u_ref, vs_ref, ga_ref, gs_ref,
                lng_ref, lnb_ref, sgw_ref, sgb_ref, wa_ref, ws_ref, out_ref,
                oatt_a, gated_a, oatt_b, gated_b):
    i = pl.program_id(0)

    @pl.when(i == 0)
    def _():
        oatt_b[...] = jnp.zeros_like(oatt_b)
        gated_b[...] = jnp.zeros_like(gated_b)

    def step(oatt_w, gated_w, oatt_r, gated_r):
        y_att = jnp.dot(oatt_r[...], wa_ref[...], preferred_element_type=F32)
        y_sg = jnp.dot(gated_r[...], ws_ref[...], preferred_element_type=F32)
        merged = ga_ref[...].astype(F32) * y_att + gs_ref[...].astype(F32) * y_sg
        out_ref[...] = merged.astype(out_ref.dtype)
        _mix_prepare((o0_ref, o1_ref, o2_ref), (s0_ref[...], s1_ref[...], s2_ref[...]), u_ref, vs_ref,
                     lng_ref, lnb_ref, sgw_ref, sgb_ref, oatt_w, gated_w)

    @pl.when(i % 2 == 0)
    def _():
        step(oatt_a, gated_a, oatt_b, gated_b)

    @pl.when(i % 2 == 1)
    def _():
        step(oatt_b, gated_b, oatt_a, gated_a)


def _mix(o_list, st_list, uv, gates, ln_g, ln_b, sg_w, sg_b_t, w_att, w_sg, *, tm=512):
    t = uv.shape[0]
    n = t // tm
    cur = lambda i: jnp.minimum(i, n - 1)
    lag = lambda i: jnp.maximum(i - 1, 0)
    prep_row = lambda width, blk: pl.BlockSpec((tm, width), lambda i: (cur(i), blk))
    lag_row = lambda width, blk: pl.BlockSpec((tm, width), lambda i: (lag(i), blk))
    const2 = lambda shape: pl.BlockSpec(shape, lambda i: (0, 0), pipeline_mode=pl.Buffered(1))
    in_specs = (
        [pl.BlockSpec((HEADS_PER_GROUP, tm, HEAD_DIM), lambda i: (0, cur(i), 0))] * 3
        + [prep_row(V7X_LANES, 0)] * 3
        + [prep_row(SG_WIDTH, 0), prep_row(SG_WIDTH, 1), lag_row(D_MODEL, 0), lag_row(D_MODEL, 1)]
        + [const2((1, SG_WIDTH)), const2((1, SG_WIDTH)),
           pl.BlockSpec(sg_w.shape, lambda i: (0, 0, 0), pipeline_mode=pl.Buffered(1)),
           const2(sg_b_t.shape), const2(w_att.shape), const2(w_sg.shape)]
    )
    att_w = HEADS_PER_GROUP * HEAD_DIM
    return pl.pallas_call(
        _mix_kernel,
        grid=(n + 1,),
        in_specs=in_specs,
        out_specs=pl.BlockSpec((tm, D_MODEL), lambda i: (lag(i), 0)),
        out_shape=jax.ShapeDtypeStruct((t, D_MODEL), BF16),
        scratch_shapes=[pltpu.VMEM((tm, att_w), BF16), pltpu.VMEM((tm, SG_WIDTH), BF16),
                        pltpu.VMEM((tm, att_w), BF16), pltpu.VMEM((tm, SG_WIDTH), BF16)],
        compiler_params=_params(("arbitrary",)),
        name="mix",
    )(*o_list, *st_list, uv, uv, gates, gates, ln_g, ln_b, sg_w, sg_b_t, w_att, w_sg)


def _out_proj_kernel(m_ref, w_ref, x_ref, o_ref):
    o_ref[...] = x_ref[...] + jnp.dot(m_ref[...], w_ref[...], preferred_element_type=F32)


def _out_proj(merged, w, x, *, tm=1024, tn=1024):
    t, d = merged.shape
    n = w.shape[1]
    return pl.pallas_call(
        _out_proj_kernel,
        grid=(t // tm, n // tn),
        in_specs=[
            pl.BlockSpec((tm, d), lambda i, j: (i, 0)),
            pl.BlockSpec((d, tn), lambda i, j: (0, j)),
            pl.BlockSpec((tm, tn), lambda i, j: (i, j)),
        ],
        out_specs=pl.BlockSpec((tm, tn), lambda i, j: (i, j)),
        out_shape=jax.ShapeDtypeStruct((t, n), F32),
        compiler_params=_params(("parallel", "arbitrary")),
        name="out_proj",
    )(merged, w, x)


def kernel(x, ffn1_norm, ffn1_w_gate, ffn1_w_up, ffn1_w_down, mix_norm, w_in, sg_ln_g, sg_ln_b, sg_w, sg_b, w_att_out, w_sg_out, w_out, ffn2_norm, ffn2_w_gate, ffn2_w_up, ffn2_w_down, final_norm):
    batch, seq, d = x.shape
    assert (seq, d) == (SEQ, D_MODEL)
    t = batch * seq
    depth = ffn1_norm.shape[0]
    xt = x.reshape(t, d)
    tables = _rope_tables()
    bf = lambda a: a.astype(BF16)
    for i in range(depth):
        last = i == depth - 1
        x1, hn = _ffn(xt, ffn1_norm[i][None], bf(ffn1_w_gate[i]), bf(ffn1_w_up[i]), bf(ffn1_w_down[i]),
                      mix_norm[i][None], post="hn")
        c_v, c_u, c_g = 2 * ATT_WIDTH, 3 * ATT_WIDTH, 3 * ATT_WIDTH + 2 * SG_WIDTH
        qk = _proj(hn, bf(w_in[i][:, :c_v]), epilogue="rope", tn=ATT_WIDTH, tables=tables)
        v = _proj(hn, bf(w_in[i][:, c_v:c_u]), epilogue="heads", tn=ATT_WIDTH)
        uv = _proj(hn, bf(w_in[i][:, c_u:c_g]), epilogue="gelu", tn=SG_WIDTH)
        gates = _proj(hn, bf(w_in[i][:, c_g:]), epilogue="sigmoid", tn=D_MODEL)
        o_list, st_list = [], []
        for g in range(N_GROUPS):
            o_g, st_g = _attention(qk, v, g, batch)
            o_list.append(o_g)
            st_list.append(st_g)
        merged = _mix(o_list, st_list, uv, gates, sg_ln_g[i][None], sg_ln_b[i][None], sg_w[i], sg_b[i].T,
                      bf(w_att_out[i]), bf(w_sg_out[i]))
        x2 = _out_proj(merged, bf(w_out[i]), x1)
        (xt,) = _ffn(x2, ffn2_norm[i][None], bf(ffn2_w_gate[i]), bf(ffn2_w_up[i]), bf(ffn2_w_down[i]),
                     final_norm[None], post="final" if last else "plain")
    return xt.reshape(batch, seq, d)
```

```python
import functools

import jax
import jax.numpy as jnp
from jax import lax
from jax.experimental import pallas as pl
from jax.experimental.pallas import tpu as pltpu

F32 = jnp.float32
BF16 = jnp.bfloat16

D_MODEL = 2048
SEQ = 4096
HEAD_DIM = 128
HEADS_PER_GROUP = 4
DILATIONS = (1, 4, 16)
N_GROUPS = len(DILATIONS)
N_ATT_HEADS = N_GROUPS * HEADS_PER_GROUP
ATT_WIDTH = N_ATT_HEADS * HEAD_DIM
ATT_BLOCK = 128
QKV_TILE = 1024
ROPE_DIM = HEAD_DIM // 4
ROPE_HALF = ROPE_DIM // 2
ROPE_THETA = 500000.0
SG_CHUNK = 128
SG_GROUPS = 12
SG_GROUP_DIM = 128
SG_WIDTH = SG_GROUPS * SG_GROUP_DIM
NORM_EPS = 1e-6
LN_EPS = 1e-5

V7X_VMEM_BYTES = 64 * 1024 * 1024
V7X_LANES = 128
BF16_ROWS = 16
VMEM_LIMIT = 52 * 1024 * 1024


def _params(semantics, vmem_limit=VMEM_LIMIT):
    return pltpu.CompilerParams(dimension_semantics=semantics, vmem_limit_bytes=vmem_limit)


def _rms(x, g):
    return x * lax.rsqrt(jnp.mean(x * x, axis=-1, keepdims=True) + NORM_EPS) * g


def _rope_table_kernel(freq_ref, cos_ref, sin_lo_ref, sin_hi_ref):
    rows = cos_ref.shape[0]
    pos = (pl.program_id(0) * rows + lax.broadcasted_iota(jnp.int32, (rows, V7X_LANES), 0)).astype(F32)
    lane = lax.broadcasted_iota(jnp.int32, (rows, V7X_LANES), 1)
    ang = pos * freq_ref[...]
    c = jnp.cos(ang)
    s = jnp.sin(ang)
    cos_ref[...] = jnp.where(lane < ROPE_DIM, c, 1.0)
    sin_lo_ref[...] = jnp.where(lane < ROPE_HALF, -s, 0.0)
    sin_hi_ref[...] = jnp.where((lane >= ROPE_HALF) & (lane < ROPE_DIM), s, 0.0)


def _rope_tables():
    inv_freq = ROPE_THETA ** (-jnp.arange(0, ROPE_DIM, 2, dtype=F32) / ROPE_DIM)
    freq_lane = jnp.concatenate([inv_freq, inv_freq, jnp.zeros((V7X_LANES - ROPE_DIM,), F32)])[None, :]
    rows = 512
    tab = jax.ShapeDtypeStruct((SEQ, V7X_LANES), F32)
    spec = pl.BlockSpec((rows, V7X_LANES), lambda i: (i, 0))
    return pl.pallas_call(
        _rope_table_kernel,
        grid=(SEQ // rows,),
        in_specs=[pl.BlockSpec((1, V7X_LANES), lambda i: (0, 0))],
        out_specs=[spec, spec, spec],
        out_shape=[tab, tab, tab],
        compiler_params=_params(("parallel",)),
        name="rope_table",
    )(freq_lane)


def _ffn_kernel(x_ref, g_ref, wg_ref, wu_ref, wd_ref, pg_ref, *refs, post):
    if post == "hn":
        o_ref, hn_ref, xn_ref = refs
    else:
        o_ref, xn_ref = refs
    j = pl.program_id(1)

    @pl.when(j == 0)
    def _():
        xn_ref[...] = _rms(x_ref[...], g_ref[...]).astype(BF16)
        o_ref[...] = jnp.zeros_like(o_ref)

    xn = xn_ref[...]
    h = jnp.dot(xn, wg_ref[...], preferred_element_type=F32)
    u = jnp.dot(xn, wu_ref[...], preferred_element_type=F32)
    a = (jax.nn.silu(h) * u).astype(BF16)
    o_ref[...] += jnp.dot(a, wd_ref[...], preferred_element_type=F32)

    @pl.when(j == pl.num_programs(1) - 1)
    def _():
        y = x_ref[...] + 0.5 * o_ref[...]
        if post == "hn":
            o_ref[...] = y
            hn_ref[...] = _rms(y, pg_ref[...]).astype(BF16)
        elif post == "final":
            o_ref[...] = _rms(y, pg_ref[...])
        else:
            o_ref[...] = y


def _ffn(x, g, wg, wu, wd, post_g, *, post, tm=512, tf=512):
    t, d = x.shape
    f = wg.shape[1]
    row = pl.BlockSpec((tm, d), lambda i, j: (i, 0))
    vec = pl.BlockSpec((1, d), lambda i, j: (0, 0))
    out_shape = [jax.ShapeDtypeStruct((t, d), F32)]
    out_specs = [row]
    if post == "hn":
        out_shape.append(jax.ShapeDtypeStruct((t, d), BF16))
        out_specs.append(row)
    return pl.pallas_call(
        functools.partial(_ffn_kernel, post=post),
        grid=(t // tm, f // tf),
        in_specs=[
            row,
            vec,
            pl.BlockSpec((d, tf), lambda i, j: (0, j)),
            pl.BlockSpec((d, tf), lambda i, j: (0, j)),
            pl.BlockSpec((tf, d), lambda i, j: (j, 0)),
            vec,
        ],
        out_specs=out_specs,
        out_shape=out_shape,
        scratch_shapes=[pltpu.VMEM((tm, d), BF16)],
        compiler_params=_params(("parallel", "arbitrary")),
        name="ffn_" + post,
    )(x, g, wg, wu, wd, post_g)


def _store_head(o_ref, stage_ref, h, val):
    dil = DILATIONS[h // HEADS_PER_GROUP]
    if dil == 1:
        o_ref[h] = val.astype(BF16)
        return
    slot = h - HEADS_PER_GROUP
    stage_ref[slot] = val
    rows = QKV_TILE // dil
    for r in range(dil):
        o_ref[h, r * rows:(r + 1) * rows, :] = stage_ref[slot, pl.ds(r, rows, stride=dil), :].astype(BF16)


def _proj_kernel(h_ref, w_ref, *refs, epilogue):
    acc = jnp.dot(h_ref[...], w_ref[...], preferred_element_type=F32)
    n_heads = acc.shape[1] // HEAD_DIM
    if epilogue == "rope":
        cos_ref, sin_lo_ref, sin_hi_ref, o_ref, stage_ref = refs
        scale = jnp.where(pl.program_id(1) == 0, HEAD_DIM ** -0.5, 1.0).astype(F32)
        cos = cos_ref[...] * scale
        sin_lo = sin_lo_ref[...] * scale
        sin_hi = sin_hi_ref[...] * scale
        for h in range(n_heads):
            t = acc[:, h * HEAD_DIM:(h + 1) * HEAD_DIM]
            rot = (t * cos
                   + pltpu.roll(t, HEAD_DIM - ROPE_HALF, 1) * sin_lo
                   + pltpu.roll(t, ROPE_HALF, 1) * sin_hi)
            _store_head(o_ref, stage_ref, h, rot)
    elif epilogue == "heads":
        o_ref, stage_ref = refs
        for h in range(n_heads):
            _store_head(o_ref, stage_ref, h, acc[:, h * HEAD_DIM:(h + 1) * HEAD_DIM])
    else:
        (o_ref,) = refs
        if epilogue == "gelu":
            acc = 0.5 * acc * (1.0 + lax.erf(acc * (0.5 ** 0.5)))
        elif epilogue == "sigmoid":
            acc = jax.nn.sigmoid(acc)
        o_ref[...] = acc.astype(o_ref.dtype)


def _proj(hn, w, *, epilogue, tn, tables=(), tm=QKV_TILE):
    t, d = hn.shape
    ncols = w.shape[1]
    scratch = []
    in_specs = [
        pl.BlockSpec((tm, d), lambda i, j: (i, 0)),
        pl.BlockSpec((d, tn), lambda i, j: (0, j)),
    ]
    seq_blocks = SEQ // tm
    for _ in tables:
        in_specs.append(pl.BlockSpec((tm, V7X_LANES), lambda i, j: (i % seq_blocks, 0)))
    if epilogue in ("rope", "heads"):
        assert tm == QKV_TILE and tn == ATT_WIDTH
        out_spec = pl.BlockSpec((N_ATT_HEADS, tm, HEAD_DIM), lambda i, j: (j, i, 0))
        out_shape = jax.ShapeDtypeStruct((ncols // HEAD_DIM, t, HEAD_DIM), BF16)
        scratch = [pltpu.VMEM((N_ATT_HEADS - HEADS_PER_GROUP, tm, HEAD_DIM), F32)]
    else:
        out_spec = pl.BlockSpec((tm, tn), lambda i, j: (i, j))
        out_shape = jax.ShapeDtypeStruct((t, ncols), BF16)
    return pl.pallas_call(
        functools.partial(_proj_kernel, epilogue=epilogue),
        grid=(t // tm, ncols // tn),
        in_specs=in_specs,
        out_specs=out_spec,
        out_shape=out_shape,
        scratch_shapes=scratch,
        compiler_params=_params(("parallel", "arbitrary")),
        name="proj_" + epilogue,
    )(hn, w, *tables)


def _attn_kernel(q_ref, k_ref, v_ref, o_ref, st_ref, *, dil):
    head = pl.program_id(1)
    n_qb = SEQ // (ATT_BLOCK * dil)

    qi = lax.broadcasted_iota(jnp.int32, (ATT_BLOCK, 2 * ATT_BLOCK), 0)
    kj = lax.broadcasted_iota(jnp.int32, (ATT_BLOCK, 2 * ATT_BLOCK), 1)
    diff = qi + ATT_BLOCK - kj
    band = (diff >= 0) & (diff <= ATT_BLOCK)
    band_first = band & (kj >= ATT_BLOCK)
    lane = lax.broadcasted_iota(jnp.int32, (ATT_BLOCK, V7X_LANES), 1)
    ones = jnp.ones((2 * ATT_BLOCK, HEAD_DIM), BF16)
    zeros = jnp.zeros((ATT_BLOCK, HEAD_DIM), BF16)

    @pl.when(head == 0)
    def _():
        st_ref[...] = jnp.zeros_like(st_ref)

    rows_per = QKV_TILE // dil
    run = min(ATT_BLOCK, rows_per)

    def load(ref, qb, r):
        parts = []
        for idx in range(qb * ATT_BLOCK, (qb + 1) * ATT_BLOCK, run):
            tile, local = divmod(idx, rows_per)
            parts.append(ref[pl.ds(tile * QKV_TILE + r * rows_per + local, run), :])
        return parts[0] if len(parts) == 1 else jnp.concatenate(parts, axis=0)

    for r in range(dil):
        k_prev, v_prev = zeros, zeros
        for qb in range(n_qb):
            q, k_cur, v_cur = load(q_ref, qb, r), load(k_ref, qb, r), load(v_ref, qb, r)
            sel = (pl.ds(qb * ATT_BLOCK * dil + r, ATT_BLOCK, stride=dil) if dil > 1
                   else pl.ds(qb * ATT_BLOCK, ATT_BLOCK))
            k_win = jnp.concatenate([k_prev, k_cur], axis=0)
            v_win = jnp.concatenate([v_prev, v_cur], axis=0)
            s = lax.dot_general(q, k_win, (((1,), (1,)), ((), ())), preferred_element_type=F32)
            s = jnp.where(band_first if qb == 0 else band, s, -jnp.inf)
            m = jnp.max(s, axis=-1, keepdims=True)
            p = jnp.exp(s - m).astype(BF16)
            pv = jnp.dot(p, jnp.concatenate([v_win, ones], axis=1), preferred_element_type=F32)
            l_b = pv[:, HEAD_DIM:]
            o_ref[sel, :] = pv[:, :HEAD_DIM] / l_b
            st = st_ref[sel, :]
            st = jnp.where(lane == head, m, st)
            st_ref[sel, :] = jnp.where(lane == HEADS_PER_GROUP + head, l_b, st)
            k_prev, v_prev = k_cur, v_cur


def _attention(qk, v, group, batch):
    dil = DILATIONS[group]
    t = qk.shape[1]
    h0 = group * HEADS_PER_GROUP

    def seq(slab0):
        return pl.BlockSpec((None, SEQ, HEAD_DIM), lambda b, h: (slab0 + h, b, 0))

    return pl.pallas_call(
        functools.partial(_attn_kernel, dil=dil),
        grid=(batch, HEADS_PER_GROUP),
        in_specs=[seq(h0), seq(N_ATT_HEADS + h0), seq(h0)],
        out_specs=[
            pl.BlockSpec((None, SEQ, HEAD_DIM), lambda b, h: (h, b, 0)),
            pl.BlockSpec((SEQ, V7X_LANES), lambda b, h: (b, 0)),
        ],
        out_shape=[
            jax.ShapeDtypeStruct((HEADS_PER_GROUP, t, HEAD_DIM), F32),
            jax.ShapeDtypeStruct((t, V7X_LANES), F32),
        ],
        compiler_params=_params(("parallel", "arbitrary")),
        name=f"attn_d{dil}",
    )(qk, qk, v)


def _mix_prepare(o_refs, stats, u_ref, vs_ref, lng_ref, lnb_ref, sgw_ref, sgb_ref, oatt_ref, gated_ref):
    tm = oatt_ref.shape[0]
    for h in range(HEADS_PER_GROUP):
        ms = [s[:, h:h + 1] for s in stats]
        ls = [s[:, HEADS_PER_GROUP + h:HEADS_PER_GROUP + h + 1] for s in stats]
        m_all = jnp.maximum(jnp.maximum(ms[0], ms[1]), ms[2])
        ws = [l * jnp.exp(m - m_all) for l, m in zip(ls, ms)]
        num = ws[0] * o_refs[0][h] + ws[1] * o_refs[1][h] + ws[2] * o_refs[2][h]
        oatt_ref[:, h * HEAD_DIM:(h + 1) * HEAD_DIM] = (num / (ws[0] + ws[1] + ws[2])).astype(BF16)

    vs = vs_ref[...].astype(F32)
    mu = jnp.mean(vs, axis=-1, keepdims=True)
    var = jnp.mean(jnp.square(vs - mu), axis=-1, keepdims=True)
    vn = ((vs - mu) * lax.rsqrt(var + LN_EPS) * lng_ref[...] + lnb_ref[...]).astype(BF16)
    ti = lax.broadcasted_iota(jnp.int32, (SG_CHUNK, SG_CHUNK), 0)
    si = lax.broadcasted_iota(jnp.int32, (SG_CHUNK, SG_CHUNK), 1)
    causal = si <= ti
    for g in range(SG_GROUPS):
        cols = slice(g * SG_GROUP_DIM, (g + 1) * SG_GROUP_DIM)
        w_sp = jnp.where(causal, sgw_ref[g], 0.0).astype(BF16)
        bias = sgb_ref[:, g:g + 1]
        for c in range(tm // SG_CHUNK):
            rows = slice(c * SG_CHUNK, (c + 1) * SG_CHUNK)
            spatial = jnp.dot(w_sp, vn[rows, cols], preferred_element_type=F32) + bias
            gated_ref[rows, cols] = (u_ref[rows, cols].astype(F32) * spatial).astype(BF16)


def _mix_kernel(o0_ref, o1_ref, o2_ref, s0_ref, s1_ref, s2_ref, u_ref, vs_ref, ga_ref, gs_ref,
                lng_ref, lnb_ref, sgw_ref, sgb_ref, wa_ref, ws_ref, out_ref,
                oatt_a, gated_a, oatt_b, gated_b):
    i = pl.program_id(0)

    @pl.when(i == 0)
    def _():
        oatt_b[...] = jnp.zeros_like(oatt_b)
        gated_b[...] = jnp.zeros_like(gated_b)

    def step(oatt_w, gated_w, oatt_r, gated_r):
        y_att = jnp.dot(oatt_r[...], wa_ref[...], preferred_element_type=F32)
        y_sg = jnp.dot(gated_r[...], ws_ref[...], preferred_element_type=F32)
        merged = ga_ref[...].astype(F32) * y_att + gs_ref[...].astype(F32) * y_sg
        out_ref[...] = merged.astype(out_ref.dtype)
        _mix_prepare((o0_ref, o1_ref, o2_ref), (s0_ref[...], s1_ref[...], s2_ref[...]), u_ref, vs_ref,
                     lng_ref, lnb_ref, sgw_ref, sgb_ref, oatt_w, gated_w)

    @pl.when(i % 2 == 0)
    def _():
        step(oatt_a, gated_a, oatt_b, gated_b)

    @pl.when(i % 2 == 1)
    def _():
        step(oatt_b, gated_b, oatt_a, gated_a)


def _mix(o_list, st_list, uv, gates, ln_g, ln_b, sg_w, sg_b_t, w_att, w_sg, *, tm=512):
    t = uv.shape[0]
    n = t // tm
    cur = lambda i: jnp.minimum(i, n - 1)
    lag = lambda i: jnp.maximum(i - 1, 0)
    prep_row = lambda width, blk: pl.BlockSpec((tm, width), lambda i: (cur(i), blk))
    lag_row = lambda width, blk: pl.BlockSpec((tm, width), lambda i: (lag(i), blk))
    const2 = lambda shape: pl.BlockSpec(shape, lambda i: (0, 0), pipeline_mode=pl.Buffered(1))
    in_specs = (
        [pl.BlockSpec((HEADS_PER_GROUP, tm, HEAD_DIM), lambda i: (0, cur(i), 0))] * 3
        + [prep_row(V7X_LANES, 0)] * 3
        + [prep_row(SG_WIDTH, 0), prep_row(SG_WIDTH, 1), lag_row(D_MODEL, 0), lag_row(D_MODEL, 1)]
        + [const2((1, SG_WIDTH)), const2((1, SG_WIDTH)),
           pl.BlockSpec(sg_w.shape, lambda i: (0, 0, 0), pipeline_mode=pl.Buffered(1)),
           const2(sg_b_t.shape), const2(w_att.shape), const2(w_sg.shape)]
    )
    att_w = HEADS_PER_GROUP * HEAD_DIM
    return pl.pallas_call(
        _mix_kernel,
        grid=(n + 1,),
        in_specs=in_specs,
        out_specs=pl.BlockSpec((tm, D_MODEL), lambda i: (lag(i), 0)),
        out_shape=jax.ShapeDtypeStruct((t, D_MODEL), BF16),
        scratch_shapes=[pltpu.VMEM((tm, att_w), BF16), pltpu.VMEM((tm, SG_WIDTH), BF16),
                        pltpu.VMEM((tm, att_w), BF16), pltpu.VMEM((tm, SG_WIDTH), BF16)],
        compiler_params=_params(("arbitrary",)),
        name="mix",
    )(*o_list, *st_list, uv, uv, gates, gates, ln_g, ln_b, sg_w, sg_b_t, w_att, w_sg)


def _out_proj_kernel(m_ref, w_ref, x_ref, o_ref):
    o_ref[...] = x_ref[...] + jnp.dot(m_ref[...], w_ref[...], preferred_element_type=F32)


def _out_proj(merged, w, x, *, tm=1024, tn=1024):
    t, d = merged.shape
    n = w.shape[1]
    return pl.pallas_call(
        _out_proj_kernel,
        grid=(t // tm, n // tn),
        in_specs=[
            pl.BlockSpec((tm, d), lambda i, j: (i, 0)),
            pl.BlockSpec((d, tn), lambda i, j: (0, j)),
            pl.BlockSpec((tm, tn), lambda i, j: (i, j)),
        ],
        out_specs=pl.BlockSpec((tm, tn), lambda i, j: (i, j)),
        out_shape=jax.ShapeDtypeStruct((t, n), F32),
        compiler_params=_params(("parallel", "arbitrary")),
        name="out_proj",
    )(merged, w, x)


def kernel(x, ffn1_norm, ffn1_w_gate, ffn1_w_up, ffn1_w_down, mix_norm, w_in, sg_ln_g, sg_ln_b, sg_w, sg_b, w_att_out, w_sg_out, w_out, ffn2_norm, ffn2_w_gate, ffn2_w_up, ffn2_w_down, final_norm):
    batch, seq, d = x.shape
    assert (seq, d) == (SEQ, D_MODEL)
    t = batch * seq
    depth = ffn1_norm.shape[0]
    xt = x.reshape(t, d)
    tables = _rope_tables()
    bf = lambda a: a.astype(BF16)
    for i in range(depth):
        last = i == depth - 1
        x1, hn = _ffn(xt, ffn1_norm[i][None], bf(ffn1_w_gate[i]), bf(ffn1_w_up[i]), bf(ffn1_w_down[i]),
                      mix_norm[i][None], post="hn")
        c_v, c_u, c_g = 2 * ATT_WIDTH, 3 * ATT_WIDTH, 3 * ATT_WIDTH + 2 * SG_WIDTH
        qk = _proj(hn, bf(w_in[i][:, :c_v]), epilogue="rope", tn=ATT_WIDTH, tables=tables)
        v = _proj(hn, bf(w_in[i][:, c_v:c_u]), epilogue="heads", tn=ATT_WIDTH)
        uv = _proj(hn, bf(w_in[i][:, c_u:c_g]), epilogue="gelu", tn=SG_WIDTH)
        gates = _proj(hn, bf(w_in[i][:, c_g:]), epilogue="sigmoid", tn=D_MODEL)
        o_list, st_list = [], []
        for g in range(N_GROUPS):
            o_g, st_g = _attention(qk, v, g, batch)
            o_list.append(o_g)
            st_list.append(st_g)
        merged = _mix(o_list, st_list, uv, gates, sg_ln_g[i][None], sg_ln_b[i][None], sg_w[i], sg_b[i].T,
                      bf(w_att_out[i]), bf(w_sg_out[i]))
        x2 = _out_proj(merged, bf(w_out[i]), x1)
        (xt,) = _ffn(x2, ffn2_norm[i][None], bf(ffn2_w_gate[i]), bf(ffn2_w_up[i]), bf(ffn2_w_down[i]),
                     final_norm[None], post="final" if last else "plain")
    return xt.reshape(batch, seq, d)
```

```python
import functools

import jax
import jax.numpy as jnp
from jax import lax
from jax.experimental import pallas as pl
from jax.experimental.pallas import tpu as pltpu

F32 = jnp.float32
BF16 = jnp.bfloat16

D_MODEL = 2048
SEQ = 4096
HEAD_DIM = 128
HEADS_PER_GROUP = 4
DILATIONS = (1, 4, 16)
N_GROUPS = len(DILATIONS)
N_ATT_HEADS = N_GROUPS * HEADS_PER_GROUP
ATT_WIDTH = N_ATT_HEADS * HEAD_DIM
ATT_BLOCK = 128
QKV_TILE = 1024
ROPE_DIM = HEAD_DIM // 4
ROPE_HALF = ROPE_DIM // 2
ROPE_THETA = 500000.0
SG_CHUNK = 128
SG_GROUPS = 12
SG_GROUP_DIM = 128
SG_WIDTH = SG_GROUPS * SG_GROUP_DIM
NORM_EPS = 1e-6
LN_EPS = 1e-5

V7X_VMEM_BYTES = 64 * 1024 * 1024
V7X_LANES = 128
BF16_ROWS = 16
CAST_STEPS = 128
VMEM_LIMIT = 52 * 1024 * 1024


def _params(semantics, vmem_limit=VMEM_LIMIT):
    return pltpu.CompilerParams(dimension_semantics=semantics, vmem_limit_bytes=vmem_limit)


def _rms(x, g):
    return x * lax.rsqrt(jnp.mean(x * x, axis=-1, keepdims=True) + NORM_EPS) * g


def _rope_table_kernel(freq_ref, cos_ref, sin_lo_ref, sin_hi_ref):
    rows = cos_ref.shape[0]
    pos = (pl.program_id(0) * rows + lax.broadcasted_iota(jnp.int32, (rows, V7X_LANES), 0)).astype(F32)
    lane = lax.broadcasted_iota(jnp.int32, (rows, V7X_LANES), 1)
    ang = pos * freq_ref[...]
    c = jnp.cos(ang)
    s = jnp.sin(ang)
    cos_ref[...] = jnp.where(lane < ROPE_DIM, c, 1.0)
    sin_lo_ref[...] = jnp.where(lane < ROPE_HALF, -s, 0.0)
    sin_hi_ref[...] = jnp.where((lane >= ROPE_HALF) & (lane < ROPE_DIM), s, 0.0)


def _rope_tables():
    inv_freq = ROPE_THETA ** (-jnp.arange(0, ROPE_DIM, 2, dtype=F32) / ROPE_DIM)
    freq_lane = jnp.concatenate([inv_freq, inv_freq, jnp.zeros((V7X_LANES - ROPE_DIM,), F32)])[None, :]
    rows = 512
    tab = jax.ShapeDtypeStruct((SEQ, V7X_LANES), F32)
    spec = pl.BlockSpec((rows, V7X_LANES), lambda i: (i, 0))
    return pl.pallas_call(
        _rope_table_kernel,
        grid=(SEQ // rows,),
        in_specs=[pl.BlockSpec((1, V7X_LANES), lambda i: (0, 0))],
        out_specs=[spec, spec, spec],
        out_shape=[tab, tab, tab],
        compiler_params=_params(("parallel",)),
        name="rope_table",
    )(freq_lane)


def _ffn_kernel(x_ref, g_ref, wg_ref, wu_ref, wd_ref, pg_ref, *refs, post, cast_splits):
    n_cast = len(cast_splits)
    cast_in, refs = refs[:n_cast], refs[n_cast:]
    if post == "hn":
        o_ref, hn_ref, *cast_out, xn_ref = refs
    else:
        o_ref, *cast_out, xn_ref = refs
    j = pl.program_id(1)

    @pl.when(j == 0)
    def _():
        xn_ref[...] = _rms(x_ref[...], g_ref[...]).astype(BF16)
        o_ref[...] = jnp.zeros_like(o_ref)

    cast_out = iter(cast_out)
    for src, splits in zip(cast_in, cast_splits):
        for c0, c1 in zip(splits[:-1], splits[1:]):
            next(cast_out)[...] = src[:, c0:c1].astype(BF16)

    xn = xn_ref[...]
    h = jnp.dot(xn, wg_ref[...], preferred_element_type=F32)
    u = jnp.dot(xn, wu_ref[...], preferred_element_type=F32)
    a = (jax.nn.silu(h) * u).astype(BF16)
    o_ref[...] += jnp.dot(a, wd_ref[...], preferred_element_type=F32)

    @pl.when(j == pl.num_programs(1) - 1)
    def _():
        y = x_ref[...] + 0.5 * o_ref[...]
        if post == "hn":
            o_ref[...] = y
            hn_ref[...] = _rms(y, pg_ref[...]).astype(BF16)
        elif post == "final":
            o_ref[...] = _rms(y, pg_ref[...])
        else:
            o_ref[...] = y


def _ffn(x, g, wg, wu, wd, post_g, *, post, casts=(), tm=512, tf=512):
    t, d = x.shape
    f = wg.shape[1]
    n_j = f // tf
    n_steps = (t // tm) * n_j
    row = pl.BlockSpec((tm, d), lambda i, j: (i, 0))
    vec = pl.BlockSpec((1, d), lambda i, j: (0, 0))
    out_shape = [jax.ShapeDtypeStruct((t, d), F32)]
    out_specs = [row]
    if post == "hn":
        out_shape.append(jax.ShapeDtypeStruct((t, d), BF16))
        out_specs.append(row)
    cast_specs = []
    for w, splits in casts:
        rows, cols = w.shape
        rb = next(r for r in range(BF16_ROWS, rows + 1, BF16_ROWS) if rows % r == 0 and rows // r <= CAST_STEPS)
        n_blocks = rows // rb
        assert n_blocks <= n_steps and splits[0] == 0 and splits[-1] == cols
        block_of = lambda i, j, n_blocks=n_blocks: (jnp.minimum(i * n_j + j, n_blocks - 1), 0)
        cast_specs.append(pl.BlockSpec((rb, cols), block_of))
        for c0, c1 in zip(splits[:-1], splits[1:]):
            out_shape.append(jax.ShapeDtypeStruct((rows, c1 - c0), BF16))
            out_specs.append(pl.BlockSpec((rb, c1 - c0), block_of))
    return pl.pallas_call(
        functools.partial(_ffn_kernel, post=post, cast_splits=tuple(s for _, s in casts)),
        grid=(t // tm, n_j),
        in_specs=[
            row,
            vec,
            pl.BlockSpec((d, tf), lambda i, j: (0, j)),
            pl.BlockSpec((d, tf), lambda i, j: (0, j)),
            pl.BlockSpec((tf, d), lambda i, j: (j, 0)),
            vec,
        ] + cast_specs,
        out_specs=out_specs,
        out_shape=out_shape,
        scratch_shapes=[pltpu.VMEM((tm, d), BF16)],
        compiler_params=_params(("arbitrary", "arbitrary")),
        name="ffn_" + post,
    )(x, g, wg, wu, wd, post_g, *[w for w, _ in casts])


def _store_head(o_ref, stage_ref, h, val):
    dil = DILATIONS[h // HEADS_PER_GROUP]
    if dil == 1:
        o_ref[h] = val.astype(BF16)
        return
    slot = h - HEADS_PER_GROUP
    stage_ref[slot] = val
    rows = QKV_TILE // dil
    for r in range(dil):
        o_ref[h, r * rows:(r + 1) * rows, :] = stage_ref[slot, pl.ds(r, rows, stride=dil), :].astype(BF16)


def _proj_kernel(h_ref, w_ref, *refs, epilogue):
    acc = jnp.dot(h_ref[...], w_ref[...], preferred_element_type=F32)
    n_heads = acc.shape[1] // HEAD_DIM
    if epilogue == "rope":
        cos_ref, sin_lo_ref, sin_hi_ref, o_ref, stage_ref = refs
        scale = jnp.where(pl.program_id(1) == 0, HEAD_DIM ** -0.5, 1.0).astype(F32)
        cos = cos_ref[...] * scale
        sin_lo = sin_lo_ref[...] * scale
        sin_hi = sin_hi_ref[...] * scale
        for h in range(n_heads):
            t = acc[:, h * HEAD_DIM:(h + 1) * HEAD_DIM]
            rot = (t * cos
                   + pltpu.roll(t, HEAD_DIM - ROPE_HALF, 1) * sin_lo
                   + pltpu.roll(t, ROPE_HALF, 1) * sin_hi)
            _store_head(o_ref, stage_ref, h, rot)
    elif epilogue == "heads":
        o_ref, stage_ref = refs
        for h in range(n_heads):
            _store_head(o_ref, stage_ref, h, acc[:, h * HEAD_DIM:(h + 1) * HEAD_DIM])
    else:
        (o_ref,) = refs
        if epilogue == "gelu":
            acc = 0.5 * acc * (1.0 + lax.erf(acc * (0.5 ** 0.5)))
        elif epilogue == "sigmoid":
            acc = jax.nn.sigmoid(acc)
        o_ref[...] = acc.astype(o_ref.dtype)


def _proj(hn, w, col0, ncols, *, epilogue, tn, tables=(), tm=QKV_TILE):
    t, d = hn.shape
    assert col0 % tn == 0 and ncols % tn == 0
    c0 = col0 // tn
    scratch = []
    in_specs = [
        pl.BlockSpec((tm, d), lambda i, j: (i, 0)),
        pl.BlockSpec((d, tn), lambda i, j: (0, c0 + j)),
    ]
    seq_blocks = SEQ // tm
    for _ in tables:
        in_specs.append(pl.BlockSpec((tm, V7X_LANES), lambda i, j: (i % seq_blocks, 0)))
    if epilogue in ("rope", "heads"):
        assert tm == QKV_TILE and tn == ATT_WIDTH
        out_spec = pl.BlockSpec((N_ATT_HEADS, tm, HEAD_DIM), lambda i, j: (j, i, 0))
        out_shape = jax.ShapeDtypeStruct((ncols // HEAD_DIM, t, HEAD_DIM), BF16)
        scratch = [pltpu.VMEM((N_ATT_HEADS - HEADS_PER_GROUP, tm, HEAD_DIM), F32)]
    else:
        out_spec = pl.BlockSpec((tm, tn), lambda i, j: (i, j))
        out_shape = jax.ShapeDtypeStruct((t, ncols), BF16)
    return pl.pallas_call(
        functools.partial(_proj_kernel, epilogue=epilogue),
        grid=(t // tm, ncols // tn),
        in_specs=in_specs,
        out_specs=out_spec,
        out_shape=out_shape,
        scratch_shapes=scratch,
        compiler_params=_params(("parallel", "arbitrary")),
        name="proj_" + epilogue,
    )(hn, w, *tables)


def _attn_kernel(q_ref, k_ref, v_ref, o_ref, st_ref, *, dil):
    head = pl.program_id(1)
    n_qb = SEQ // (ATT_BLOCK * dil)

    qi = lax.broadcasted_iota(jnp.int32, (ATT_BLOCK, 2 * ATT_BLOCK), 0)
    kj = lax.broadcasted_iota(jnp.int32, (ATT_BLOCK, 2 * ATT_BLOCK), 1)
    diff = qi + ATT_BLOCK - kj
    band = (diff >= 0) & (diff <= ATT_BLOCK)
    band_first = band & (kj >= ATT_BLOCK)
    lane = lax.broadcasted_iota(jnp.int32, (ATT_BLOCK, V7X_LANES), 1)
    ones = jnp.ones((2 * ATT_BLOCK, HEAD_DIM), BF16)
    zeros = jnp.zeros((ATT_BLOCK, HEAD_DIM), BF16)

    @pl.when(head == 0)
    def _():
        st_ref[...] = jnp.zeros_like(st_ref)

    rows_per = QKV_TILE // dil
    run = min(ATT_BLOCK, rows_per)

    def load(ref, qb, r):
        parts = []
        for idx in range(qb * ATT_BLOCK, (qb + 1) * ATT_BLOCK, run):
            tile, local = divmod(idx, rows_per)
            parts.append(ref[pl.ds(tile * QKV_TILE + r * rows_per + local, run), :])
        return parts[0] if len(parts) == 1 else jnp.concatenate(parts, axis=0)

    for r in range(dil):
        k_prev, v_prev = zeros, zeros
        for qb in range(n_qb):
            q, k_cur, v_cur = load(q_ref, qb, r), load(k_ref, qb, r), load(v_ref, qb, r)
            sel = (pl.ds(qb * ATT_BLOCK * dil + r, ATT_BLOCK, stride=dil) if dil > 1
                   else pl.ds(qb * ATT_BLOCK, ATT_BLOCK))
            k_win = jnp.concatenate([k_prev, k_cur], axis=0)
            v_win = jnp.concatenate([v_prev, v_cur], axis=0)
            s = lax.dot_general(q, k_win, (((1,), (1,)), ((), ())), preferred_element_type=F32)
            s = jnp.where(band_first if qb == 0 else band, s, -jnp.inf)
            m = jnp.max(s, axis=-1, keepdims=True)
            p = jnp.exp(s - m).astype(BF16)
            pv = jnp.dot(p, jnp.concatenate([v_win, ones], axis=1), preferred_element_type=F32)
            l_b = pv[:, HEAD_DIM:]
            o_ref[sel, :] = pv[:, :HEAD_DIM] / l_b
            st = st_ref[sel, :]
            st = jnp.where(lane == head, m, st)
            st_ref[sel, :] = jnp.where(lane == HEADS_PER_GROUP + head, l_b, st)
            k_prev, v_prev = k_cur, v_cur


def _attention(qk, v, group, batch):
    dil = DILATIONS[group]
    t = qk.shape[1]
    h0 = group * HEADS_PER_GROUP

    def seq(slab0):
        return pl.BlockSpec((None, SEQ, HEAD_DIM), lambda b, h: (slab0 + h, b, 0))

    return pl.pallas_call(
        functools.partial(_attn_kernel, dil=dil),
        grid=(batch, HEADS_PER_GROUP),
        in_specs=[seq(h0), seq(N_ATT_HEADS + h0), seq(h0)],
        out_specs=[
            pl.BlockSpec((None, SEQ, HEAD_DIM), lambda b, h: (h, b, 0)),
            pl.BlockSpec((SEQ, V7X_LANES), lambda b, h: (b, 0)),
        ],
        out_shape=[
            jax.ShapeDtypeStruct((HEADS_PER_GROUP, t, HEAD_DIM), F32),
            jax.ShapeDtypeStruct((t, V7X_LANES), F32),
        ],
        compiler_params=_params(("parallel", "arbitrary")),
        name=f"attn_d{dil}",
    )(qk, qk, v)


def _mix_prepare(o_refs, stats, u_ref, vs_ref, lng_ref, lnb_ref, sgw_ref, sgb_ref, oatt_ref, gated_ref):
    tm = oatt_ref.shape[0]
    for h in range(HEADS_PER_GROUP):
        ms = [s[:, h:h + 1] for s in stats]
        ls = [s[:, HEADS_PER_GROUP + h:HEADS_PER_GROUP + h + 1] for s in stats]
        m_all = jnp.maximum(jnp.maximum(ms[0], ms[1]), ms[2])
        ws = [l * jnp.exp(m - m_all) for l, m in zip(ls, ms)]
        num = ws[0] * o_refs[0][h] + ws[1] * o_refs[1][h] + ws[2] * o_refs[2][h]
        oatt_ref[:, h * HEAD_DIM:(h + 1) * HEAD_DIM] = (num / (ws[0] + ws[1] + ws[2])).astype(BF16)

    vs = vs_ref[...].astype(F32)
    mu = jnp.mean(vs, axis=-1, keepdims=True)
    var = jnp.mean(jnp.square(vs - mu), axis=-1, keepdims=True)
    vn = ((vs - mu) * lax.rsqrt(var + LN_EPS) * lng_ref[...] + lnb_ref[...]).astype(BF16)
    ti = lax.broadcasted_iota(jnp.int32, (SG_CHUNK, SG_CHUNK), 0)
    si = lax.broadcasted_iota(jnp.int32, (SG_CHUNK, SG_CHUNK), 1)
    causal = si <= ti
    for g in range(SG_GROUPS):
        cols = slice(g * SG_GROUP_DIM, (g + 1) * SG_GROUP_DIM)
        w_sp = jnp.where(causal, sgw_ref[g], 0.0).astype(BF16)
        bias = sgb_ref[:, g:g + 1]
        for c in range(tm // SG_CHUNK):
            rows = slice(c * SG_CHUNK, (c + 1) * SG_CHUNK)
            spatial = jnp.dot(w_sp, vn[rows, cols], preferred_element_type=F32) + bias
            gated_ref[rows, cols] = (u_ref[rows, cols].astype(F32) * spatial).astype(BF16)


def _mix_kernel(o0_ref, o1_ref, o2_ref, s0_ref, s1_ref, s2_ref, u_ref, vs_ref, ga_ref, gs_ref,
                lng_ref, lnb_ref, sgw_ref, sgb_ref, wa_ref, ws_ref, out_ref,
                oatt_a, gated_a, oatt_b, gated_b):
    i = pl.program_id(0)

    @pl.when(i == 0)
    def _():
        oatt_b[...] = jnp.zeros_like(oatt_b)
        gated_b[...] = jnp.zeros_like(gated_b)

    def step(oatt_w, gated_w, oatt_r, gated_r):
        y_att = jnp.dot(oatt_r[...], wa_ref[...], preferred_element_type=F32)
        y_sg = jnp.dot(gated_r[...], ws_ref[...], preferred_element_type=F32)
        merged = ga_ref[...].astype(F32) * y_att + gs_ref[...].astype(F32) * y_sg
        out_ref[...] = merged.astype(out_ref.dtype)
        _mix_prepare((o0_ref, o1_ref, o2_ref), (s0_ref[...], s1_ref[...], s2_ref[...]), u_ref, vs_ref,
                     lng_ref, lnb_ref, sgw_ref, sgb_ref, oatt_w, gated_w)

    @pl.when(i % 2 == 0)
    def _():
        step(oatt_a, gated_a, oatt_b, gated_b)

    @pl.when(i % 2 == 1)
    def _():
        step(oatt_b, gated_b, oatt_a, gated_a)


def _mix(o_list, st_list, uv, gates, ln_g, ln_b, sg_w, sg_b_t, w_att, w_sg, *, tm=512):
    t = uv.shape[0]
    n = t // tm
    cur = lambda i: jnp.minimum(i, n - 1)
    lag = lambda i: jnp.maximum(i - 1, 0)
    prep_row = lambda width, blk: pl.BlockSpec((tm, width), lambda i: (cur(i), blk))
    lag_row = lambda width, blk: pl.BlockSpec((tm, width), lambda i: (lag(i), blk))
    const2 = lambda shape: pl.BlockSpec(shape, lambda i: (0, 0), pipeline_mode=pl.Buffered(1))
    in_specs = (
        [pl.BlockSpec((HEADS_PER_GROUP, tm, HEAD_DIM), lambda i: (0, cur(i), 0))] * 3
        + [prep_row(V7X_LANES, 0)] * 3
        + [prep_row(SG_WIDTH, 0), prep_row(SG_WIDTH, 1), lag_row(D_MODEL, 0), lag_row(D_MODEL, 1)]
        + [const2((1, SG_WIDTH)), const2((1, SG_WIDTH)),
           pl.BlockSpec(sg_w.shape, lambda i: (0, 0, 0), pipeline_mode=pl.Buffered(1)),
           const2(sg_b_t.shape), const2(w_att.shape), const2(w_sg.shape)]
    )
    att_w = HEADS_PER_GROUP * HEAD_DIM
    return pl.pallas_call(
        _mix_kernel,
        grid=(n + 1,),
        in_specs=in_specs,
        out_specs=pl.BlockSpec((tm, D_MODEL), lambda i: (lag(i), 0)),
        out_shape=jax.ShapeDtypeStruct((t, D_MODEL), BF16),
        scratch_shapes=[pltpu.VMEM((tm, att_w), BF16), pltpu.VMEM((tm, SG_WIDTH), BF16),
                        pltpu.VMEM((tm, att_w), BF16), pltpu.VMEM((tm, SG_WIDTH), BF16)],
        compiler_params=_params(("arbitrary",)),
        name="mix",
    )(*o_list, *st_list, uv, uv, gates, gates, ln_g, ln_b, sg_w, sg_b_t, w_att, w_sg)


def _out_proj_kernel(m_ref, w_ref, x_ref, o_ref):
    o_ref[...] = x_ref[...] + jnp.dot(m_ref[...], w_ref[...], preferred_element_type=F32)


def _out_proj(merged, w, x, *, tm=1024, tn=1024):
    t, d = merged.shape
    n = w.shape[1]
    return pl.pallas_call(
        _out_proj_kernel,
        grid=(t // tm, n // tn),
        in_specs=[
            pl.BlockSpec((tm, d), lambda i, j: (i, 0)),
            pl.BlockSpec((d, tn), lambda i, j: (0, j)),
            pl.BlockSpec((tm, tn), lambda i, j: (i, j)),
        ],
        out_specs=pl.BlockSpec((tm, tn), lambda i, j: (i, j)),
        out_shape=jax.ShapeDtypeStruct((t, n), F32),
        compiler_params=_params(("parallel", "arbitrary")),
        name="out_proj",
    )(merged, w, x)


def kernel(x, ffn1_norm, ffn1_w_gate, ffn1_w_up, ffn1_w_down, mix_norm, w_in, sg_ln_g, sg_ln_b, sg_w, sg_b, w_att_out, w_sg_out, w_out, ffn2_norm, ffn2_w_gate, ffn2_w_up, ffn2_w_down, final_norm):
    batch, seq, d = x.shape
    assert (seq, d) == (SEQ, D_MODEL)
    t = batch * seq
    depth = ffn1_norm.shape[0]
    xt = x.reshape(t, d)
    tables = _rope_tables()
    bf = lambda a: a.astype(BF16)
    for i in range(depth):
        last = i == depth - 1
        c_v, c_u, c_g = 2 * ATT_WIDTH, 3 * ATT_WIDTH, 3 * ATT_WIDTH + 2 * SG_WIDTH
        later = (ffn2_w_gate[i], ffn2_w_up[i], ffn2_w_down[i], w_out[i], w_sg_out[i], w_att_out[i])
        casts = tuple((w, (0, w.shape[1])) for w in later) + ((w_in[i], (0, c_g, w_in.shape[2])),)
        x1, hn, w2_gate, w2_up, w2_down, w_out_b, w_sg_b, w_att_b, w_in_b, w_gates_b = _ffn(
            xt, ffn1_norm[i][None], bf(ffn1_w_gate[i]), bf(ffn1_w_up[i]), bf(ffn1_w_down[i]),
            mix_norm[i][None], post="hn", casts=casts)
        qk = _proj(hn, w_in_b, 0, c_v, epilogue="rope", tn=ATT_WIDTH, tables=tables)
        v = _proj(hn, w_in_b, c_v, ATT_WIDTH, epilogue="heads", tn=ATT_WIDTH)
        uv = _proj(hn, w_in_b, c_u, 2 * SG_WIDTH, epilogue="gelu", tn=SG_WIDTH)
        gates = _proj(hn, w_gates_b, 0, 2 * D_MODEL, epilogue="sigmoid", tn=D_MODEL)
        o_list, st_list = [], []
        for g in range(N_GROUPS):
            o_g, st_g = _attention(qk, v, g, batch)
            o_list.append(o_g)
            st_list.append(st_g)
        merged = _mix(o_list, st_list, uv, gates, sg_ln_g[i][None], sg_ln_b[i][None], sg_w[i], sg_b[i].T,
                      w_att_b, w_sg_b)
        x2 = _out_proj(merged, w_out_b, x1)
        (xt,) = _ffn(x2, ffn2_norm[i][None], w2_gate, w2_up, w2_down,
                     final_norm[None], post="final" if last else "plain")
    return xt.reshape(batch, seq, d)
```

```python
import functools

import jax
import jax.numpy as jnp
from jax import lax
from jax.experimental import pallas as pl
from jax.experimental.pallas import tpu as pltpu

F32 = jnp.float32
BF16 = jnp.bfloat16

D_MODEL = 2048
SEQ = 4096
HEAD_DIM = 128
HEADS_PER_GROUP = 4
DILATIONS = (1, 4, 16)
N_GROUPS = len(DILATIONS)
N_ATT_HEADS = N_GROUPS * HEADS_PER_GROUP
ATT_WIDTH = N_ATT_HEADS * HEAD_DIM
ATT_BLOCK = 128
QKV_TILE = 1024
ROPE_DIM = HEAD_DIM // 4
ROPE_HALF = ROPE_DIM // 2
ROPE_THETA = 500000.0
SG_CHUNK = 128
SG_GROUPS = 12
SG_GROUP_DIM = 128
SG_WIDTH = SG_GROUPS * SG_GROUP_DIM
NORM_EPS = 1e-6
LN_EPS = 1e-5

V7X_VMEM_BYTES = 64 * 1024 * 1024
V7X_LANES = 128
BF16_ROWS = 16
CAST_STEPS = 128
VMEM_LIMIT = 52 * 1024 * 1024


def _params(semantics, vmem_limit=VMEM_LIMIT):
    return pltpu.CompilerParams(dimension_semantics=semantics, vmem_limit_bytes=vmem_limit)


def _rms(x, g):
    return x * lax.rsqrt(jnp.mean(x * x, axis=-1, keepdims=True) + NORM_EPS) * g


def _rope_table_kernel(freq_ref, cos_ref, sin_lo_ref, sin_hi_ref):
    rows = cos_ref.shape[0]
    pos = (pl.program_id(0) * rows + lax.broadcasted_iota(jnp.int32, (rows, V7X_LANES), 0)).astype(F32)
    lane = lax.broadcasted_iota(jnp.int32, (rows, V7X_LANES), 1)
    ang = pos * freq_ref[...]
    c = jnp.cos(ang)
    s = jnp.sin(ang)
    cos_ref[...] = jnp.where(lane < ROPE_DIM, c, 1.0)
    sin_lo_ref[...] = jnp.where(lane < ROPE_HALF, -s, 0.0)
    sin_hi_ref[...] = jnp.where((lane >= ROPE_HALF) & (lane < ROPE_DIM), s, 0.0)


def _rope_tables():
    inv_freq = ROPE_THETA ** (-jnp.arange(0, ROPE_DIM, 2, dtype=F32) / ROPE_DIM)
    freq_lane = jnp.concatenate([inv_freq, inv_freq, jnp.zeros((V7X_LANES - ROPE_DIM,), F32)])[None, :]
    rows = 512
    tab = jax.ShapeDtypeStruct((SEQ, V7X_LANES), F32)
    spec = pl.BlockSpec((rows, V7X_LANES), lambda i: (i, 0))
    return pl.pallas_call(
        _rope_table_kernel,
        grid=(SEQ // rows,),
        in_specs=[pl.BlockSpec((1, V7X_LANES), lambda i: (0, 0))],
        out_specs=[spec, spec, spec],
        out_shape=[tab, tab, tab],
        compiler_params=_params(("parallel",)),
        name="rope_table",
    )(freq_lane)


def _cast_specs(casts, n_steps, step_of):
    in_specs, out_specs, out_shapes = [], [], []
    for w, splits in casts:
        rows, cols = w.shape
        assert splits[0] == 0 and splits[-1] == cols
        rb = next(r for r in range(BF16_ROWS, rows + 1, BF16_ROWS) if rows % r == 0 and rows // r <= n_steps)
        n_blocks = rows // rb
        block_of = lambda *idx, n_blocks=n_blocks: (jnp.minimum(step_of(*idx), n_blocks - 1), 0)
        in_specs.append(pl.BlockSpec((rb, cols), block_of))
        for c0, c1 in zip(splits[:-1], splits[1:]):
            out_specs.append(pl.BlockSpec((rb, c1 - c0), block_of))
            out_shapes.append(jax.ShapeDtypeStruct((rows, c1 - c0), BF16))
    return in_specs, out_specs, out_shapes


def _cast_blocks(cast_in, cast_out, cast_splits):
    cast_out = iter(cast_out)
    for src, splits in zip(cast_in, cast_splits):
        for c0, c1 in zip(splits[:-1], splits[1:]):
            next(cast_out)[...] = src[:, c0:c1].astype(BF16)


def _ffn_kernel(x_ref, g_ref, wg_ref, wu_ref, wd_ref, pg_ref, *refs, post, cast_splits):
    n_cast = len(cast_splits)
    cast_in, refs = refs[:n_cast], refs[n_cast:]
    if post == "hn":
        o_ref, hn_ref, *cast_out, xn_ref = refs
    else:
        o_ref, *cast_out, xn_ref = refs
    j = pl.program_id(1)

    @pl.when(j == 0)
    def _():
        xn_ref[...] = _rms(x_ref[...], g_ref[...]).astype(BF16)
        o_ref[...] = jnp.zeros_like(o_ref)

    _cast_blocks(cast_in, cast_out, cast_splits)

    xn = xn_ref[...]
    h = jnp.dot(xn, wg_ref[...], preferred_element_type=F32)
    u = jnp.dot(xn, wu_ref[...], preferred_element_type=F32)
    a = (jax.nn.silu(h) * u).astype(BF16)
    o_ref[...] += jnp.dot(a, wd_ref[...], preferred_element_type=F32)

    @pl.when(j == pl.num_programs(1) - 1)
    def _():
        y = x_ref[...] + 0.5 * o_ref[...]
        if post == "hn":
            o_ref[...] = y
            hn_ref[...] = _rms(y, pg_ref[...]).astype(BF16)
        elif post == "final":
            o_ref[...] = _rms(y, pg_ref[...])
        else:
            o_ref[...] = y


def _ffn(x, g, wg, wu, wd, post_g, *, post, casts=(), tm=512, tf=512):
    t, d = x.shape
    f = wg.shape[1]
    n_j = f // tf
    row = pl.BlockSpec((tm, d), lambda i, j: (i, 0))
    vec = pl.BlockSpec((1, d), lambda i, j: (0, 0))
    out_shape = [jax.ShapeDtypeStruct((t, d), F32)]
    out_specs = [row]
    if post == "hn":
        out_shape.append(jax.ShapeDtypeStruct((t, d), BF16))
        out_specs.append(row)
    cast_specs, cast_out_specs, cast_out_shapes = _cast_specs(
        casts, min(CAST_STEPS, (t // tm) * n_j), lambda i, j: i * n_j + j)
    out_specs += cast_out_specs
    out_shape += cast_out_shapes
    return pl.pallas_call(
        functools.partial(_ffn_kernel, post=post, cast_splits=tuple(s for _, s in casts)),
        grid=(t // tm, n_j),
        in_specs=[
            row,
            vec,
            pl.BlockSpec((d, tf), lambda i, j: (0, j)),
            pl.BlockSpec((d, tf), lambda i, j: (0, j)),
            pl.BlockSpec((tf, d), lambda i, j: (j, 0)),
            vec,
        ] + cast_specs,
        out_specs=out_specs,
        out_shape=out_shape,
        scratch_shapes=[pltpu.VMEM((tm, d), BF16)],
        compiler_params=_params(("arbitrary", "arbitrary")),
        name="ffn_" + post,
    )(x, g, wg, wu, wd, post_g, *[w for w, _ in casts])


def _store_head(o_ref, stage_ref, h, val):
    dil = DILATIONS[h // HEADS_PER_GROUP]
    if dil == 1:
        o_ref[h] = val.astype(BF16)
        return
    slot = h - HEADS_PER_GROUP
    stage_ref[slot] = val
    rows = QKV_TILE // dil
    for r in range(dil):
        o_ref[h, r * rows:(r + 1) * rows, :] = stage_ref[slot, pl.ds(r, rows, stride=dil), :].astype(BF16)


def _proj_kernel(h_ref, w_ref, *refs, epilogue, cast_splits):
    n_cast = len(cast_splits)
    n_cast_out = sum(len(s) - 1 for s in cast_splits)
    n_tables = 3 if epilogue == "rope" else 0
    cast_in = refs[n_tables:n_tables + n_cast]
    cast_out = refs[n_tables + n_cast + 1:n_tables + n_cast + 1 + n_cast_out]
    refs = refs[:n_tables] + refs[n_tables + n_cast:n_tables + n_cast + 1] + refs[n_tables + n_cast + 1 + n_cast_out:]
    _cast_blocks(cast_in, cast_out, cast_splits)

    acc = jnp.dot(h_ref[...], w_ref[...], preferred_element_type=F32)
    n_heads = acc.shape[1] // HEAD_DIM
    if epilogue == "rope":
        cos_ref, sin_lo_ref, sin_hi_ref, o_ref, stage_ref = refs
        scale = jnp.where(pl.program_id(1) == 0, HEAD_DIM ** -0.5, 1.0).astype(F32)
        cos = cos_ref[...] * scale
        sin_lo = sin_lo_ref[...] * scale
        sin_hi = sin_hi_ref[...] * scale
        for h in range(n_heads):
            t = acc[:, h * HEAD_DIM:(h + 1) * HEAD_DIM]
            rot = (t * cos
                   + pltpu.roll(t, HEAD_DIM - ROPE_HALF, 1) * sin_lo
                   + pltpu.roll(t, ROPE_HALF, 1) * sin_hi)
            _store_head(o_ref, stage_ref, h, rot)
    elif epilogue == "heads":
        o_ref, stage_ref = refs
        for h in range(n_heads):
            _store_head(o_ref, stage_ref, h, acc[:, h * HEAD_DIM:(h + 1) * HEAD_DIM])
    else:
        (o_ref,) = refs
        if epilogue == "gelu":
            acc = 0.5 * acc * (1.0 + lax.erf(acc * (0.5 ** 0.5)))
        elif epilogue == "sigmoid":
            acc = jax.nn.sigmoid(acc)
        o_ref[...] = acc.astype(o_ref.dtype)


def _proj(hn, w, col0, ncols, *, epilogue, tn, tables=(), casts=(), tm=QKV_TILE):
    t, d = hn.shape
    assert col0 % tn == 0 and ncols % tn == 0
    c0 = col0 // tn
    n_j = ncols // tn
    scratch = []
    in_specs = [
        pl.BlockSpec((tm, d), lambda i, j: (i, 0)),
        pl.BlockSpec((d, tn), lambda i, j: (0, c0 + j)),
    ]
    seq_blocks = SEQ // tm
    for _ in tables:
        in_specs.append(pl.BlockSpec((tm, V7X_LANES), lambda i, j: (i % seq_blocks, 0)))
    if epilogue in ("rope", "heads"):
        assert tm == QKV_TILE and tn == ATT_WIDTH
        out_spec = pl.BlockSpec((N_ATT_HEADS, tm, HEAD_DIM), lambda i, j: (j, i, 0))
        out_shape = jax.ShapeDtypeStruct((ncols // HEAD_DIM, t, HEAD_DIM), BF16)
        scratch = [pltpu.VMEM((N_ATT_HEADS - HEADS_PER_GROUP, tm, HEAD_DIM), F32)]
    else:
        out_spec = pl.BlockSpec((tm, tn), lambda i, j: (i, j))
        out_shape = jax.ShapeDtypeStruct((t, ncols), BF16)
    cast_specs, cast_out_specs, cast_out_shapes = _cast_specs(casts, (t // tm) * n_j, lambda i, j: i * n_j + j)
    res = pl.pallas_call(
        functools.partial(_proj_kernel, epilogue=epilogue, cast_splits=tuple(s for _, s in casts)),
        grid=(t // tm, n_j),
        in_specs=in_specs + cast_specs,
        out_specs=[out_spec] + cast_out_specs,
        out_shape=[out_shape] + cast_out_shapes,
        scratch_shapes=scratch,
        compiler_params=_params(("arbitrary", "arbitrary")),
        name="proj_" + epilogue,
    )(hn, w, *tables, *[cw for cw, _ in casts])
    return res if casts else res[0]


def _attn_kernel(q_ref, k_ref, v_ref, o_ref, st_ref, *, dil):
    head = pl.program_id(1)
    n_qb = SEQ // (ATT_BLOCK * dil)

    qi = lax.broadcasted_iota(jnp.int32, (ATT_BLOCK, 2 * ATT_BLOCK), 0)
    kj = lax.broadcasted_iota(jnp.int32, (ATT_BLOCK, 2 * ATT_BLOCK), 1)
    diff = qi + ATT_BLOCK - kj
    band = (diff >= 0) & (diff <= ATT_BLOCK)
    band_first = band & (kj >= ATT_BLOCK)
    lane = lax.broadcasted_iota(jnp.int32, (ATT_BLOCK, V7X_LANES), 1)
    ones = jnp.ones((2 * ATT_BLOCK, HEAD_DIM), BF16)
    zeros = jnp.zeros((ATT_BLOCK, HEAD_DIM), BF16)

    @pl.when(head == 0)
    def _():
        st_ref[...] = jnp.zeros_like(st_ref)

    rows_per = QKV_TILE // dil
    run = min(ATT_BLOCK, rows_per)

    def load(ref, qb, r):
        parts = []
        for idx in range(qb * ATT_BLOCK, (qb + 1) * ATT_BLOCK, run):
            tile, local = divmod(idx, rows_per)
            parts.append(ref[pl.ds(tile * QKV_TILE + r * rows_per + local, run), :])
        return parts[0] if len(parts) == 1 else jnp.concatenate(parts, axis=0)

    for r in range(dil):
        k_prev, v_prev = zeros, zeros
        for qb in range(n_qb):
            q, k_cur, v_cur = load(q_ref, qb, r), load(k_ref, qb, r), load(v_ref, qb, r)
            sel = (pl.ds(qb * ATT_BLOCK * dil + r, ATT_BLOCK, stride=dil) if dil > 1
                   else pl.ds(qb * ATT_BLOCK, ATT_BLOCK))
            k_win = jnp.concatenate([k_prev, k_cur], axis=0)
            v_win = jnp.concatenate([v_prev, v_cur], axis=0)
            s = lax.dot_general(q, k_win, (((1,), (1,)), ((), ())), preferred_element_type=F32)
            s = jnp.where(band_first if qb == 0 else band, s, -jnp.inf)
            m = jnp.max(s, axis=-1, keepdims=True)
            p = jnp.exp(s - m).astype(BF16)
            pv = jnp.dot(p, jnp.concatenate([v_win, ones], axis=1), preferred_element_type=F32)
            l_b = pv[:, HEAD_DIM:]
            o_ref[sel, :] = pv[:, :HEAD_DIM] / l_b
            st = st_ref[sel, :]
            st = jnp.where(lane == head, m, st)
            st_ref[sel, :] = jnp.where(lane == HEADS_PER_GROUP + head, l_b, st)
            k_prev, v_prev = k_cur, v_cur


def _attention(qk, v, group, batch):
    dil = DILATIONS[group]
    t = qk.shape[1]
    h0 = group * HEADS_PER_GROUP

    def seq(slab0):
        return pl.BlockSpec((None, SEQ, HEAD_DIM), lambda b, h: (slab0 + h, b, 0))

    return pl.pallas_call(
        functools.partial(_attn_kernel, dil=dil),
        grid=(batch, HEADS_PER_GROUP),
        in_specs=[seq(h0), seq(N_ATT_HEADS + h0), seq(h0)],
        out_specs=[
            pl.BlockSpec((None, SEQ, HEAD_DIM), lambda b, h: (h, b, 0)),
            pl.BlockSpec((SEQ, V7X_LANES), lambda b, h: (b, 0)),
        ],
        out_shape=[
            jax.ShapeDtypeStruct((HEADS_PER_GROUP, t, HEAD_DIM), F32),
            jax.ShapeDtypeStruct((t, V7X_LANES), F32),
        ],
        compiler_params=_params(("parallel", "arbitrary")),
        name=f"attn_d{dil}",
    )(qk, qk, v)


def _mix_prepare(o_refs, stats, u_ref, vs_ref, lng_ref, lnb_ref, sgw_ref, sgb_ref, oatt_ref, gated_ref):
    tm = oatt_ref.shape[0]
    for h in range(HEADS_PER_GROUP):
        ms = [s[:, h:h + 1] for s in stats]
        ls = [s[:, HEADS_PER_GROUP + h:HEADS_PER_GROUP + h + 1] for s in stats]
        m_all = jnp.maximum(jnp.maximum(ms[0], ms[1]), ms[2])
        ws = [l * jnp.exp(m - m_all) for l, m in zip(ls, ms)]
        num = ws[0] * o_refs[0][h] + ws[1] * o_refs[1][h] + ws[2] * o_refs[2][h]
        oatt_ref[:, h * HEAD_DIM:(h + 1) * HEAD_DIM] = (num / (ws[0] + ws[1] + ws[2])).astype(BF16)

    vs = vs_ref[...].astype(F32)
    mu = jnp.mean(vs, axis=-1, keepdims=True)
    var = jnp.mean(jnp.square(vs - mu), axis=-1, keepdims=True)
    vn = ((vs - mu) * lax.rsqrt(var + LN_EPS) * lng_ref[...] + lnb_ref[...]).astype(BF16)
    ti = lax.broadcasted_iota(jnp.int32, (SG_CHUNK, SG_CHUNK), 0)
    si = lax.broadcasted_iota(jnp.int32, (SG_CHUNK, SG_CHUNK), 1)
    causal = si <= ti
    for g in range(SG_GROUPS):
        cols = slice(g * SG_GROUP_DIM, (g + 1) * SG_GROUP_DIM)
        w_sp = jnp.where(causal, sgw_ref[g], 0.0).astype(BF16)
        bias = sgb_ref[:, g:g + 1]
        for c in range(tm // SG_CHUNK):
            rows = slice(c * SG_CHUNK, (c + 1) * SG_CHUNK)
            spatial = jnp.dot(w_sp, vn[rows, cols], preferred_element_type=F32) + bias
            gated_ref[rows, cols] = (u_ref[rows, cols].astype(F32) * spatial).astype(BF16)


def _mix_kernel(o0_ref, o1_ref, o2_ref, s0_ref, s1_ref, s2_ref, u_ref, vs_ref, ga_ref, gs_ref,
                lng_ref, lnb_ref, sgw_ref, sgb_ref, wa_ref, ws_ref, out_ref,
                oatt_a, gated_a, oatt_b, gated_b):
    i = pl.program_id(0)

    @pl.when(i == 0)
    def _():
        oatt_b[...] = jnp.zeros_like(oatt_b)
        gated_b[...] = jnp.zeros_like(gated_b)

    def step(oatt_w, gated_w, oatt_r, gated_r):
        y_att = jnp.dot(oatt_r[...], wa_ref[...], preferred_element_type=F32)
        y_sg = jnp.dot(gated_r[...], ws_ref[...], preferred_element_type=F32)
        merged = ga_ref[...].astype(F32) * y_att + gs_ref[...].astype(F32) * y_sg
        out_ref[...] = merged.astype(out_ref.dtype)
        _mix_prepare((o0_ref, o1_ref, o2_ref), (s0_ref[...], s1_ref[...], s2_ref[...]), u_ref, vs_ref,
                     lng_ref, lnb_ref, sgw_ref, sgb_ref, oatt_w, gated_w)

    @pl.when(i % 2 == 0)
    def _():
        step(oatt_a, gated_a, oatt_b, gated_b)

    @pl.when(i % 2 == 1)
    def _():
        step(oatt_b, gated_b, oatt_a, gated_a)


def _mix(o_list, st_list, uv, gates, ln_g, ln_b, sg_w, sg_b_t, w_att, w_sg, *, tm=512):
    t = uv.shape[0]
    n = t // tm
    cur = lambda i: jnp.minimum(i, n - 1)
    lag = lambda i: jnp.maximum(i - 1, 0)
    prep_row = lambda width, blk: pl.BlockSpec((tm, width), lambda i: (cur(i), blk))
    lag_row = lambda width, blk: pl.BlockSpec((tm, width), lambda i: (lag(i), blk))
    const2 = lambda shape: pl.BlockSpec(shape, lambda i: (0, 0), pipeline_mode=pl.Buffered(1))
    in_specs = (
        [pl.BlockSpec((HEADS_PER_GROUP, tm, HEAD_DIM), lambda i: (0, cur(i), 0))] * 3
        + [prep_row(V7X_LANES, 0)] * 3
        + [prep_row(SG_WIDTH, 0), prep_row(SG_WIDTH, 1), lag_row(D_MODEL, 0), lag_row(D_MODEL, 1)]
        + [const2((1, SG_WIDTH)), const2((1, SG_WIDTH)),
           pl.BlockSpec(sg_w.shape, lambda i: (0, 0, 0), pipeline_mode=pl.Buffered(1)),
           const2(sg_b_t.shape), const2(w_att.shape), const2(w_sg.shape)]
    )
    att_w = HEADS_PER_GROUP * HEAD_DIM
    return pl.pallas_call(
        _mix_kernel,
        grid=(n + 1,),
        in_specs=in_specs,
        out_specs=pl.BlockSpec((tm, D_MODEL), lambda i: (lag(i), 0)),
        out_shape=jax.ShapeDtypeStruct((t, D_MODEL), BF16),
        scratch_shapes=[pltpu.VMEM((tm, att_w), BF16), pltpu.VMEM((tm, SG_WIDTH), BF16),
                        pltpu.VMEM((tm, att_w), BF16), pltpu.VMEM((tm, SG_WIDTH), BF16)],
        compiler_params=_params(("arbitrary",)),
        name="mix",
    )(*o_list, *st_list, uv, uv, gates, gates, ln_g, ln_b, sg_w, sg_b_t, w_att, w_sg)


def _out_proj_kernel(m_ref, w_ref, x_ref, o_ref):
    o_ref[...] = x_ref[...] + jnp.dot(m_ref[...], w_ref[...], preferred_element_type=F32)


def _out_proj(merged, w, x, *, tm=1024, tn=1024):
    t, d = merged.shape
    n = w.shape[1]
    return pl.pallas_call(
        _out_proj_kernel,
        grid=(t // tm, n // tn),
        in_specs=[
            pl.BlockSpec((tm, d), lambda i, j: (i, 0)),
            pl.BlockSpec((d, tn), lambda i, j: (0, j)),
            pl.BlockSpec((tm, tn), lambda i, j: (i, j)),
        ],
        out_specs=pl.BlockSpec((tm, tn), lambda i, j: (i, j)),
        out_shape=jax.ShapeDtypeStruct((t, n), F32),
        compiler_params=_params(("parallel", "arbitrary")),
        name="out_proj",
    )(merged, w, x)


def kernel(x, ffn1_norm, ffn1_w_gate, ffn1_w_up, ffn1_w_down, mix_norm, w_in, sg_ln_g, sg_ln_b, sg_w, sg_b, w_att_out, w_sg_out, w_out, ffn2_norm, ffn2_w_gate, ffn2_w_up, ffn2_w_down, final_norm):
    batch, seq, d = x.shape
    assert (seq, d) == (SEQ, D_MODEL)
    t = batch * seq
    depth = ffn1_norm.shape[0]
    xt = x.reshape(t, d)
    tables = _rope_tables()
    bf = lambda a: a.astype(BF16)
    for i in range(depth):
        last = i == depth - 1
        c_v, c_u, c_g = 2 * ATT_WIDTH, 3 * ATT_WIDTH, 3 * ATT_WIDTH + 2 * SG_WIDTH
        whole = lambda w: (w, (0, w.shape[1]))
        x1, hn, w_in_b, w_gates_b = _ffn(
            xt, ffn1_norm[i][None], bf(ffn1_w_gate[i]), bf(ffn1_w_up[i]), bf(ffn1_w_down[i]),
            mix_norm[i][None], post="hn", casts=((w_in[i], (0, c_g, w_in.shape[2])),))
        qk, w_out_b, w_sg_b, w_att_b = _proj(
            hn, w_in_b, 0, c_v, epilogue="rope", tn=ATT_WIDTH, tables=tables,
            casts=(whole(w_out[i]), whole(w_sg_out[i]), whole(w_att_out[i])))
        v = _proj(hn, w_in_b, c_v, ATT_WIDTH, epilogue="heads", tn=ATT_WIDTH)
        uv, w2_gate, w2_up = _proj(hn, w_in_b, c_u, 2 * SG_WIDTH, epilogue="gelu", tn=SG_WIDTH,
                                   casts=(whole(ffn2_w_gate[i]), whole(ffn2_w_up[i])))
        gates, w2_down = _proj(hn, w_gates_b, 0, 2 * D_MODEL, epilogue="sigmoid", tn=D_MODEL,
                               casts=(whole(ffn2_w_down[i]),))
        o_list, st_list = [], []
        for g in range(N_GROUPS):
            o_g, st_g = _attention(qk, v, g, batch)
            o_list.append(o_g)
            st_list.append(st_g)
        merged = _mix(o_list, st_list, uv, gates, sg_ln_g[i][None], sg_ln_b[i][None], sg_w[i], sg_b[i].T,
                      w_att_b, w_sg_b)
        x2 = _out_proj(merged, w_out_b, x1)
        (xt,) = _ffn(x2, ffn2_norm[i][None], w2_gate, w2_up, w2_down,
                     final_norm[None], post="final" if last else "plain")
    return xt.reshape(batch, seq, d)
```

```python
import functools

import jax
import jax.numpy as jnp
from jax import lax
from jax.experimental import pallas as pl
from jax.experimental.pallas import tpu as pltpu

F32 = jnp.float32
BF16 = jnp.bfloat16

D_MODEL = 2048
SEQ = 4096
HEAD_DIM = 128
HEADS_PER_GROUP = 4
DILATIONS = (1, 4, 16)
N_GROUPS = len(DILATIONS)
N_ATT_HEADS = N_GROUPS * HEADS_PER_GROUP
ATT_WIDTH = N_ATT_HEADS * HEAD_DIM
ATT_BLOCK = 128
QKV_TILE = 1024
ROPE_DIM = HEAD_DIM // 4
ROPE_HALF = ROPE_DIM // 2
ROPE_THETA = 500000.0
SG_CHUNK = 128
SG_GROUPS = 12
SG_GROUP_DIM = 128
SG_WIDTH = SG_GROUPS * SG_GROUP_DIM
NORM_EPS = 1e-6
LN_EPS = 1e-5

V7X_VMEM_BYTES = 64 * 1024 * 1024
V7X_LANES = 128
BF16_ROWS = 16
CAST_STEPS = 128
VMEM_LIMIT = 52 * 1024 * 1024


def _params(semantics, vmem_limit=VMEM_LIMIT):
    return pltpu.CompilerParams(dimension_semantics=semantics, vmem_limit_bytes=vmem_limit)


def _rms(x, g):
    return x * lax.rsqrt(jnp.mean(x * x, axis=-1, keepdims=True) + NORM_EPS) * g


ROPE_PAIR_LANE = V7X_LANES // 2


def _rope_lane_masks(lane):
    first = lane < ROPE_HALF
    second = (lane >= ROPE_PAIR_LANE) & (lane < ROPE_PAIR_LANE + ROPE_HALF)
    return first, second


def _rope_lane_layout(x):
    cols = x.shape[1]
    lane = lax.broadcasted_iota(jnp.int32, x.shape, 1) % V7X_LANES
    shift = ROPE_PAIR_LANE - ROPE_HALF
    to_pair_lane = (lane >= ROPE_PAIR_LANE) & (lane < ROPE_PAIR_LANE + ROPE_HALF)
    from_pair_lane = (lane >= ROPE_HALF) & (lane < ROPE_DIM)
    return jnp.where(to_pair_lane, pltpu.roll(x, shift, 1),
                     jnp.where(from_pair_lane, pltpu.roll(x, cols - shift, 1), x))


def _rope_table_kernel(freq_ref, cos_ref, sin_ref):
    rows = cos_ref.shape[0]
    pos = (pl.program_id(0) * rows + lax.broadcasted_iota(jnp.int32, (rows, V7X_LANES), 0)).astype(F32)
    first, second = _rope_lane_masks(lax.broadcasted_iota(jnp.int32, (rows, V7X_LANES), 1))
    ang = pos * freq_ref[...]
    c = jnp.cos(ang)
    s = jnp.sin(ang)
    cos_ref[...] = jnp.where(first | second, c, 1.0)
    sin_ref[...] = jnp.where(first, -s, jnp.where(second, s, 0.0))


def _rope_tables():
    inv_freq = ROPE_THETA ** (-jnp.arange(0, ROPE_DIM, 2, dtype=F32) / ROPE_DIM)
    gap = jnp.zeros((ROPE_PAIR_LANE - ROPE_HALF,), F32)
    freq_lane = jnp.concatenate([inv_freq, gap, inv_freq, gap])[None, :]
    rows = 512
    tab = jax.ShapeDtypeStruct((SEQ, V7X_LANES), F32)
    spec = pl.BlockSpec((rows, V7X_LANES), lambda i: (i, 0))
    return pl.pallas_call(
        _rope_table_kernel,
        grid=(SEQ // rows,),
        in_specs=[pl.BlockSpec((1, V7X_LANES), lambda i: (0, 0))],
        out_specs=[spec, spec],
        out_shape=[tab, tab],
        compiler_params=_params(("parallel",)),
        name="rope_table",
    )(freq_lane)


def _cast_specs(casts, n_steps, step_of):
    in_specs, out_specs, out_shapes = [], [], []
    for w, splits, _ in casts:
        rows, cols = w.shape
        assert splits[0] == 0 and splits[-1] == cols
        rb = next(r for r in range(BF16_ROWS, rows + 1, BF16_ROWS) if rows % r == 0 and rows // r <= n_steps)
        n_blocks = rows // rb
        block_of = lambda *idx, n_blocks=n_blocks: (jnp.minimum(step_of(*idx), n_blocks - 1), 0)
        in_specs.append(pl.BlockSpec((rb, cols), block_of))
        for c0, c1 in zip(splits[:-1], splits[1:]):
            out_specs.append(pl.BlockSpec((rb, c1 - c0), block_of))
            out_shapes.append(jax.ShapeDtypeStruct((rows, c1 - c0), BF16))
    return in_specs, out_specs, out_shapes


def _cast_blocks(cast_in, cast_out, cast_meta):
    cast_out = iter(cast_out)
    for src, (splits, rope_cols) in zip(cast_in, cast_meta):
        for c0, c1 in zip(splits[:-1], splits[1:]):
            dst = next(cast_out)
            n_rope = min(max(rope_cols - c0, 0), c1 - c0)
            if n_rope:
                dst[:, :n_rope] = _rope_lane_layout(src[:, c0:c0 + n_rope]).astype(BF16)
            if n_rope < c1 - c0:
                dst[:, n_rope:] = src[:, c0 + n_rope:c1].astype(BF16)


def _cast_meta(casts):
    return tuple((splits, rope_cols) for _, splits, rope_cols in casts)


def _ffn_kernel(x_ref, g_ref, wg_ref, wu_ref, wd_ref, pg_ref, *refs, post, cast_meta):
    n_cast = len(cast_meta)
    cast_in, refs = refs[:n_cast], refs[n_cast:]
    if post == "hn":
        o_ref, hn_ref, *cast_out, xn_ref = refs
    else:
        o_ref, *cast_out, xn_ref = refs
    j = pl.program_id(1)

    @pl.when(j == 0)
    def _():
        xn_ref[...] = _rms(x_ref[...], g_ref[...]).astype(BF16)
        o_ref[...] = jnp.zeros_like(o_ref)

    _cast_blocks(cast_in, cast_out, cast_meta)

    xn = xn_ref[...]
    h = jnp.dot(xn, wg_ref[...], preferred_element_type=F32)
    u = jnp.dot(xn, wu_ref[...], preferred_element_type=F32)
    a = (jax.nn.silu(h) * u).astype(BF16)
    o_ref[...] += jnp.dot(a, wd_ref[...], preferred_element_type=F32)

    @pl.when(j == pl.num_programs(1) - 1)
    def _():
        y = x_ref[...] + 0.5 * o_ref[...]
        if post == "hn":
            o_ref[...] = y
            hn_ref[...] = _rms(y, pg_ref[...]).astype(BF16)
        elif post == "final":
            o_ref[...] = _rms(y, pg_ref[...])
        else:
            o_ref[...] = y


def _ffn(x, g, wg, wu, wd, post_g, *, post, casts=(), tm=512, tf=512):
    t, d = x.shape
    f = wg.shape[1]
    n_j = f // tf
    row = pl.BlockSpec((tm, d), lambda i, j: (i, 0))
    vec = pl.BlockSpec((1, d), lambda i, j: (0, 0))
    out_shape = [jax.ShapeDtypeStruct((t, d), F32)]
    out_specs = [row]
    if post == "hn":
        out_shape.append(jax.ShapeDtypeStruct((t, d), BF16))
        out_specs.append(row)
    cast_specs, cast_out_specs, cast_out_shapes = _cast_specs(
        casts, min(CAST_STEPS, (t // tm) * n_j), lambda i, j: i * n_j + j)
    out_specs += cast_out_specs
    out_shape += cast_out_shapes
    return pl.pallas_call(
        functools.partial(_ffn_kernel, post=post, cast_meta=_cast_meta(casts)),
        grid=(t // tm, n_j),
        in_specs=[
            row,
            vec,
            pl.BlockSpec((d, tf), lambda i, j: (0, j)),
            pl.BlockSpec((d, tf), lambda i, j: (0, j)),
            pl.BlockSpec((tf, d), lambda i, j: (j, 0)),
            vec,
        ] + cast_specs,
        out_specs=out_specs,
        out_shape=out_shape,
        scratch_shapes=[pltpu.VMEM((tm, d), BF16)],
        compiler_params=_params(("arbitrary", "arbitrary")),
        name="ffn_" + post,
    )(x, g, wg, wu, wd, post_g, *[c[0] for c in casts])


def _store_head(o_ref, stage_refs, h, val):
    dil = DILATIONS[h // HEADS_PER_GROUP]
    if dil == 1:
        o_ref[h] = val.astype(BF16)
        return
    stage = stage_refs[h - HEADS_PER_GROUP]
    stage[...] = val
    rows = QKV_TILE // dil
    for r in range(dil):
        o_ref[h, r * rows:(r + 1) * rows, :] = stage[pl.ds(r, rows, stride=dil), :].astype(BF16)


def _proj_kernel(h_ref, w_ref, *refs, epilogue, cast_meta):
    n_cast = len(cast_meta)
    n_cast_out = sum(len(splits) - 1 for splits, _ in cast_meta)
    n_tables = 2 if epilogue == "rope" else 0
    cast_in = refs[n_tables:n_tables + n_cast]
    cast_out = refs[n_tables + n_cast + 1:n_tables + n_cast + 1 + n_cast_out]
    refs = refs[:n_tables] + refs[n_tables + n_cast:n_tables + n_cast + 1] + refs[n_tables + n_cast + 1 + n_cast_out:]
    _cast_blocks(cast_in, cast_out, cast_meta)

    acc = jnp.dot(h_ref[...], w_ref[...], preferred_element_type=F32)
    n_heads = acc.shape[1] // HEAD_DIM
    if epilogue == "rope":
        cos_ref, sin_ref, o_ref, *stage_ref = refs
        scale = jnp.where(pl.program_id(1) == 0, HEAD_DIM ** -0.5, 1.0).astype(F32)
        cos = cos_ref[...] * scale
        sin = sin_ref[...] * scale
        for h in range(n_heads):
            t = acc[:, h * HEAD_DIM:(h + 1) * HEAD_DIM]
            _store_head(o_ref, stage_ref, h, t * cos + pltpu.roll(t, ROPE_PAIR_LANE, 1) * sin)
        return
    if epilogue == "heads":
        o_ref, *stage_ref = refs
        for h in range(n_heads):
            _store_head(o_ref, stage_ref, h, acc[:, h * HEAD_DIM:(h + 1) * HEAD_DIM])
        return

    (o_ref,) = refs
    if epilogue == "gelu":
        acc = 0.5 * acc * (1.0 + lax.erf(acc * (0.5 ** 0.5)))
    elif epilogue == "sigmoid":
        acc = jax.nn.sigmoid(acc)
    o_ref[...] = acc.astype(o_ref.dtype)


def _proj(hn, w, col0, ncols, *, epilogue, tn, tables=(), casts=(), tm=QKV_TILE):
    t, d = hn.shape
    assert col0 % tn == 0 and ncols % tn == 0
    c0 = col0 // tn
    n_j = ncols // tn
    scratch = []
    in_specs = [
        pl.BlockSpec((tm, d), lambda i, j: (i, 0)),
        pl.BlockSpec((d, tn), lambda i, j: (0, c0 + j)),
    ]
    seq_blocks = SEQ // tm
    for _ in tables:
        in_specs.append(pl.BlockSpec((tm, V7X_LANES), lambda i, j: (i % seq_blocks, 0)))
    if epilogue in ("rope", "heads"):
        assert tm == QKV_TILE and tn == ATT_WIDTH
        out_spec = pl.BlockSpec((N_ATT_HEADS, tm, HEAD_DIM), lambda i, j: (j, i, 0))
        out_shape = jax.ShapeDtypeStruct((ncols // HEAD_DIM, t, HEAD_DIM), BF16)
        scratch = [pltpu.VMEM((tm, HEAD_DIM), F32) for _ in range(N_ATT_HEADS - HEADS_PER_GROUP)]
    else:
        out_spec = pl.BlockSpec((tm, tn), lambda i, j: (i, j))
        out_shape = jax.ShapeDtypeStruct((t, ncols), BF16)
    cast_specs, cast_out_specs, cast_out_shapes = _cast_specs(casts, (t // tm) * n_j, lambda i, j: i * n_j + j)
    res = pl.pallas_call(
        functools.partial(_proj_kernel, epilogue=epilogue, cast_meta=_cast_meta(casts)),
        grid=(t // tm, n_j),
        in_specs=in_specs + cast_specs,
        out_specs=[out_spec] + cast_out_specs,
        out_shape=[out_shape] + cast_out_shapes,
        scratch_shapes=scratch,
        compiler_params=_params(("arbitrary", "arbitrary")),
        name="proj_" + epilogue,
    )(hn, w, *tables, *[c[0] for c in casts])
    return res if casts else res[0]


def _attn_kernel(q_ref, k_ref, v_ref, o_ref, st_ref, *, dil):
    head = pl.program_id(1)
    n_qb = SEQ // (ATT_BLOCK * dil)

    qi = lax.broadcasted_iota(jnp.int32, (ATT_BLOCK, 2 * ATT_BLOCK), 0)
    kj = lax.broadcasted_iota(jnp.int32, (ATT_BLOCK, 2 * ATT_BLOCK), 1)
    diff = qi + ATT_BLOCK - kj
    band = (diff >= 0) & (diff <= ATT_BLOCK)
    band_first = band & (kj >= ATT_BLOCK)
    lane = lax.broadcasted_iota(jnp.int32, (ATT_BLOCK, V7X_LANES), 1)
    ones = jnp.ones((2 * ATT_BLOCK, HEAD_DIM), BF16)
    zeros = jnp.zeros((ATT_BLOCK, HEAD_DIM), BF16)

    @pl.when(head == 0)
    def _():
        st_ref[...] = jnp.zeros_like(st_ref)

    rows_per = QKV_TILE // dil
    run = min(ATT_BLOCK, rows_per)

    def load(ref, qb, r):
        parts = []
        for idx in range(qb * ATT_BLOCK, (qb + 1) * ATT_BLOCK, run):
            tile, local = divmod(idx, rows_per)
            parts.append(ref[pl.ds(tile * QKV_TILE + r * rows_per + local, run), :])
        return parts[0] if len(parts) == 1 else jnp.concatenate(parts, axis=0)

    for r in range(dil):
        k_prev, v_prev = zeros, zeros
        for qb in range(n_qb):
            q, k_cur, v_cur = load(q_ref, qb, r), load(k_ref, qb, r), load(v_ref, qb, r)
            sel = (pl.ds(qb * ATT_BLOCK * dil + r, ATT_BLOCK, stride=dil) if dil > 1
                   else pl.ds(qb * ATT_BLOCK, ATT_BLOCK))
            k_win = jnp.concatenate([k_prev, k_cur], axis=0)
            v_win = jnp.concatenate([v_prev, v_cur], axis=0)
            s = lax.dot_general(q, k_win, (((1,), (1,)), ((), ())), preferred_element_type=F32)
            s = jnp.where(band_first if qb == 0 else band, s, -jnp.inf)
            m = jnp.max(s, axis=-1, keepdims=True)
            p = jnp.exp(s - m).astype(BF16)
            pv = jnp.dot(p, jnp.concatenate([v_win, ones], axis=1), preferred_element_type=F32)
            l_b = pv[:, HEAD_DIM:]
            o_ref[sel, :] = pv[:, :HEAD_DIM] / l_b
            st = st_ref[sel, :]
            st = jnp.where(lane == head, m, st)
            st_ref[sel, :] = jnp.where(lane == HEADS_PER_GROUP + head, l_b, st)
            k_prev, v_prev = k_cur, v_cur


def _attention(qk, v, group, batch):
    dil = DILATIONS[group]
    t = qk.shape[1]
    h0 = group * HEADS_PER_GROUP

    def seq(slab0):
        return pl.BlockSpec((None, SEQ, HEAD_DIM), lambda b, h: (slab0 + h, b, 0))

    return pl.pallas_call(
        functools.partial(_attn_kernel, dil=dil),
        grid=(batch, HEADS_PER_GROUP),
        in_specs=[seq(h0), seq(N_ATT_HEADS + h0), seq(h0)],
        out_specs=[
            pl.BlockSpec((None, SEQ, HEAD_DIM), lambda b, h: (h, b, 0)),
            pl.BlockSpec((SEQ, V7X_LANES), lambda b, h: (b, 0)),
        ],
        out_shape=[
            jax.ShapeDtypeStruct((HEADS_PER_GROUP, t, HEAD_DIM), F32),
            jax.ShapeDtypeStruct((t, V7X_LANES), F32),
        ],
        compiler_params=_params(("parallel", "arbitrary")),
        name=f"attn_d{dil}",
    )(qk, qk, v)


def _mix_prepare(o_refs, stats, u_ref, vs_ref, lng_ref, lnb_ref, sgw_ref, sgb_ref, oatt_ref, gated_ref):
    tm = oatt_ref.shape[0]
    for h in range(HEADS_PER_GROUP):
        ms = [s[:, h:h + 1] for s in stats]
        ls = [s[:, HEADS_PER_GROUP + h:HEADS_PER_GROUP + h + 1] for s in stats]
        m_all = jnp.maximum(jnp.maximum(ms[0], ms[1]), ms[2])
        ws = [l * jnp.exp(m - m_all) for l, m in zip(ls, ms)]
        num = ws[0] * o_refs[0][h] + ws[1] * o_refs[1][h] + ws[2] * o_refs[2][h]
        oatt_ref[:, h * HEAD_DIM:(h + 1) * HEAD_DIM] = (num / (ws[0] + ws[1] + ws[2])).astype(BF16)

    vs = vs_ref[...].astype(F32)
    mu = jnp.mean(vs, axis=-1, keepdims=True)
    var = jnp.mean(jnp.square(vs - mu), axis=-1, keepdims=True)
    vn = ((vs - mu) * lax.rsqrt(var + LN_EPS) * lng_ref[...] + lnb_ref[...]).astype(BF16)
    ti = lax.broadcasted_iota(jnp.int32, (SG_CHUNK, SG_CHUNK), 0)
    si = lax.broadcasted_iota(jnp.int32, (SG_CHUNK, SG_CHUNK), 1)
    causal = si <= ti
    for g in range(SG_GROUPS):
        cols = slice(g * SG_GROUP_DIM, (g + 1) * SG_GROUP_DIM)
        w_sp = jnp.where(causal, sgw_ref[g], 0.0).astype(BF16)
        bias = sgb_ref[:, g:g + 1]
        for c in range(tm // SG_CHUNK):
            rows = slice(c * SG_CHUNK, (c + 1) * SG_CHUNK)
            spatial = jnp.dot(w_sp, vn[rows, cols], preferred_element_type=F32) + bias
            gated_ref[rows, cols] = (u_ref[rows, cols].astype(F32) * spatial).astype(BF16)


def _mix_kernel(o0_ref, o1_ref, o2_ref, s0_ref, s1_ref, s2_ref, u_ref, vs_ref, ga_ref, gs_ref,
                lng_ref, lnb_ref, sgw_ref, sgb_ref, wa_ref, ws_ref, out_ref,
                oatt_a, gated_a, oatt_b, gated_b):
    i = pl.program_id(0)

    @pl.when(i == 0)
    def _():
        oatt_b[...] = jnp.zeros_like(oatt_b)
        gated_b[...] = jnp.zeros_like(gated_b)

    def step(oatt_w, gated_w, oatt_r, gated_r):
        y_att = jnp.dot(oatt_r[...], wa_ref[...], preferred_element_type=F32)
        y_sg = jnp.dot(gated_r[...], ws_ref[...], preferred_element_type=F32)
        merged = ga_ref[...].astype(F32) * y_att + gs_ref[...].astype(F32) * y_sg
        out_ref[...] = merged.astype(out_ref.dtype)
        _mix_prepare((o0_ref, o1_ref, o2_ref), (s0_ref[...], s1_ref[...], s2_ref[...]), u_ref, vs_ref,
                     lng_ref, lnb_ref, sgw_ref, sgb_ref, oatt_w, gated_w)

    @pl.when(i % 2 == 0)
    def _():
        step(oatt_a, gated_a, oatt_b, gated_b)

    @pl.when(i % 2 == 1)
    def _():
        step(oatt_b, gated_b, oatt_a, gated_a)


def _mix(o_list, st_list, uv, gates, ln_g, ln_b, sg_w, sg_b_t, w_att, w_sg, *, tm=512):
    t = uv.shape[0]
    n = t // tm
    cur = lambda i: jnp.minimum(i, n - 1)
    lag = lambda i: jnp.maximum(i - 1, 0)
    prep_row = lambda width, blk: pl.BlockSpec((tm, width), lambda i: (cur(i), blk))
    lag_row = lambda width, blk: pl.BlockSpec((tm, width), lambda i: (lag(i), blk))
    const2 = lambda shape: pl.BlockSpec(shape, lambda i: (0, 0), pipeline_mode=pl.Buffered(1))
    in_specs = (
        [pl.BlockSpec((HEADS_PER_GROUP, tm, HEAD_DIM), lambda i: (0, cur(i), 0))] * 3
        + [prep_row(V7X_LANES, 0)] * 3
        + [prep_row(SG_WIDTH, 0), prep_row(SG_WIDTH, 1), lag_row(D_MODEL, 0), lag_row(D_MODEL, 1)]
        + [const2((1, SG_WIDTH)), const2((1, SG_WIDTH)),
           pl.BlockSpec(sg_w.shape, lambda i: (0, 0, 0), pipeline_mode=pl.Buffered(1)),
           const2(sg_b_t.shape), const2(w_att.shape), const2(w_sg.shape)]
    )
    att_w = HEADS_PER_GROUP * HEAD_DIM
    return pl.pallas_call(
        _mix_kernel,
        grid=(n + 1,),
        in_specs=in_specs,
        out_specs=pl.BlockSpec((tm, D_MODEL), lambda i: (lag(i), 0)),
        out_shape=jax.ShapeDtypeStruct((t, D_MODEL), BF16),
        scratch_shapes=[pltpu.VMEM((tm, att_w), BF16), pltpu.VMEM((tm, SG_WIDTH), BF16),
                        pltpu.VMEM((tm, att_w), BF16), pltpu.VMEM((tm, SG_WIDTH), BF16)],
        compiler_params=_params(("arbitrary",)),
        name="mix",
    )(*o_list, *st_list, uv, uv, gates, gates, ln_g, ln_b, sg_w, sg_b_t, w_att, w_sg)


def _out_proj_kernel(m_ref, w_ref, x_ref, o_ref):
    o_ref[...] = x_ref[...] + jnp.dot(m_ref[...], w_ref[...], preferred_element_type=F32)


def _out_proj(merged, w, x, *, tm=1024):
    t, d = merged.shape
    n = w.shape[1]
    return pl.pallas_call(
        _out_proj_kernel,
        grid=(t // tm,),
        in_specs=[
            pl.BlockSpec((tm, d), lambda i: (i, 0)),
            pl.BlockSpec((d, n), lambda i: (0, 0), pipeline_mode=pl.Buffered(1)),
            pl.BlockSpec((tm, n), lambda i: (i, 0)),
        ],
        out_specs=pl.BlockSpec((tm, n), lambda i: (i, 0)),
        out_shape=jax.ShapeDtypeStruct((t, n), F32),
        compiler_params=_params(("parallel",)),
        name="out_proj",
    )(merged, w, x)


def kernel(x, ffn1_norm, ffn1_w_gate, ffn1_w_up, ffn1_w_down, mix_norm, w_in, sg_ln_g, sg_ln_b, sg_w, sg_b, w_att_out, w_sg_out, w_out, ffn2_norm, ffn2_w_gate, ffn2_w_up, ffn2_w_down, final_norm):
    batch, seq, d = x.shape
    assert (seq, d) == (SEQ, D_MODEL)
    t = batch * seq
    depth = ffn1_norm.shape[0]
    xt = x.reshape(t, d)
    tables = _rope_tables()
    bf = lambda a: a.astype(BF16)
    for i in range(depth):
        last = i == depth - 1
        c_v, c_u, c_g = 2 * ATT_WIDTH, 3 * ATT_WIDTH, 3 * ATT_WIDTH + 2 * SG_WIDTH
        whole = lambda w: (w, (0, w.shape[1]), 0)
        x1, hn, w_in_b, w_gates_b = _ffn(
            xt, ffn1_norm[i][None], bf(ffn1_w_gate[i]), bf(ffn1_w_up[i]), bf(ffn1_w_down[i]),
            mix_norm[i][None], post="hn", casts=((w_in[i], (0, c_g, w_in.shape[2]), c_v),))
        qk, w_out_b, w_sg_b, w_att_b = _proj(
            hn, w_in_b, 0, c_v, epilogue="rope", tn=ATT_WIDTH, tables=tables,
            casts=(whole(w_out[i]), whole(w_sg_out[i]), whole(w_att_out[i])))
        v = _proj(hn, w_in_b, c_v, ATT_WIDTH, epilogue="heads", tn=ATT_WIDTH)
        uv, w2_gate, w2_up = _proj(hn, w_in_b, c_u, 2 * SG_WIDTH, epilogue="gelu", tn=SG_WIDTH,
                                   casts=(whole(ffn2_w_gate[i]), whole(ffn2_w_up[i])))
        gates, w2_down = _proj(hn, w_gates_b, 0, 2 * D_MODEL, epilogue="sigmoid", tn=D_MODEL,
                               casts=(whole(ffn2_w_down[i]),))
        o_list, st_list = [], []
        for g in range(N_GROUPS):
            o_g, st_g = _attention(qk, v, g, batch)
            o_list.append(o_g)
            st_list.append(st_g)
        merged = _mix(o_list, st_list, uv, gates, sg_ln_g[i][None], sg_ln_b[i][None], sg_w[i], sg_b[i].T,
                      w_att_b, w_sg_b)
        x2 = _out_proj(merged, w_out_b, x1)
        (xt,) = _ffn(x2, ffn2_norm[i][None], w2_gate, w2_up, w2_down,
                     final_norm[None], post="final" if last else "plain")
    return xt.reshape(batch, seq, d)
```

```python
import functools

import jax
import jax.numpy as jnp
from jax import lax
from jax.experimental import pallas as pl
from jax.experimental.pallas import tpu as pltpu

F32 = jnp.float32
BF16 = jnp.bfloat16

D_MODEL = 2048
SEQ = 4096
HEAD_DIM = 128
HEADS_PER_GROUP = 4
DILATIONS = (1, 4, 16)
N_GROUPS = len(DILATIONS)
N_ATT_HEADS = N_GROUPS * HEADS_PER_GROUP
ATT_WIDTH = N_ATT_HEADS * HEAD_DIM
ATT_BLOCK = 128
QKV_TILE = 1024
ROPE_DIM = HEAD_DIM // 4
ROPE_HALF = ROPE_DIM // 2
ROPE_THETA = 500000.0
SG_CHUNK = 128
SG_GROUPS = 12
SG_GROUP_DIM = 128
SG_WIDTH = SG_GROUPS * SG_GROUP_DIM
NORM_EPS = 1e-6
LN_EPS = 1e-5

V7X_VMEM_BYTES = 64 * 1024 * 1024
V7X_LANES = 128
BF16_ROWS = 16
CAST_STEPS = 128
VMEM_LIMIT = 52 * 1024 * 1024


def _params(semantics, vmem_limit=VMEM_LIMIT):
    return pltpu.CompilerParams(dimension_semantics=semantics, vmem_limit_bytes=vmem_limit)


def _rms(x, g):
    return x * lax.rsqrt(jnp.mean(x * x, axis=-1, keepdims=True) + NORM_EPS) * g


ROPE_PAIR_LANE = V7X_LANES // 2


def _rope_lane_masks(lane):
    first = lane < ROPE_HALF
    second = (lane >= ROPE_PAIR_LANE) & (lane < ROPE_PAIR_LANE + ROPE_HALF)
    return first, second


def _rope_lane_layout(x):
    cols = x.shape[1]
    lane = lax.broadcasted_iota(jnp.int32, x.shape, 1) % V7X_LANES
    shift = ROPE_PAIR_LANE - ROPE_HALF
    to_pair_lane = (lane >= ROPE_PAIR_LANE) & (lane < ROPE_PAIR_LANE + ROPE_HALF)
    from_pair_lane = (lane >= ROPE_HALF) & (lane < ROPE_DIM)
    return jnp.where(to_pair_lane, pltpu.roll(x, shift, 1),
                     jnp.where(from_pair_lane, pltpu.roll(x, cols - shift, 1), x))


def _rope_table_kernel(freq_ref, cos_ref, sin_ref):
    rows = cos_ref.shape[0]
    pos = (pl.program_id(0) * rows + lax.broadcasted_iota(jnp.int32, (rows, V7X_LANES), 0)).astype(F32)
    first, second = _rope_lane_masks(lax.broadcasted_iota(jnp.int32, (rows, V7X_LANES), 1))
    ang = pos * freq_ref[...]
    c = jnp.cos(ang)
    s = jnp.sin(ang)
    cos_ref[...] = jnp.where(first | second, c, 1.0)
    sin_ref[...] = jnp.where(first, -s, jnp.where(second, s, 0.0))


def _rope_tables():
    inv_freq = ROPE_THETA ** (-jnp.arange(0, ROPE_DIM, 2, dtype=F32) / ROPE_DIM)
    gap = jnp.zeros((ROPE_PAIR_LANE - ROPE_HALF,), F32)
    freq_lane = jnp.concatenate([inv_freq, gap, inv_freq, gap])[None, :]
    rows = 512
    tab = jax.ShapeDtypeStruct((SEQ, V7X_LANES), F32)
    spec = pl.BlockSpec((rows, V7X_LANES), lambda i: (i, 0))
    return pl.pallas_call(
        _rope_table_kernel,
        grid=(SEQ // rows,),
        in_specs=[pl.BlockSpec((1, V7X_LANES), lambda i: (0, 0))],
        out_specs=[spec, spec],
        out_shape=[tab, tab],
        compiler_params=_params(("parallel",)),
        name="rope_table",
    )(freq_lane)


def _cast_specs(casts, n_steps, step_of):
    in_specs, out_specs, out_shapes = [], [], []
    for w, splits, _ in casts:
        rows, cols = w.shape
        assert splits[0] == 0 and splits[-1] == cols
        rb = next(r for r in range(BF16_ROWS, rows + 1, BF16_ROWS) if rows % r == 0 and rows // r <= n_steps)
        n_blocks = rows // rb
        block_of = lambda *idx, n_blocks=n_blocks: (jnp.minimum(step_of(*idx), n_blocks - 1), 0)
        in_specs.append(pl.BlockSpec((rb, cols), block_of))
        for c0, c1 in zip(splits[:-1], splits[1:]):
            out_specs.append(pl.BlockSpec((rb, c1 - c0), block_of))
            out_shapes.append(jax.ShapeDtypeStruct((rows, c1 - c0), BF16))
    return in_specs, out_specs, out_shapes


def _cast_blocks(cast_in, cast_out, cast_meta):
    cast_out = iter(cast_out)
    for src, (splits, rope_cols) in zip(cast_in, cast_meta):
        for c0, c1 in zip(splits[:-1], splits[1:]):
            dst = next(cast_out)
            n_rope = min(max(rope_cols - c0, 0), c1 - c0)
            if n_rope:
                dst[:, :n_rope] = _rope_lane_layout(src[:, c0:c0 + n_rope]).astype(BF16)
            if n_rope < c1 - c0:
                dst[:, n_rope:] = src[:, c0 + n_rope:c1].astype(BF16)


def _cast_meta(casts):
    return tuple((splits, rope_cols) for _, splits, rope_cols in casts)


def _ffn_kernel(x_ref, g_ref, wg_ref, wu_ref, wd_ref, pg_ref, *refs, post, cast_meta):
    n_cast = len(cast_meta)
    cast_in, refs = refs[:n_cast], refs[n_cast:]
    if post == "hn":
        o_ref, hn_ref, *cast_out, xn_ref = refs
    else:
        o_ref, *cast_out, xn_ref = refs
    j = pl.program_id(1)

    @pl.when(j == 0)
    def _():
        xn_ref[...] = _rms(x_ref[...], g_ref[...]).astype(BF16)
        o_ref[...] = jnp.zeros_like(o_ref)

    _cast_blocks(cast_in, cast_out, cast_meta)

    xn = xn_ref[...]
    h = jnp.dot(xn, wg_ref[...], preferred_element_type=F32)
    u = jnp.dot(xn, wu_ref[...], preferred_element_type=F32)
    a = (jax.nn.silu(h) * u).astype(BF16)
    o_ref[...] += jnp.dot(a, wd_ref[...], preferred_element_type=F32)

    @pl.when(j == pl.num_programs(1) - 1)
    def _():
        y = x_ref[...] + 0.5 * o_ref[...]
        if post == "hn":
            o_ref[...] = y
            hn_ref[...] = _rms(y, pg_ref[...]).astype(BF16)
        elif post == "final":
            o_ref[...] = _rms(y, pg_ref[...])
        else:
            o_ref[...] = y


def _ffn(x, g, wg, wu, wd, post_g, *, post, casts=(), tm=512, tf=512):
    t, d = x.shape
    f = wg.shape[1]
    n_j = f // tf
    row = pl.BlockSpec((tm, d), lambda i, j: (i, 0))
    vec = pl.BlockSpec((1, d), lambda i, j: (0, 0))
    out_shape = [jax.ShapeDtypeStruct((t, d), F32)]
    out_specs = [row]
    if post == "hn":
        out_shape.append(jax.ShapeDtypeStruct((t, d), BF16))
        out_specs.append(row)
    cast_specs, cast_out_specs, cast_out_shapes = _cast_specs(
        casts, min(CAST_STEPS, (t // tm) * n_j), lambda i, j: i * n_j + j)
    out_specs += cast_out_specs
    out_shape += cast_out_shapes
    return pl.pallas_call(
        functools.partial(_ffn_kernel, post=post, cast_meta=_cast_meta(casts)),
        grid=(t // tm, n_j),
        in_specs=[
            row,
            vec,
            pl.BlockSpec((d, tf), lambda i, j: (0, j)),
            pl.BlockSpec((d, tf), lambda i, j: (0, j)),
            pl.BlockSpec((tf, d), lambda i, j: (j, 0)),
            vec,
        ] + cast_specs,
        out_specs=out_specs,
        out_shape=out_shape,
        scratch_shapes=[pltpu.VMEM((tm, d), BF16)],
        compiler_params=_params(("arbitrary", "arbitrary")),
        name="ffn_" + post,
    )(x, g, wg, wu, wd, post_g, *[c[0] for c in casts])


def _store_head(o_ref, h, val):
    dil = DILATIONS[h // HEADS_PER_GROUP]
    if dil == 1:
        o_ref[h] = val.astype(BF16)
        return
    regrouped = pltpu.einshape("lrd->rld", val.reshape(QKV_TILE // dil, dil, HEAD_DIM))
    o_ref[h] = regrouped.reshape(QKV_TILE, HEAD_DIM).astype(BF16)


def _proj_kernel(h_ref, w_ref, *refs, epilogue, cast_meta):
    n_cast = len(cast_meta)
    n_cast_out = sum(len(splits) - 1 for splits, _ in cast_meta)
    n_tables = 2 if epilogue == "rope" else 0
    cast_in = refs[n_tables:n_tables + n_cast]
    cast_out = refs[n_tables + n_cast + 1:n_tables + n_cast + 1 + n_cast_out]
    refs = refs[:n_tables] + refs[n_tables + n_cast:n_tables + n_cast + 1] + refs[n_tables + n_cast + 1 + n_cast_out:]
    _cast_blocks(cast_in, cast_out, cast_meta)

    acc = jnp.dot(h_ref[...], w_ref[...], preferred_element_type=F32)
    n_heads = acc.shape[1] // HEAD_DIM
    if epilogue == "rope":
        cos_ref, sin_ref, o_ref = refs
        scale = jnp.where(pl.program_id(1) == 0, HEAD_DIM ** -0.5, 1.0).astype(F32)
        cos = cos_ref[...] * scale
        sin = sin_ref[...] * scale
        for h in range(n_heads):
            t = acc[:, h * HEAD_DIM:(h + 1) * HEAD_DIM]
            _store_head(o_ref, h, t * cos + pltpu.roll(t, ROPE_PAIR_LANE, 1) * sin)
        return
    if epilogue == "heads":
        (o_ref,) = refs
        for h in range(n_heads):
            _store_head(o_ref, h, acc[:, h * HEAD_DIM:(h + 1) * HEAD_DIM])
        return

    (o_ref,) = refs
    if epilogue == "gelu":
        acc = 0.5 * acc * (1.0 + lax.erf(acc * (0.5 ** 0.5)))
    elif epilogue == "sigmoid":
        acc = jax.nn.sigmoid(acc)
    o_ref[...] = acc.astype(o_ref.dtype)


def _proj(hn, w, col0, ncols, *, epilogue, tn, tables=(), casts=(), tm=QKV_TILE):
    t, d = hn.shape
    assert col0 % tn == 0 and ncols % tn == 0
    c0 = col0 // tn
    n_j = ncols // tn
    in_specs = [
        pl.BlockSpec((tm, d), lambda i, j: (i, 0)),
        pl.BlockSpec((d, tn), lambda i, j: (0, c0 + j)),
    ]
    seq_blocks = SEQ // tm
    for _ in tables:
        in_specs.append(pl.BlockSpec((tm, V7X_LANES), lambda i, j: (i % seq_blocks, 0)))
    if epilogue in ("rope", "heads"):
        assert tm == QKV_TILE and tn == ATT_WIDTH
        out_spec = pl.BlockSpec((N_ATT_HEADS, tm, HEAD_DIM), lambda i, j: (j, i, 0))
        out_shape = jax.ShapeDtypeStruct((ncols // HEAD_DIM, t, HEAD_DIM), BF16)
    else:
        out_spec = pl.BlockSpec((tm, tn), lambda i, j: (i, j))
        out_shape = jax.ShapeDtypeStruct((t, ncols), BF16)
    cast_specs, cast_out_specs, cast_out_shapes = _cast_specs(casts, (t // tm) * n_j, lambda i, j: i * n_j + j)
    res = pl.pallas_call(
        functools.partial(_proj_kernel, epilogue=epilogue, cast_meta=_cast_meta(casts)),
        grid=(t // tm, n_j),
        in_specs=in_specs + cast_specs,
        out_specs=[out_spec] + cast_out_specs,
        out_shape=[out_shape] + cast_out_shapes,
        compiler_params=_params(("arbitrary", "arbitrary")),
        name="proj_" + epilogue,
    )(hn, w, *tables, *[c[0] for c in casts])
    return res if casts else res[0]


def _attn_kernel(q_ref, k_ref, v_ref, o_ref, st_ref, *, dil):
    head = pl.program_id(1)
    n_qb = SEQ // (ATT_BLOCK * dil)

    qi = lax.broadcasted_iota(jnp.int32, (ATT_BLOCK, 2 * ATT_BLOCK), 0)
    kj = lax.broadcasted_iota(jnp.int32, (ATT_BLOCK, 2 * ATT_BLOCK), 1)
    diff = qi + ATT_BLOCK - kj
    band = (diff >= 0) & (diff <= ATT_BLOCK)
    band_first = band & (kj >= ATT_BLOCK)
    lane = lax.broadcasted_iota(jnp.int32, (ATT_BLOCK, V7X_LANES), 1)
    ones = jnp.ones((2 * ATT_BLOCK, HEAD_DIM), BF16)
    zeros = jnp.zeros((ATT_BLOCK, HEAD_DIM), BF16)

    @pl.when(head == 0)
    def _():
        st_ref[...] = jnp.zeros_like(st_ref)

    rows_per = QKV_TILE // dil
    run = min(ATT_BLOCK, rows_per)

    def load(ref, qb, r):
        parts = []
        for idx in range(qb * ATT_BLOCK, (qb + 1) * ATT_BLOCK, run):
            tile, local = divmod(idx, rows_per)
            parts.append(ref[pl.ds(tile * QKV_TILE + r * rows_per + local, run), :])
        return parts[0] if len(parts) == 1 else jnp.concatenate(parts, axis=0)

    interleave = dil % 8 == 0

    def put(ref, qb, blocks, merge=None):
        if interleave:
            rows = pl.ds(qb * ATT_BLOCK * dil, ATT_BLOCK * dil)
            val = pltpu.einshape("rld->lrd", jnp.stack(blocks)).reshape(ATT_BLOCK * dil, V7X_LANES)
            ref[rows, :] = val if merge is None else merge(val, ref[rows, :])
            return
        for r, val in enumerate(blocks):
            rows = (pl.ds(qb * ATT_BLOCK * dil + r, ATT_BLOCK, stride=dil) if dil > 1
                    else pl.ds(qb * ATT_BLOCK, ATT_BLOCK))
            ref[rows, :] = val if merge is None else merge(val, ref[rows, :])

    def keep_other_heads(new, old):
        own = lax.broadcasted_iota(jnp.int32, new.shape, 1)
        return jnp.where((own == head) | (own == HEADS_PER_GROUP + head), new, old)

    k_prev, v_prev = [zeros] * dil, [zeros] * dil
    for qb in range(n_qb):
        outs, stats = [], []
        for r in range(dil):
            q, k_cur, v_cur = load(q_ref, qb, r), load(k_ref, qb, r), load(v_ref, qb, r)
            k_win = jnp.concatenate([k_prev[r], k_cur], axis=0)
            v_win = jnp.concatenate([v_prev[r], v_cur], axis=0)
            s = lax.dot_general(q, k_win, (((1,), (1,)), ((), ())), preferred_element_type=F32)
            s = jnp.where(band_first if qb == 0 else band, s, -jnp.inf)
            m = jnp.max(s, axis=-1, keepdims=True)
            p = jnp.exp(s - m).astype(BF16)
            pv = jnp.dot(p, jnp.concatenate([v_win, ones], axis=1), preferred_element_type=F32)
            l_b = pv[:, HEAD_DIM:]
            outs.append(pv[:, :HEAD_DIM] / l_b)
            stats.append(jnp.where(lane == head, m, l_b))
            k_prev[r], v_prev[r] = k_cur, v_cur
        put(o_ref, qb, outs)
        put(st_ref, qb, stats, merge=keep_other_heads)


def _attention(qk, v, group, batch):
    dil = DILATIONS[group]
    t = qk.shape[1]
    h0 = group * HEADS_PER_GROUP

    def seq(slab0):
        return pl.BlockSpec((None, SEQ, HEAD_DIM), lambda b, h: (slab0 + h, b, 0))

    return pl.pallas_call(
        functools.partial(_attn_kernel, dil=dil),
        grid=(batch, HEADS_PER_GROUP),
        in_specs=[seq(h0), seq(N_ATT_HEADS + h0), seq(h0)],
        out_specs=[
            pl.BlockSpec((None, SEQ, HEAD_DIM), lambda b, h: (h, b, 0)),
            pl.BlockSpec((SEQ, V7X_LANES), lambda b, h: (b, 0)),
        ],
        out_shape=[
            jax.ShapeDtypeStruct((HEADS_PER_GROUP, t, HEAD_DIM), F32),
            jax.ShapeDtypeStruct((t, V7X_LANES), F32),
        ],
        compiler_params=_params(("parallel", "arbitrary")),
        name=f"attn_d{dil}",
    )(qk, qk, v)


def _mix_prepare(o_refs, stats, u_ref, vs_ref, lng_ref, lnb_ref, sgw_ref, sgb_ref, oatt_ref, gated_ref):
    tm = oatt_ref.shape[0]
    for h in range(HEADS_PER_GROUP):
        ms = [s[:, h:h + 1] for s in stats]
        ls = [s[:, HEADS_PER_GROUP + h:HEADS_PER_GROUP + h + 1] for s in stats]
        m_all = jnp.maximum(jnp.maximum(ms[0], ms[1]), ms[2])
        ws = [l * jnp.exp(m - m_all) for l, m in zip(ls, ms)]
        num = ws[0] * o_refs[0][h] + ws[1] * o_refs[1][h] + ws[2] * o_refs[2][h]
        oatt_ref[:, h * HEAD_DIM:(h + 1) * HEAD_DIM] = (num / (ws[0] + ws[1] + ws[2])).astype(BF16)

    vs = vs_ref[...].astype(F32)
    mu = jnp.mean(vs, axis=-1, keepdims=True)
    var = jnp.mean(jnp.square(vs - mu), axis=-1, keepdims=True)
    vn = ((vs - mu) * lax.rsqrt(var + LN_EPS) * lng_ref[...] + lnb_ref[...]).astype(BF16)
    ti = lax.broadcasted_iota(jnp.int32, (SG_CHUNK, SG_CHUNK), 0)
    si = lax.broadcasted_iota(jnp.int32, (SG_CHUNK, SG_CHUNK), 1)
    causal = si <= ti
    for g in range(SG_GROUPS):
        cols = slice(g * SG_GROUP_DIM, (g + 1) * SG_GROUP_DIM)
        w_sp = jnp.where(causal, sgw_ref[g], 0.0).astype(BF16)
        bias = sgb_ref[:, g:g + 1]
        for c in range(tm // SG_CHUNK):
            rows = slice(c * SG_CHUNK, (c + 1) * SG_CHUNK)
            spatial = jnp.dot(w_sp, vn[rows, cols], preferred_element_type=F32) + bias
            gated_ref[rows, cols] = (u_ref[rows, cols].astype(F32) * spatial).astype(BF16)


def _mix_kernel(o0_ref, o1_ref, o2_ref, s0_ref, s1_ref, s2_ref, u_ref, vs_ref, ga_ref, gs_ref,
                lng_ref, lnb_ref, sgw_ref, sgb_ref, wa_ref, ws_ref, out_ref,
                oatt_a, gated_a, oatt_b, gated_b):
    i = pl.program_id(0)

    @pl.when(i == 0)
    def _():
        oatt_b[...] = jnp.zeros_like(oatt_b)
        gated_b[...] = jnp.zeros_like(gated_b)

    def step(oatt_w, gated_w, oatt_r, gated_r):
        y_att = jnp.dot(oatt_r[...], wa_ref[...], preferred_element_type=F32)
        y_sg = jnp.dot(gated_r[...], ws_ref[...], preferred_element_type=F32)
        merged = ga_ref[...].astype(F32) * y_att + gs_ref[...].astype(F32) * y_sg
        out_ref[...] = merged.astype(out_ref.dtype)
        _mix_prepare((o0_ref, o1_ref, o2_ref), (s0_ref[...], s1_ref[...], s2_ref[...]), u_ref, vs_ref,
                     lng_ref, lnb_ref, sgw_ref, sgb_ref, oatt_w, gated_w)

    @pl.when(i % 2 == 0)
    def _():
        step(oatt_a, gated_a, oatt_b, gated_b)

    @pl.when(i % 2 == 1)
    def _():
        step(oatt_b, gated_b, oatt_a, gated_a)


def _mix(o_list, st_list, uv, gates, ln_g, ln_b, sg_w, sg_b_t, w_att, w_sg, *, tm=512):
    t = uv.shape[0]
    n = t // tm
    cur = lambda i: jnp.minimum(i, n - 1)
    lag = lambda i: jnp.maximum(i - 1, 0)
    prep_row = lambda width, blk: pl.BlockSpec((tm, width), lambda i: (cur(i), blk))
    lag_row = lambda width, blk: pl.BlockSpec((tm, width), lambda i: (lag(i), blk))
    const2 = lambda shape: pl.BlockSpec(shape, lambda i: (0, 0), pipeline_mode=pl.Buffered(1))
    in_specs = (
        [pl.BlockSpec((HEADS_PER_GROUP, tm, HEAD_DIM), lambda i: (0, cur(i), 0))] * 3
        + [prep_row(V7X_LANES, 0)] * 3
        + [prep_row(SG_WIDTH, 0), prep_row(SG_WIDTH, 1), lag_row(D_MODEL, 0), lag_row(D_MODEL, 1)]
        + [const2((1, SG_WIDTH)), const2((1, SG_WIDTH)),
           pl.BlockSpec(sg_w.shape, lambda i: (0, 0, 0), pipeline_mode=pl.Buffered(1)),
           const2(sg_b_t.shape), const2(w_att.shape), const2(w_sg.shape)]
    )
    att_w = HEADS_PER_GROUP * HEAD_DIM
    return pl.pallas_call(
        _mix_kernel,
        grid=(n + 1,),
        in_specs=in_specs,
        out_specs=pl.BlockSpec((tm, D_MODEL), lambda i: (lag(i), 0)),
        out_shape=jax.ShapeDtypeStruct((t, D_MODEL), BF16),
        scratch_shapes=[pltpu.VMEM((tm, att_w), BF16), pltpu.VMEM((tm, SG_WIDTH), BF16),
                        pltpu.VMEM((tm, att_w), BF16), pltpu.VMEM((tm, SG_WIDTH), BF16)],
        compiler_params=_params(("arbitrary",)),
        name="mix",
    )(*o_list, *st_list, uv, uv, gates, gates, ln_g, ln_b, sg_w, sg_b_t, w_att, w_sg)


def _out_proj_kernel(m_ref, w_ref, x_ref, o_ref):
    o_ref[...] = x_ref[...] + jnp.dot(m_ref[...], w_ref[...], preferred_element_type=F32)


def _out_proj(merged, w, x, *, tm=1024):
    t, d = merged.shape
    n = w.shape[1]
    return pl.pallas_call(
        _out_proj_kernel,
        grid=(t // tm,),
        in_specs=[
            pl.BlockSpec((tm, d), lambda i: (i, 0)),
            pl.BlockSpec((d, n), lambda i: (0, 0), pipeline_mode=pl.Buffered(1)),
            pl.BlockSpec((tm, n), lambda i: (i, 0)),
        ],
        out_specs=pl.BlockSpec((tm, n), lambda i: (i, 0)),
        out_shape=jax.ShapeDtypeStruct((t, n), F32),
        compiler_params=_params(("parallel",)),
        name="out_proj",
    )(merged, w, x)


def kernel(x, ffn1_norm, ffn1_w_gate, ffn1_w_up, ffn1_w_down, mix_norm, w_in, sg_ln_g, sg_ln_b, sg_w, sg_b, w_att_out, w_sg_out, w_out, ffn2_norm, ffn2_w_gate, ffn2_w_up, ffn2_w_down, final_norm):
    batch, seq, d = x.shape
    assert (seq, d) == (SEQ, D_MODEL)
    t = batch * seq
    depth = ffn1_norm.shape[0]
    xt = x.reshape(t, d)
    tables = _rope_tables()
    bf = lambda a: a.astype(BF16)
    for i in range(depth):
        last = i == depth - 1
        c_v, c_u, c_g = 2 * ATT_WIDTH, 3 * ATT_WIDTH, 3 * ATT_WIDTH + 2 * SG_WIDTH
        whole = lambda w: (w, (0, w.shape[1]), 0)
        x1, hn, w_in_b, w_gates_b = _ffn(
            xt, ffn1_norm[i][None], bf(ffn1_w_gate[i]), bf(ffn1_w_up[i]), bf(ffn1_w_down[i]),
            mix_norm[i][None], post="hn", casts=((w_in[i], (0, c_g, w_in.shape[2]), c_v),))
        qk, w_out_b, w_sg_b, w_att_b = _proj(
            hn, w_in_b, 0, c_v, epilogue="rope", tn=ATT_WIDTH, tables=tables,
            casts=(whole(w_out[i]), whole(w_sg_out[i]), whole(w_att_out[i])))
        v = _proj(hn, w_in_b, c_v, ATT_WIDTH, epilogue="heads", tn=ATT_WIDTH)
        uv, w2_gate, w2_up = _proj(hn, w_in_b, c_u, 2 * SG_WIDTH, epilogue="gelu", tn=SG_WIDTH,
                                   casts=(whole(ffn2_w_gate[i]), whole(ffn2_w_up[i])))
        gates, w2_down = _proj(hn, w_gates_b, 0, 2 * D_MODEL, epilogue="sigmoid", tn=D_MODEL,
                               casts=(whole(ffn2_w_down[i]),))
        o_list, st_list = [], []
        for g in range(N_GROUPS):
            o_g, st_g = _attention(qk, v, g, batch)
            o_list.append(o_g)
            st_list.append(st_g)
        merged = _mix(o_list, st_list, uv, gates, sg_ln_g[i][None], sg_ln_b[i][None], sg_w[i], sg_b[i].T,
                      w_att_b, w_sg_b)
        x2 = _out_proj(merged, w_out_b, x1)
        (xt,) = _ffn(x2, ffn2_norm[i][None], w2_gate, w2_up, w2_down,
                     final_norm[None], post="final" if last else "plain")
    return xt.reshape(batch, seq, d)
```

```python
import functools

import jax
import jax.numpy as jnp
from jax import lax
from jax.experimental import pallas as pl
from jax.experimental.pallas import tpu as pltpu

F32 = jnp.float32
BF16 = jnp.bfloat16

D_MODEL = 2048
SEQ = 4096
HEAD_DIM = 128
HEADS_PER_GROUP = 4
DILATIONS = (1, 4, 16)
N_GROUPS = len(DILATIONS)
N_ATT_HEADS = N_GROUPS * HEADS_PER_GROUP
ATT_WIDTH = N_ATT_HEADS * HEAD_DIM
ATT_BLOCK = 128
QKV_TILE = 1024
ROPE_DIM = HEAD_DIM // 4
ROPE_HALF = ROPE_DIM // 2
ROPE_THETA = 500000.0
SG_CHUNK = 128
SG_GROUPS = 12
SG_GROUP_DIM = 128
SG_WIDTH = SG_GROUPS * SG_GROUP_DIM
NORM_EPS = 1e-6
LN_EPS = 1e-5

V7X_VMEM_BYTES = 64 * 1024 * 1024
V7X_LANES = 128
BF16_ROWS = 16
CAST_STEPS = 128
VMEM_LIMIT = 52 * 1024 * 1024


def _params(semantics, vmem_limit=VMEM_LIMIT):
    return pltpu.CompilerParams(dimension_semantics=semantics, vmem_limit_bytes=vmem_limit)


def _rms(x, g):
    return x * lax.rsqrt(jnp.mean(x * x, axis=-1, keepdims=True) + NORM_EPS) * g


ROPE_PAIR_LANE = V7X_LANES // 2


def _rope_lane_masks(lane):
    first = lane < ROPE_HALF
    second = (lane >= ROPE_PAIR_LANE) & (lane < ROPE_PAIR_LANE + ROPE_HALF)
    return first, second


def _rope_lane_layout(x):
    cols = x.shape[1]
    lane = lax.broadcasted_iota(jnp.int32, x.shape, 1) % V7X_LANES
    shift = ROPE_PAIR_LANE - ROPE_HALF
    to_pair_lane = (lane >= ROPE_PAIR_LANE) & (lane < ROPE_PAIR_LANE + ROPE_HALF)
    from_pair_lane = (lane >= ROPE_HALF) & (lane < ROPE_DIM)
    return jnp.where(to_pair_lane, pltpu.roll(x, shift, 1),
                     jnp.where(from_pair_lane, pltpu.roll(x, cols - shift, 1), x))


def _rope_table_kernel(freq_ref, cos_ref, sin_ref):
    rows = cos_ref.shape[0]
    pos = (pl.program_id(0) * rows + lax.broadcasted_iota(jnp.int32, (rows, V7X_LANES), 0)).astype(F32)
    first, second = _rope_lane_masks(lax.broadcasted_iota(jnp.int32, (rows, V7X_LANES), 1))
    ang = pos * freq_ref[...]
    c = jnp.cos(ang)
    s = jnp.sin(ang)
    cos_ref[...] = jnp.where(first | second, c, 1.0)
    sin_ref[...] = jnp.where(first, -s, jnp.where(second, s, 0.0))


def _rope_tables():
    inv_freq = ROPE_THETA ** (-jnp.arange(0, ROPE_DIM, 2, dtype=F32) / ROPE_DIM)
    gap = jnp.zeros((ROPE_PAIR_LANE - ROPE_HALF,), F32)
    freq_lane = jnp.concatenate([inv_freq, gap, inv_freq, gap])[None, :]
    rows = 512
    tab = jax.ShapeDtypeStruct((SEQ, V7X_LANES), F32)
    spec = pl.BlockSpec((rows, V7X_LANES), lambda i: (i, 0))
    return pl.pallas_call(
        _rope_table_kernel,
        grid=(SEQ // rows,),
        in_specs=[pl.BlockSpec((1, V7X_LANES), lambda i: (0, 0))],
        out_specs=[spec, spec],
        out_shape=[tab, tab],
        compiler_params=_params(("parallel",)),
        name="rope_table",
    )(freq_lane)


def _cast_specs(casts, n_steps, step_of):
    in_specs, out_specs, out_shapes = [], [], []
    for w, splits, _ in casts:
        rows, cols = w.shape
        assert splits[0] == 0 and splits[-1] == cols
        rb = next(r for r in range(BF16_ROWS, rows + 1, BF16_ROWS) if rows % r == 0 and rows // r <= n_steps)
        n_blocks = rows // rb
        block_of = lambda *idx, n_blocks=n_blocks: (jnp.minimum(step_of(*idx), n_blocks - 1), 0)
        in_specs.append(pl.BlockSpec((rb, cols), block_of))
        for c0, c1 in zip(splits[:-1], splits[1:]):
            out_specs.append(pl.BlockSpec((rb, c1 - c0), block_of))
            out_shapes.append(jax.ShapeDtypeStruct((rows, c1 - c0), BF16))
    return in_specs, out_specs, out_shapes


def _cast_blocks(cast_in, cast_out, cast_meta):
    cast_out = iter(cast_out)
    for src, (splits, rope_cols) in zip(cast_in, cast_meta):
        for c0, c1 in zip(splits[:-1], splits[1:]):
            dst = next(cast_out)
            n_rope = min(max(rope_cols - c0, 0), c1 - c0)
            if n_rope:
                dst[:, :n_rope] = _rope_lane_layout(src[:, c0:c0 + n_rope]).astype(BF16)
            if n_rope < c1 - c0:
                dst[:, n_rope:] = src[:, c0 + n_rope:c1].astype(BF16)


def _cast_meta(casts):
    return tuple((splits, rope_cols) for _, splits, rope_cols in casts)


def _ffn_kernel(x_ref, g_ref, wg_ref, wu_ref, wd_ref, pg_ref, *refs, post, cast_meta):
    n_cast = len(cast_meta)
    cast_in, refs = refs[:n_cast], refs[n_cast:]
    if post == "hn":
        o_ref, hn_ref, *cast_out, xn_ref = refs
    else:
        o_ref, *cast_out, xn_ref = refs
    j = pl.program_id(1)
    last = pl.num_programs(1) - 1

    def tile_contribution(xn):
        _cast_blocks(cast_in, cast_out, cast_meta)
        h = jnp.dot(xn, wg_ref[...], preferred_element_type=F32)
        u = jnp.dot(xn, wu_ref[...], preferred_element_type=F32)
        a = (jax.nn.silu(h) * u).astype(BF16)
        return jnp.dot(a, wd_ref[...], preferred_element_type=F32)

    @pl.when(j == 0)
    def _():
        xn = _rms(x_ref[...], g_ref[...]).astype(BF16)
        xn_ref[...] = xn
        o_ref[...] = tile_contribution(xn)

    @pl.when((j > 0) & (j < last))
    def _():
        o_ref[...] += tile_contribution(xn_ref[...])

    @pl.when(j == last)
    def _():
        y = x_ref[...] + 0.5 * (o_ref[...] + tile_contribution(xn_ref[...]))
        if post == "hn":
            o_ref[...] = y
            hn_ref[...] = _rms(y, pg_ref[...]).astype(BF16)
        elif post == "final":
            o_ref[...] = _rms(y, pg_ref[...])
        else:
            o_ref[...] = y


def _ffn(x, g, wg, wu, wd, post_g, *, post, casts=(), tm=512, tf=512):
    t, d = x.shape
    f = wg.shape[1]
    n_j = f // tf
    row = pl.BlockSpec((tm, d), lambda i, j: (i, 0))
    vec = pl.BlockSpec((1, d), lambda i, j: (0, 0))
    out_shape = [jax.ShapeDtypeStruct((t, d), F32)]
    out_specs = [row]
    if post == "hn":
        out_shape.append(jax.ShapeDtypeStruct((t, d), BF16))
        out_specs.append(row)
    cast_specs, cast_out_specs, cast_out_shapes = _cast_specs(
        casts, min(CAST_STEPS, (t // tm) * n_j), lambda i, j: i * n_j + j)
    out_specs += cast_out_specs
    out_shape += cast_out_shapes
    return pl.pallas_call(
        functools.partial(_ffn_kernel, post=post, cast_meta=_cast_meta(casts)),
        grid=(t // tm, n_j),
        in_specs=[
            row,
            vec,
            pl.BlockSpec((d, tf), lambda i, j: (0, j)),
            pl.BlockSpec((d, tf), lambda i, j: (0, j)),
            pl.BlockSpec((tf, d), lambda i, j: (j, 0)),
            vec,
        ] + cast_specs,
        out_specs=out_specs,
        out_shape=out_shape,
        scratch_shapes=[pltpu.VMEM((tm, d), BF16)],
        compiler_params=_params(("arbitrary", "arbitrary")),
        name="ffn_" + post,
    )(x, g, wg, wu, wd, post_g, *[c[0] for c in casts])


def _store_head(o_ref, h, val):
    dil = DILATIONS[h // HEADS_PER_GROUP]
    if dil == 1:
        o_ref[h] = val.astype(BF16)
        return
    regrouped = pltpu.einshape("lrd->rld", val.reshape(QKV_TILE // dil, dil, HEAD_DIM))
    o_ref[h] = regrouped.reshape(QKV_TILE, HEAD_DIM).astype(BF16)


def _proj_kernel(h_ref, w_ref, *refs, epilogue, cast_meta):
    n_cast = len(cast_meta)
    n_cast_out = sum(len(splits) - 1 for splits, _ in cast_meta)
    n_tables = 2 if epilogue == "rope" else 0
    cast_in = refs[n_tables:n_tables + n_cast]
    cast_out = refs[n_tables + n_cast + 1:n_tables + n_cast + 1 + n_cast_out]
    refs = refs[:n_tables] + refs[n_tables + n_cast:n_tables + n_cast + 1] + refs[n_tables + n_cast + 1 + n_cast_out:]
    _cast_blocks(cast_in, cast_out, cast_meta)

    if epilogue in ("rope", "heads"):
        if epilogue == "rope":
            cos_ref, sin_ref, o_ref = refs
            scale = jnp.where(pl.program_id(1) == 0, HEAD_DIM ** -0.5, 1.0).astype(F32)
            cos = cos_ref[...] * scale
            sin = sin_ref[...] * scale
        else:
            (o_ref,) = refs
        lhs = h_ref[...]
        for g in reversed(range(N_GROUPS)):
            cols = slice(g * HEADS_PER_GROUP * HEAD_DIM, (g + 1) * HEADS_PER_GROUP * HEAD_DIM)
            acc = jnp.dot(lhs, w_ref[:, cols], preferred_element_type=F32)
            for hh in range(HEADS_PER_GROUP):
                t = acc[:, hh * HEAD_DIM:(hh + 1) * HEAD_DIM]
                if epilogue == "rope":
                    t = t * cos + pltpu.roll(t, ROPE_PAIR_LANE, 1) * sin
                _store_head(o_ref, g * HEADS_PER_GROUP + hh, t)
        return

    (o_ref,) = refs
    acc = jnp.dot(h_ref[...], w_ref[...], preferred_element_type=F32)
    if epilogue == "gelu":
        acc = 0.5 * acc * (1.0 + lax.erf(acc * (0.5 ** 0.5)))
    elif epilogue == "sigmoid":
        acc = jax.nn.sigmoid(acc)
    o_ref[...] = acc.astype(o_ref.dtype)


def _proj(hn, w, col0, ncols, *, epilogue, tn, tables=(), casts=(), tm=QKV_TILE):
    t, d = hn.shape
    assert col0 % tn == 0 and ncols % tn == 0
    c0 = col0 // tn
    n_j = ncols // tn
    in_specs = [
        pl.BlockSpec((tm, d), lambda i, j: (i, 0)),
        pl.BlockSpec((d, tn), lambda i, j: (0, c0 + j)),
    ]
    seq_blocks = SEQ // tm
    for _ in tables:
        in_specs.append(pl.BlockSpec((tm, V7X_LANES), lambda i, j: (i % seq_blocks, 0)))
    if epilogue in ("rope", "heads"):
        assert tm == QKV_TILE and tn == ATT_WIDTH
        out_spec = pl.BlockSpec((N_ATT_HEADS, tm, HEAD_DIM), lambda i, j: (j, i, 0))
        out_shape = jax.ShapeDtypeStruct((ncols // HEAD_DIM, t, HEAD_DIM), BF16)
    else:
        out_spec = pl.BlockSpec((tm, tn), lambda i, j: (i, j))
        out_shape = jax.ShapeDtypeStruct((t, ncols), BF16)
    cast_specs, cast_out_specs, cast_out_shapes = _cast_specs(casts, (t // tm) * n_j, lambda i, j: i * n_j + j)
    res = pl.pallas_call(
        functools.partial(_proj_kernel, epilogue=epilogue, cast_meta=_cast_meta(casts)),
        grid=(t // tm, n_j),
        in_specs=in_specs + cast_specs,
        out_specs=[out_spec] + cast_out_specs,
        out_shape=[out_shape] + cast_out_shapes,
        compiler_params=_params(("arbitrary", "arbitrary")),
        name="proj_" + epilogue,
    )(hn, w, *tables, *[c[0] for c in casts])
    return res if casts else res[0]


def _attn_kernel(q_ref, k_ref, v_ref, o_ref, st_ref, *, dil):
    head = pl.program_id(1)
    n_qb = SEQ // (ATT_BLOCK * dil)

    qi = lax.broadcasted_iota(jnp.int32, (ATT_BLOCK, 2 * ATT_BLOCK), 0)
    kj = lax.broadcasted_iota(jnp.int32, (ATT_BLOCK, 2 * ATT_BLOCK), 1)
    diff = qi + ATT_BLOCK - kj
    band = (diff >= 0) & (diff <= ATT_BLOCK)
    band_first = band & (kj >= ATT_BLOCK)
    lane = lax.broadcasted_iota(jnp.int32, (ATT_BLOCK, V7X_LANES), 1)
    ones = jnp.ones((2 * ATT_BLOCK, HEAD_DIM), BF16)
    zeros = jnp.zeros((ATT_BLOCK, HEAD_DIM), BF16)

    @pl.when(head == 0)
    def _():
        st_ref[...] = jnp.zeros_like(st_ref)

    rows_per = QKV_TILE // dil
    run = min(ATT_BLOCK, rows_per)

    def load(ref, qb, r):
        parts = []
        for idx in range(qb * ATT_BLOCK, (qb + 1) * ATT_BLOCK, run):
            tile, local = divmod(idx, rows_per)
            parts.append(ref[pl.ds(tile * QKV_TILE + r * rows_per + local, run), :])
        return parts[0] if len(parts) == 1 else jnp.concatenate(parts, axis=0)

    interleave = dil % 8 == 0

    def put(ref, qb, blocks, merge=None):
        if interleave:
            rows = pl.ds(qb * ATT_BLOCK * dil, ATT_BLOCK * dil)
            val = pltpu.einshape("rld->lrd", jnp.stack(blocks)).reshape(ATT_BLOCK * dil, V7X_LANES)
            ref[rows, :] = val if merge is None else merge(val, ref[rows, :])
            return
        for r, val in enumerate(blocks):
            rows = (pl.ds(qb * ATT_BLOCK * dil + r, ATT_BLOCK, stride=dil) if dil > 1
                    else pl.ds(qb * ATT_BLOCK, ATT_BLOCK))
            ref[rows, :] = val if merge is None else merge(val, ref[rows, :])

    def keep_other_heads(new, old):
        own = lax.broadcasted_iota(jnp.int32, new.shape, 1)
        return jnp.where((own == head) | (own == HEADS_PER_GROUP + head), new, old)

    k_prev, v_prev = [zeros] * dil, [zeros] * dil
    for qb in range(n_qb):
        outs, stats = [], []
        for r in range(dil):
            q, k_cur, v_cur = load(q_ref, qb, r), load(k_ref, qb, r), load(v_ref, qb, r)
            k_win = jnp.concatenate([k_prev[r], k_cur], axis=0)
            v_win = jnp.concatenate([v_prev[r], v_cur], axis=0)
            s = lax.dot_general(q, k_win, (((1,), (1,)), ((), ())), preferred_element_type=F32)
            s = jnp.where(band_first if qb == 0 else band, s, -jnp.inf)
            m = jnp.max(s, axis=-1, keepdims=True)
            p = jnp.exp(s - m).astype(BF16)
            pv = jnp.dot(p, jnp.concatenate([v_win, ones], axis=1), preferred_element_type=F32)
            l_b = pv[:, HEAD_DIM:]
            outs.append(pv[:, :HEAD_DIM] / l_b)
            stats.append(jnp.where(lane == head, m, l_b))
            k_prev[r], v_prev[r] = k_cur, v_cur
        put(o_ref, qb, outs)
        put(st_ref, qb, stats, merge=keep_other_heads)


def _attention(qk, v, group, batch):
    dil = DILATIONS[group]
    t = qk.shape[1]
    h0 = group * HEADS_PER_GROUP

    def seq(slab0):
        return pl.BlockSpec((None, SEQ, HEAD_DIM), lambda b, h: (slab0 + h, b, 0))

    return pl.pallas_call(
        functools.partial(_attn_kernel, dil=dil),
        grid=(batch, HEADS_PER_GROUP),
        in_specs=[seq(h0), seq(N_ATT_HEADS + h0), seq(h0)],
        out_specs=[
            pl.BlockSpec((None, SEQ, HEAD_DIM), lambda b, h: (h, b, 0)),
            pl.BlockSpec((SEQ, V7X_LANES), lambda b, h: (b, 0)),
        ],
        out_shape=[
            jax.ShapeDtypeStruct((HEADS_PER_GROUP, t, HEAD_DIM), F32),
            jax.ShapeDtypeStruct((t, V7X_LANES), F32),
        ],
        compiler_params=_params(("parallel", "arbitrary")),
        name=f"attn_d{dil}",
    )(qk, qk, v)


def _mix_prepare(o_refs, stats, u_ref, vs_ref, lng_ref, lnb_ref, sgw_ref, sgb_ref, oatt_ref, gated_ref):
    tm = oatt_ref.shape[0]
    for h in range(HEADS_PER_GROUP):
        ms = [s[:, h:h + 1] for s in stats]
        ls = [s[:, HEADS_PER_GROUP + h:HEADS_PER_GROUP + h + 1] for s in stats]
        m_all = jnp.maximum(jnp.maximum(ms[0], ms[1]), ms[2])
        ws = [l * jnp.exp(m - m_all) for l, m in zip(ls, ms)]
        num = ws[0] * o_refs[0][h] + ws[1] * o_refs[1][h] + ws[2] * o_refs[2][h]
        oatt_ref[:, h * HEAD_DIM:(h + 1) * HEAD_DIM] = (num / (ws[0] + ws[1] + ws[2])).astype(BF16)

    vs = vs_ref[...].astype(F32)
    mu = jnp.mean(vs, axis=-1, keepdims=True)
    var = jnp.mean(jnp.square(vs - mu), axis=-1, keepdims=True)
    vn = ((vs - mu) * lax.rsqrt(var + LN_EPS) * lng_ref[...] + lnb_ref[...]).astype(BF16)
    ti = lax.broadcasted_iota(jnp.int32, (SG_CHUNK, SG_CHUNK), 0)
    si = lax.broadcasted_iota(jnp.int32, (SG_CHUNK, SG_CHUNK), 1)
    causal = si <= ti
    for g in range(SG_GROUPS):
        cols = slice(g * SG_GROUP_DIM, (g + 1) * SG_GROUP_DIM)
        w_sp = jnp.where(causal, sgw_ref[g], 0.0).astype(BF16)
        bias = sgb_ref[:, g:g + 1]
        for c in range(tm // SG_CHUNK):
            rows = slice(c * SG_CHUNK, (c + 1) * SG_CHUNK)
            spatial = jnp.dot(w_sp, vn[rows, cols], preferred_element_type=F32) + bias
            gated_ref[rows, cols] = (u_ref[rows, cols].astype(F32) * spatial).astype(BF16)


def _mix_kernel(o0_ref, o1_ref, o2_ref, s0_ref, s1_ref, s2_ref, u_ref, vs_ref, ga_ref, gs_ref,
                lng_ref, lnb_ref, sgw_ref, sgb_ref, wa_ref, ws_ref, out_ref,
                oatt_a, gated_a, oatt_b, gated_b):
    i = pl.program_id(0)

    @pl.when(i == 0)
    def _():
        oatt_b[...] = jnp.zeros_like(oatt_b)
        gated_b[...] = jnp.zeros_like(gated_b)

    def step(oatt_w, gated_w, oatt_r, gated_r):
        y_att = jnp.dot(oatt_r[...], wa_ref[...], preferred_element_type=F32)
        y_sg = jnp.dot(gated_r[...], ws_ref[...], preferred_element_type=F32)
        merged = ga_ref[...].astype(F32) * y_att + gs_ref[...].astype(F32) * y_sg
        out_ref[...] = merged.astype(out_ref.dtype)
        _mix_prepare((o0_ref, o1_ref, o2_ref), (s0_ref[...], s1_ref[...], s2_ref[...]), u_ref, vs_ref,
                     lng_ref, lnb_ref, sgw_ref, sgb_ref, oatt_w, gated_w)

    @pl.when(i % 2 == 0)
    def _():
        step(oatt_a, gated_a, oatt_b, gated_b)

    @pl.when(i % 2 == 1)
    def _():
        step(oatt_b, gated_b, oatt_a, gated_a)


def _mix(o_list, st_list, uv, gates, ln_g, ln_b, sg_w, sg_b_t, w_att, w_sg, *, tm=512):
    t = uv.shape[0]
    n = t // tm
    cur = lambda i: jnp.minimum(i, n - 1)
    lag = lambda i: jnp.maximum(i - 1, 0)
    prep_row = lambda width, blk: pl.BlockSpec((tm, width), lambda i: (cur(i), blk))
    lag_row = lambda width, blk: pl.BlockSpec((tm, width), lambda i: (lag(i), blk))
    const2 = lambda shape: pl.BlockSpec(shape, lambda i: (0, 0), pipeline_mode=pl.Buffered(1))
    in_specs = (
        [pl.BlockSpec((HEADS_PER_GROUP, tm, HEAD_DIM), lambda i: (0, cur(i), 0))] * 3
        + [prep_row(V7X_LANES, 0)] * 3
        + [prep_row(SG_WIDTH, 0), prep_row(SG_WIDTH, 1), lag_row(D_MODEL, 0), lag_row(D_MODEL, 1)]
        + [const2((1, SG_WIDTH)), const2((1, SG_WIDTH)),
           pl.BlockSpec(sg_w.shape, lambda i: (0, 0, 0), pipeline_mode=pl.Buffered(1)),
           const2(sg_b_t.shape), const2(w_att.shape), const2(w_sg.shape)]
    )
    att_w = HEADS_PER_GROUP * HEAD_DIM
    return pl.pallas_call(
        _mix_kernel,
        grid=(n + 1,),
        in_specs=in_specs,
        out_specs=pl.BlockSpec((tm, D_MODEL), lambda i: (lag(i), 0)),
        out_shape=jax.ShapeDtypeStruct((t, D_MODEL), BF16),
        scratch_shapes=[pltpu.VMEM((tm, att_w), BF16), pltpu.VMEM((tm, SG_WIDTH), BF16),
                        pltpu.VMEM((tm, att_w), BF16), pltpu.VMEM((tm, SG_WIDTH), BF16)],
        compiler_params=_params(("arbitrary",)),
        name="mix",
    )(*o_list, *st_list, uv, uv, gates, gates, ln_g, ln_b, sg_w, sg_b_t, w_att, w_sg)


def _out_proj_kernel(m_ref, w_ref, x_ref, o_ref):
    o_ref[...] = x_ref[...] + jnp.dot(m_ref[...], w_ref[...], preferred_element_type=F32)


def _out_proj(merged, w, x, *, tm=1024):
    t, d = merged.shape
    n = w.shape[1]
    return pl.pallas_call(
        _out_proj_kernel,
        grid=(t // tm,),
        in_specs=[
            pl.BlockSpec((tm, d), lambda i: (i, 0)),
            pl.BlockSpec((d, n), lambda i: (0, 0), pipeline_mode=pl.Buffered(1)),
            pl.BlockSpec((tm, n), lambda i: (i, 0)),
        ],
        out_specs=pl.BlockSpec((tm, n), lambda i: (i, 0)),
        out_shape=jax.ShapeDtypeStruct((t, n), F32),
        compiler_params=_params(("parallel",)),
        name="out_proj",
    )(merged, w, x)


def kernel(x, ffn1_norm, ffn1_w_gate, ffn1_w_up, ffn1_w_down, mix_norm, w_in, sg_ln_g, sg_ln_b, sg_w, sg_b, w_att_out, w_sg_out, w_out, ffn2_norm, ffn2_w_gate, ffn2_w_up, ffn2_w_down, final_norm):
    batch, seq, d = x.shape
    assert (seq, d) == (SEQ, D_MODEL)
    t = batch * seq
    depth = ffn1_norm.shape[0]
    xt = x.reshape(t, d)
    tables = _rope_tables()
    bf = lambda a: a.astype(BF16)
    for i in range(depth):
        last = i == depth - 1
        c_v, c_u, c_g = 2 * ATT_WIDTH, 3 * ATT_WIDTH, 3 * ATT_WIDTH + 2 * SG_WIDTH
        whole = lambda w: (w, (0, w.shape[1]), 0)
        x1, hn, w_in_b, w_gates_b = _ffn(
            xt, ffn1_norm[i][None], bf(ffn1_w_gate[i]), bf(ffn1_w_up[i]), bf(ffn1_w_down[i]),
            mix_norm[i][None], post="hn", casts=((w_in[i], (0, c_g, w_in.shape[2]), c_v),))
        qk, w_out_b, w_sg_b, w_att_b = _proj(
            hn, w_in_b, 0, c_v, epilogue="rope", tn=ATT_WIDTH, tables=tables,
            casts=(whole(w_out[i]), whole(w_sg_out[i]), whole(w_att_out[i])))
        v = _proj(hn, w_in_b, c_v, ATT_WIDTH, epilogue="heads", tn=ATT_WIDTH)
        uv, w2_gate, w2_up = _proj(hn, w_in_b, c_u, 2 * SG_WIDTH, epilogue="gelu", tn=SG_WIDTH,
                                   casts=(whole(ffn2_w_gate[i]), whole(ffn2_w_up[i])))
        gates, w2_down = _proj(hn, w_gates_b, 0, 2 * D_MODEL, epilogue="sigmoid", tn=D_MODEL,
                               casts=(whole(ffn2_w_down[i]),))
        o_list, st_list = [], []
        for g in range(N_GROUPS):
            o_g, st_g = _attention(qk, v, g, batch)
            o_list.append(o_g)
            st_list.append(st_g)
        merged = _mix(o_list, st_list, uv, gates, sg_ln_g[i][None], sg_ln_b[i][None], sg_w[i], sg_b[i].T,
                      w_att_b, w_sg_b)
        x2 = _out_proj(merged, w_out_b, x1)
        (xt,) = _ffn(x2, ffn2_norm[i][None], w2_gate, w2_up, w2_down,
                     final_norm[None], post="final" if last else "plain")
    return xt.reshape(batch, seq, d)
```

```python
import functools

import jax
import jax.numpy as jnp
from jax import lax
from jax.experimental import pallas as pl
from jax.experimental.pallas import tpu as pltpu

F32 = jnp.float32
BF16 = jnp.bfloat16

D_MODEL = 2048
SEQ = 4096
HEAD_DIM = 128
HEADS_PER_GROUP = 4
DILATIONS = (1, 4, 16)
N_GROUPS = len(DILATIONS)
N_ATT_HEADS = N_GROUPS * HEADS_PER_GROUP
ATT_WIDTH = N_ATT_HEADS * HEAD_DIM
ATT_BLOCK = 128
QKV_TILE = 512
ROPE_DIM = HEAD_DIM // 4
ROPE_HALF = ROPE_DIM // 2
ROPE_THETA = 500000.0
SG_CHUNK = 128
SG_GROUPS = 12
SG_GROUP_DIM = 128
SG_WIDTH = SG_GROUPS * SG_GROUP_DIM
NORM_EPS = 1e-6
LN_EPS = 1e-5

V7X_VMEM_BYTES = 64 * 1024 * 1024
V7X_LANES = 128
BF16_ROWS = 16
CAST_STEPS = 128
VMEM_LIMIT = 52 * 1024 * 1024


def _params(semantics, vmem_limit=VMEM_LIMIT):
    return pltpu.CompilerParams(dimension_semantics=semantics, vmem_limit_bytes=vmem_limit)


def _rms(x, g):
    return x * lax.rsqrt(jnp.mean(x * x, axis=-1, keepdims=True) + NORM_EPS) * g


ROPE_PAIR_LANE = V7X_LANES // 2


def _rope_lane_masks(lane):
    first = lane < ROPE_HALF
    second = (lane >= ROPE_PAIR_LANE) & (lane < ROPE_PAIR_LANE + ROPE_HALF)
    return first, second


def _rope_lane_layout(x):
    cols = x.shape[1]
    lane = lax.broadcasted_iota(jnp.int32, x.shape, 1) % V7X_LANES
    shift = ROPE_PAIR_LANE - ROPE_HALF
    to_pair_lane = (lane >= ROPE_PAIR_LANE) & (lane < ROPE_PAIR_LANE + ROPE_HALF)
    from_pair_lane = (lane >= ROPE_HALF) & (lane < ROPE_DIM)
    return jnp.where(to_pair_lane, pltpu.roll(x, shift, 1),
                     jnp.where(from_pair_lane, pltpu.roll(x, cols - shift, 1), x))


def _rope_table_kernel(freq_ref, cos_ref, sin_ref):
    rows = cos_ref.shape[0]
    pos = (pl.program_id(0) * rows + lax.broadcasted_iota(jnp.int32, (rows, V7X_LANES), 0)).astype(F32)
    first, second = _rope_lane_masks(lax.broadcasted_iota(jnp.int32, (rows, V7X_LANES), 1))
    ang = pos * freq_ref[...]
    c = jnp.cos(ang)
    s = jnp.sin(ang)
    cos_ref[...] = jnp.where(first | second, c, 1.0)
    sin_ref[...] = jnp.where(first, -s, jnp.where(second, s, 0.0))


def _rope_tables():
    inv_freq = ROPE_THETA ** (-jnp.arange(0, ROPE_DIM, 2, dtype=F32) / ROPE_DIM)
    gap = jnp.zeros((ROPE_PAIR_LANE - ROPE_HALF,), F32)
    freq_lane = jnp.concatenate([inv_freq, gap, inv_freq, gap])[None, :]
    rows = 512
    tab = jax.ShapeDtypeStruct((SEQ, V7X_LANES), F32)
    spec = pl.BlockSpec((rows, V7X_LANES), lambda i: (i, 0))
    return pl.pallas_call(
        _rope_table_kernel,
        grid=(SEQ // rows,),
        in_specs=[pl.BlockSpec((1, V7X_LANES), lambda i: (0, 0))],
        out_specs=[spec, spec],
        out_shape=[tab, tab],
        compiler_params=_params(("parallel",)),
        name="rope_table",
    )(freq_lane)


def _cast_specs(casts, n_steps, step_of):
    in_specs, out_specs, out_shapes = [], [], []
    for w, splits, _ in casts:
        rows, cols = w.shape
        assert splits[0] == 0 and splits[-1] == cols
        rb = next(r for r in range(BF16_ROWS, rows + 1, BF16_ROWS) if rows % r == 0 and rows // r <= n_steps)
        n_blocks = rows // rb
        block_of = lambda *idx, n_blocks=n_blocks: (jnp.minimum(step_of(*idx), n_blocks - 1), 0)
        in_specs.append(pl.BlockSpec((rb, cols), block_of))
        for c0, c1 in zip(splits[:-1], splits[1:]):
            out_specs.append(pl.BlockSpec((rb, c1 - c0), block_of))
            out_shapes.append(jax.ShapeDtypeStruct((rows, c1 - c0), BF16))
    return in_specs, out_specs, out_shapes


def _cast_blocks(cast_in, cast_out, cast_meta):
    cast_out = iter(cast_out)
    for src, (splits, rope_cols) in zip(cast_in, cast_meta):
        for c0, c1 in zip(splits[:-1], splits[1:]):
            dst = next(cast_out)
            n_rope = min(max(rope_cols - c0, 0), c1 - c0)
            if n_rope:
                dst[:, :n_rope] = _rope_lane_layout(src[:, c0:c0 + n_rope]).astype(BF16)
            if n_rope < c1 - c0:
                dst[:, n_rope:] = src[:, c0 + n_rope:c1].astype(BF16)


def _cast_meta(casts):
    return tuple((splits, rope_cols) for _, splits, rope_cols in casts)


def _ffn_kernel(x_ref, g_ref, wg_ref, wu_ref, wd_ref, pg_ref, *refs, post, cast_meta):
    n_cast = len(cast_meta)
    cast_in, refs = refs[:n_cast], refs[n_cast:]
    if post == "hn":
        o_ref, hn_ref, *cast_out, xn_ref = refs
    else:
        o_ref, *cast_out, xn_ref = refs
    j = pl.program_id(1)
    last = pl.num_programs(1) - 1

    def tile_contribution(xn):
        _cast_blocks(cast_in, cast_out, cast_meta)
        h = jnp.dot(xn, wg_ref[...], preferred_element_type=F32)
        u = jnp.dot(xn, wu_ref[...], preferred_element_type=F32)
        a = (jax.nn.silu(h) * u).astype(BF16)
        return jnp.dot(a, wd_ref[...], preferred_element_type=F32)

    @pl.when(j == 0)
    def _():
        xn = _rms(x_ref[...], g_ref[...]).astype(BF16)
        xn_ref[...] = xn
        o_ref[...] = tile_contribution(xn)

    @pl.when((j > 0) & (j < last))
    def _():
        o_ref[...] += tile_contribution(xn_ref[...])

    @pl.when(j == last)
    def _():
        y = x_ref[...] + 0.5 * (o_ref[...] + tile_contribution(xn_ref[...]))
        if post == "hn":
            o_ref[...] = y
            hn_ref[...] = _rms(y, pg_ref[...]).astype(BF16)
        elif post == "final":
            o_ref[...] = _rms(y, pg_ref[...])
        else:
            o_ref[...] = y


def _ffn(x, g, wg, wu, wd, post_g, *, post, casts=(), tm=512, tf=512):
    t, d = x.shape
    f = wg.shape[1]
    n_j = f // tf
    row = pl.BlockSpec((tm, d), lambda i, j: (i, 0))
    vec = pl.BlockSpec((1, d), lambda i, j: (0, 0))
    out_shape = [jax.ShapeDtypeStruct((t, d), F32)]
    out_specs = [row]
    if post == "hn":
        out_shape.append(jax.ShapeDtypeStruct((t, d), BF16))
        out_specs.append(row)
    cast_specs, cast_out_specs, cast_out_shapes = _cast_specs(
        casts, min(CAST_STEPS, (t // tm) * n_j), lambda i, j: i * n_j + j)
    out_specs += cast_out_specs
    out_shape += cast_out_shapes
    return pl.pallas_call(
        functools.partial(_ffn_kernel, post=post, cast_meta=_cast_meta(casts)),
        grid=(t // tm, n_j),
        in_specs=[
            row,
            vec,
            pl.BlockSpec((d, tf), lambda i, j: (0, j)),
            pl.BlockSpec((d, tf), lambda i, j: (0, j)),
            pl.BlockSpec((tf, d), lambda i, j: (j, 0)),
            vec,
        ] + cast_specs,
        out_specs=out_specs,
        out_shape=out_shape,
        scratch_shapes=[pltpu.VMEM((tm, d), BF16)],
        compiler_params=_params(("arbitrary", "arbitrary")),
        name="ffn_" + post,
    )(x, g, wg, wu, wd, post_g, *[c[0] for c in casts])


def _store_head(o_ref, h, val):
    dil = DILATIONS[h % N_ATT_HEADS // HEADS_PER_GROUP]
    if dil == 1:
        o_ref[h] = val.astype(BF16)
        return
    regrouped = pltpu.einshape("lrd->rld", val.reshape(QKV_TILE // dil, dil, HEAD_DIM))
    o_ref[h] = regrouped.reshape(QKV_TILE, HEAD_DIM).astype(BF16)


def _proj_kernel(h_ref, w_ref, *refs, epilogue, cast_meta):
    n_cast = len(cast_meta)
    n_cast_out = sum(len(splits) - 1 for splits, _ in cast_meta)
    n_tables = 2 if epilogue == "qkv" else 0
    cast_in = refs[n_tables:n_tables + n_cast]
    cast_out = refs[n_tables + n_cast + 1:n_tables + n_cast + 1 + n_cast_out]
    refs = refs[:n_tables] + refs[n_tables + n_cast:n_tables + n_cast + 1] + refs[n_tables + n_cast + 1 + n_cast_out:]
    _cast_blocks(cast_in, cast_out, cast_meta)

    if epilogue == "qkv":
        cos_ref, sin_ref, o_ref = refs
        cos, sin = cos_ref[...], sin_ref[...]
        rope = ((cos * HEAD_DIM ** -0.5, sin * HEAD_DIM ** -0.5), (cos, sin), None)
        lhs = h_ref[...]
        group_cols = HEADS_PER_GROUP * HEAD_DIM
        for kind, tables in enumerate(rope):
            for g in reversed(range(N_GROUPS)):
                h0 = kind * N_ATT_HEADS + g * HEADS_PER_GROUP
                acc = jnp.dot(lhs, w_ref[:, h0 * HEAD_DIM:h0 * HEAD_DIM + group_cols], preferred_element_type=F32)
                for hh in range(HEADS_PER_GROUP):
                    t = acc[:, hh * HEAD_DIM:(hh + 1) * HEAD_DIM]
                    if tables is not None:
                        t = t * tables[0] + pltpu.roll(t, ROPE_PAIR_LANE, 1) * tables[1]
                    _store_head(o_ref, h0 + hh, t)
        return

    (o_ref,) = refs
    acc = jnp.dot(h_ref[...], w_ref[...], preferred_element_type=F32)
    if epilogue == "gelu":
        acc = 0.5 * acc * (1.0 + lax.erf(acc * (0.5 ** 0.5)))
    elif epilogue == "sigmoid":
        acc = jax.nn.sigmoid(acc)
    o_ref[...] = acc.astype(o_ref.dtype)


def _proj(hn, w, col0, ncols, *, epilogue, tn, tm, tables=(), casts=()):
    t, d = hn.shape
    assert col0 % tn == 0 and ncols % tn == 0
    c0 = col0 // tn
    n_j = ncols // tn
    in_specs = [
        pl.BlockSpec((tm, d), lambda i, j: (i, 0)),
        pl.BlockSpec((d, tn), lambda i, j: (0, c0 + j), pipeline_mode=pl.Buffered(1) if n_j == 1 else None),
    ]
    seq_blocks = SEQ // tm
    for _ in tables:
        in_specs.append(pl.BlockSpec((tm, V7X_LANES), lambda i, j: (i % seq_blocks, 0)))
    if epilogue == "qkv":
        assert tm == QKV_TILE and tn == ncols == 3 * ATT_WIDTH
        out_spec = pl.BlockSpec((3 * N_ATT_HEADS, tm, HEAD_DIM), lambda i, j: (0, i, 0))
        out_shape = jax.ShapeDtypeStruct((3 * N_ATT_HEADS, t, HEAD_DIM), BF16)
    else:
        out_spec = pl.BlockSpec((tm, tn), lambda i, j: (i, j))
        out_shape = jax.ShapeDtypeStruct((t, ncols), BF16)
    cast_specs, cast_out_specs, cast_out_shapes = _cast_specs(casts, (t // tm) * n_j, lambda i, j: i * n_j + j)
    res = pl.pallas_call(
        functools.partial(_proj_kernel, epilogue=epilogue, cast_meta=_cast_meta(casts)),
        grid=(t // tm, n_j),
        in_specs=in_specs + cast_specs,
        out_specs=[out_spec] + cast_out_specs,
        out_shape=[out_shape] + cast_out_shapes,
        compiler_params=_params(("arbitrary", "arbitrary")),
        name="proj_" + epilogue,
    )(hn, w, *tables, *[c[0] for c in casts])
    return res if casts else res[0]


def _attn_kernel(q_ref, k_ref, v_ref, o_ref, st_ref, *, dil):
    head = pl.program_id(1)
    n_qb = SEQ // (ATT_BLOCK * dil)

    qi = lax.broadcasted_iota(jnp.int32, (ATT_BLOCK, 2 * ATT_BLOCK), 0)
    kj = lax.broadcasted_iota(jnp.int32, (ATT_BLOCK, 2 * ATT_BLOCK), 1)
    diff = qi + ATT_BLOCK - kj
    band = (diff >= 0) & (diff <= ATT_BLOCK)
    band_first = band & (kj >= ATT_BLOCK)
    lane = lax.broadcasted_iota(jnp.int32, (ATT_BLOCK, V7X_LANES), 1)
    ones = jnp.ones((2 * ATT_BLOCK, HEAD_DIM), BF16)
    zeros = jnp.zeros((ATT_BLOCK, HEAD_DIM), BF16)

    @pl.when(head == 0)
    def _():
        st_ref[...] = jnp.zeros_like(st_ref)

    rows_per = QKV_TILE // dil
    run = min(ATT_BLOCK, rows_per)

    def load(ref, qb, r):
        parts = []
        for idx in range(qb * ATT_BLOCK, (qb + 1) * ATT_BLOCK, run):
            tile, local = divmod(idx, rows_per)
            parts.append(ref[pl.ds(tile * QKV_TILE + r * rows_per + local, run), :])
        return parts[0] if len(parts) == 1 else jnp.concatenate(parts, axis=0)

    interleave = dil % 8 == 0

    def put(ref, qb, blocks, merge=None):
        if interleave:
            rows = pl.ds(qb * ATT_BLOCK * dil, ATT_BLOCK * dil)
            val = pltpu.einshape("rld->lrd", jnp.stack(blocks)).reshape(ATT_BLOCK * dil, V7X_LANES)
            ref[rows, :] = val if merge is None else merge(val, ref[rows, :])
            return
        for r, val in enumerate(blocks):
            rows = (pl.ds(qb * ATT_BLOCK * dil + r, ATT_BLOCK, stride=dil) if dil > 1
                    else pl.ds(qb * ATT_BLOCK, ATT_BLOCK))
            ref[rows, :] = val if merge is None else merge(val, ref[rows, :])

    def keep_other_heads(new, old):
        own = lax.broadcasted_iota(jnp.int32, new.shape, 1)
        return jnp.where((own == head) | (own == HEADS_PER_GROUP + head), new, old)

    k_prev, v_prev = [zeros] * dil, [zeros] * dil
    for qb in range(n_qb):
        outs, stats = [], []
        for r in range(dil):
            q, k_cur, v_cur = load(q_ref, qb, r), load(k_ref, qb, r), load(v_ref, qb, r)
            k_win = jnp.concatenate([k_prev[r], k_cur], axis=0)
            v_win = jnp.concatenate([v_prev[r], v_cur], axis=0)
            s = lax.dot_general(q, k_win, (((1,), (1,)), ((), ())), preferred_element_type=F32)
            s = jnp.where(band_first if qb == 0 else band, s, -jnp.inf)
            m = jnp.max(s, axis=-1, keepdims=True)
            p = jnp.exp(s - m).astype(BF16)
            pv = jnp.dot(p, jnp.concatenate([v_win, ones], axis=1), preferred_element_type=F32)
            l_b = pv[:, HEAD_DIM:]
            outs.append(pv[:, :HEAD_DIM] / l_b)
            stats.append(jnp.where(lane == head, m, l_b))
            k_prev[r], v_prev[r] = k_cur, v_cur
        put(o_ref, qb, outs)
        put(st_ref, qb, stats, merge=keep_other_heads)


def _attention(qkv, group, batch):
    dil = DILATIONS[group]
    t = qkv.shape[1]
    h0 = group * HEADS_PER_GROUP

    def seq(slab0):
        return pl.BlockSpec((None, SEQ, HEAD_DIM), lambda b, h: (slab0 + h, b, 0))

    return pl.pallas_call(
        functools.partial(_attn_kernel, dil=dil),
        grid=(batch, HEADS_PER_GROUP),
        in_specs=[seq(h0), seq(N_ATT_HEADS + h0), seq(2 * N_ATT_HEADS + h0)],
        out_specs=[
            pl.BlockSpec((None, SEQ, HEAD_DIM), lambda b, h: (h, b, 0)),
            pl.BlockSpec((SEQ, V7X_LANES), lambda b, h: (b, 0)),
        ],
        out_shape=[
            jax.ShapeDtypeStruct((HEADS_PER_GROUP, t, HEAD_DIM), F32),
            jax.ShapeDtypeStruct((t, V7X_LANES), F32),
        ],
        compiler_params=_params(("parallel", "arbitrary")),
        name=f"attn_d{dil}",
    )(qkv, qkv, qkv)


def _mix_prepare(o_refs, stats, u_ref, vs_ref, lng_ref, lnb_ref, sgw_ref, sgb_ref, oatt_ref, gated_ref):
    tm = oatt_ref.shape[0]
    for h in range(HEADS_PER_GROUP):
        ms = [s[:, h:h + 1] for s in stats]
        ls = [s[:, HEADS_PER_GROUP + h:HEADS_PER_GROUP + h + 1] for s in stats]
        m_all = jnp.maximum(jnp.maximum(ms[0], ms[1]), ms[2])
        ws = [l * jnp.exp(m - m_all) for l, m in zip(ls, ms)]
        num = ws[0] * o_refs[0][h] + ws[1] * o_refs[1][h] + ws[2] * o_refs[2][h]
        oatt_ref[:, h * HEAD_DIM:(h + 1) * HEAD_DIM] = (num / (ws[0] + ws[1] + ws[2])).astype(BF16)

    vs = vs_ref[...].astype(F32)
    mu = jnp.mean(vs, axis=-1, keepdims=True)
    var = jnp.mean(jnp.square(vs - mu), axis=-1, keepdims=True)
    vn = ((vs - mu) * lax.rsqrt(var + LN_EPS) * lng_ref[...] + lnb_ref[...]).astype(BF16)
    ti = lax.broadcasted_iota(jnp.int32, (SG_CHUNK, SG_CHUNK), 0)
    si = lax.broadcasted_iota(jnp.int32, (SG_CHUNK, SG_CHUNK), 1)
    causal = si <= ti
    for g in range(SG_GROUPS):
        cols = slice(g * SG_GROUP_DIM, (g + 1) * SG_GROUP_DIM)
        w_sp = jnp.where(causal, sgw_ref[g], 0.0).astype(BF16)
        bias = sgb_ref[:, g:g + 1]
        for c in range(tm // SG_CHUNK):
            rows = slice(c * SG_CHUNK, (c + 1) * SG_CHUNK)
            spatial = jnp.dot(w_sp, vn[rows, cols], preferred_element_type=F32) + bias
            gated_ref[rows, cols] = (u_ref[rows, cols].astype(F32) * spatial).astype(BF16)


def _mix_kernel(o0_ref, o1_ref, o2_ref, s0_ref, s1_ref, s2_ref, u_ref, vs_ref, ga_ref, gs_ref,
                lng_ref, lnb_ref, sgw_ref, sgb_ref, wa_ref, ws_ref, out_ref,
                oatt_a, gated_a, oatt_b, gated_b):
    i = pl.program_id(0)

    @pl.when(i == 0)
    def _():
        oatt_b[...] = jnp.zeros_like(oatt_b)
        gated_b[...] = jnp.zeros_like(gated_b)

    def step(oatt_w, gated_w, oatt_r, gated_r):
        y_att = jnp.dot(oatt_r[...], wa_ref[...], preferred_element_type=F32)
        y_sg = jnp.dot(gated_r[...], ws_ref[...], preferred_element_type=F32)
        merged = ga_ref[...].astype(F32) * y_att + gs_ref[...].astype(F32) * y_sg
        out_ref[...] = merged.astype(out_ref.dtype)
        _mix_prepare((o0_ref, o1_ref, o2_ref), (s0_ref[...], s1_ref[...], s2_ref[...]), u_ref, vs_ref,
                     lng_ref, lnb_ref, sgw_ref, sgb_ref, oatt_w, gated_w)

    @pl.when(i % 2 == 0)
    def _():
        step(oatt_a, gated_a, oatt_b, gated_b)

    @pl.when(i % 2 == 1)
    def _():
        step(oatt_b, gated_b, oatt_a, gated_a)


def _mix(o_list, st_list, uv, gates, ln_g, ln_b, sg_w, sg_b_t, w_att, w_sg, *, tm=512):
    t = uv.shape[0]
    n = t // tm
    cur = lambda i: jnp.minimum(i, n - 1)
    lag = lambda i: jnp.maximum(i - 1, 0)
    prep_row = lambda width, blk: pl.BlockSpec((tm, width), lambda i: (cur(i), blk))
    lag_row = lambda width, blk: pl.BlockSpec((tm, width), lambda i: (lag(i), blk))
    const2 = lambda shape: pl.BlockSpec(shape, lambda i: (0, 0), pipeline_mode=pl.Buffered(1))
    in_specs = (
        [pl.BlockSpec((HEADS_PER_GROUP, tm, HEAD_DIM), lambda i: (0, cur(i), 0))] * 3
        + [prep_row(V7X_LANES, 0)] * 3
        + [prep_row(SG_WIDTH, 0), prep_row(SG_WIDTH, 1), lag_row(D_MODEL, 0), lag_row(D_MODEL, 1)]
        + [const2((1, SG_WIDTH)), const2((1, SG_WIDTH)),
           pl.BlockSpec(sg_w.shape, lambda i: (0, 0, 0), pipeline_mode=pl.Buffered(1)),
           const2(sg_b_t.shape), const2(w_att.shape), const2(w_sg.shape)]
    )
    att_w = HEADS_PER_GROUP * HEAD_DIM
    return pl.pallas_call(
        _mix_kernel,
        grid=(n + 1,),
        in_specs=in_specs,
        out_specs=pl.BlockSpec((tm, D_MODEL), lambda i: (lag(i), 0)),
        out_shape=jax.ShapeDtypeStruct((t, D_MODEL), BF16),
        scratch_shapes=[pltpu.VMEM((tm, att_w), BF16), pltpu.VMEM((tm, SG_WIDTH), BF16),
                        pltpu.VMEM((tm, att_w), BF16), pltpu.VMEM((tm, SG_WIDTH), BF16)],
        compiler_params=_params(("arbitrary",)),
        name="mix",
    )(*o_list, *st_list, uv, uv, gates, gates, ln_g, ln_b, sg_w, sg_b_t, w_att, w_sg)


def _out_proj_kernel(m_ref, w_ref, x_ref, o_ref):
    o_ref[...] = x_ref[...] + jnp.dot(m_ref[...], w_ref[...], preferred_element_type=F32)


def _out_proj(merged, w, x, *, tm=1024):
    t, d = merged.shape
    n = w.shape[1]
    return pl.pallas_call(
        _out_proj_kernel,
        grid=(t // tm,),
        in_specs=[
            pl.BlockSpec((tm, d), lambda i: (i, 0)),
            pl.BlockSpec((d, n), lambda i: (0, 0), pipeline_mode=pl.Buffered(1)),
            pl.BlockSpec((tm, n), lambda i: (i, 0)),
        ],
        out_specs=pl.BlockSpec((tm, n), lambda i: (i, 0)),
        out_shape=jax.ShapeDtypeStruct((t, n), F32),
        compiler_params=_params(("parallel",)),
        name="out_proj",
    )(merged, w, x)


def kernel(x, ffn1_norm, ffn1_w_gate, ffn1_w_up, ffn1_w_down, mix_norm, w_in, sg_ln_g, sg_ln_b, sg_w, sg_b, w_att_out, w_sg_out, w_out, ffn2_norm, ffn2_w_gate, ffn2_w_up, ffn2_w_down, final_norm):
    batch, seq, d = x.shape
    assert (seq, d) == (SEQ, D_MODEL)
    t = batch * seq
    depth = ffn1_norm.shape[0]
    xt = x.reshape(t, d)
    tables = _rope_tables()
    bf = lambda a: a.astype(BF16)
    for i in range(depth):
        last = i == depth - 1
        c_v, c_u, c_g = 2 * ATT_WIDTH, 3 * ATT_WIDTH, 3 * ATT_WIDTH + 2 * SG_WIDTH
        whole = lambda w: (w, (0, w.shape[1]), 0)
        x1, hn, w_in_b, w_gates_b = _ffn(
            xt, ffn1_norm[i][None], bf(ffn1_w_gate[i]), bf(ffn1_w_up[i]), bf(ffn1_w_down[i]),
            mix_norm[i][None], post="hn", casts=((w_in[i], (0, c_g, w_in.shape[2]), c_v),))
        qkv, w_out_b, w_sg_b, w_att_b = _proj(
            hn, w_in_b, 0, c_u, epilogue="qkv", tn=c_u, tm=QKV_TILE, tables=tables,
            casts=(whole(w_out[i]), whole(w_sg_out[i]), whole(w_att_out[i])))
        uv, w2_gate, w2_up = _proj(hn, w_in_b, c_u, 2 * SG_WIDTH, epilogue="gelu", tn=SG_WIDTH, tm=1024,
                                   casts=(whole(ffn2_w_gate[i]), whole(ffn2_w_up[i])))
        gates, w2_down = _proj(hn, w_gates_b, 0, 2 * D_MODEL, epilogue="sigmoid", tn=2 * D_MODEL, tm=512,
                               casts=(whole(ffn2_w_down[i]),))
        o_list, st_list = [], []
        for g in range(N_GROUPS):
            o_g, st_g = _attention(qkv, g, batch)
            o_list.append(o_g)
            st_list.append(st_g)
        merged = _mix(o_list, st_list, uv, gates, sg_ln_g[i][None], sg_ln_b[i][None], sg_w[i], sg_b[i].T,
                      w_att_b, w_sg_b)
        x2 = _out_proj(merged, w_out_b, x1)
        (xt,) = _ffn(x2, ffn2_norm[i][None], w2_gate, w2_up, w2_down,
                     final_norm[None], post="final" if last else "plain")
    return xt.reshape(batch, seq, d)
```

```python
import functools

import jax
import jax.numpy as jnp
from jax import lax
from jax.experimental import pallas as pl
from jax.experimental.pallas import tpu as pltpu

F32 = jnp.float32
BF16 = jnp.bfloat16

D_MODEL = 2048
SEQ = 4096
HEAD_DIM = 128
HEADS_PER_GROUP = 4
DILATIONS = (1, 4, 16)
N_GROUPS = len(DILATIONS)
N_ATT_HEADS = N_GROUPS * HEADS_PER_GROUP
ATT_WIDTH = N_ATT_HEADS * HEAD_DIM
ATT_BLOCK = 128
QKV_TILE = 512
ROPE_DIM = HEAD_DIM // 4
ROPE_HALF = ROPE_DIM // 2
ROPE_THETA = 500000.0
SG_CHUNK = 128
SG_GROUPS = 12
SG_GROUP_DIM = 128
SG_WIDTH = SG_GROUPS * SG_GROUP_DIM
NORM_EPS = 1e-6
LN_EPS = 1e-5

V7X_VMEM_BYTES = 64 * 1024 * 1024
V7X_LANES = 128
BF16_ROWS = 16
CAST_STEPS = 128
VMEM_LIMIT = 52 * 1024 * 1024


def _params(semantics, vmem_limit=VMEM_LIMIT):
    return pltpu.CompilerParams(dimension_semantics=semantics, vmem_limit_bytes=vmem_limit)


def _rms(x, g):
    return x * lax.rsqrt(jnp.mean(x * x, axis=-1, keepdims=True) + NORM_EPS) * g


ROPE_PAIR_LANE = V7X_LANES // 2


def _rope_lane_masks(lane):
    first = lane < ROPE_HALF
    second = (lane >= ROPE_PAIR_LANE) & (lane < ROPE_PAIR_LANE + ROPE_HALF)
    return first, second


def _rope_lane_layout(x):
    cols = x.shape[1]
    lane = lax.broadcasted_iota(jnp.int32, x.shape, 1) % V7X_LANES
    shift = ROPE_PAIR_LANE - ROPE_HALF
    to_pair_lane = (lane >= ROPE_PAIR_LANE) & (lane < ROPE_PAIR_LANE + ROPE_HALF)
    from_pair_lane = (lane >= ROPE_HALF) & (lane < ROPE_DIM)
    return jnp.where(to_pair_lane, pltpu.roll(x, shift, 1),
                     jnp.where(from_pair_lane, pltpu.roll(x, cols - shift, 1), x))


def _rope_table_kernel(freq_ref, cos_ref, sin_ref):
    rows = cos_ref.shape[0]
    pos = (pl.program_id(0) * rows + lax.broadcasted_iota(jnp.int32, (rows, V7X_LANES), 0)).astype(F32)
    first, second = _rope_lane_masks(lax.broadcasted_iota(jnp.int32, (rows, V7X_LANES), 1))
    ang = pos * freq_ref[...]
    c = jnp.cos(ang)
    s = jnp.sin(ang)
    cos_ref[...] = jnp.where(first | second, c, 1.0)
    sin_ref[...] = jnp.where(first, -s, jnp.where(second, s, 0.0))


def _rope_tables():
    inv_freq = ROPE_THETA ** (-jnp.arange(0, ROPE_DIM, 2, dtype=F32) / ROPE_DIM)
    gap = jnp.zeros((ROPE_PAIR_LANE - ROPE_HALF,), F32)
    freq_lane = jnp.concatenate([inv_freq, gap, inv_freq, gap])[None, :]
    rows = 512
    tab = jax.ShapeDtypeStruct((SEQ, V7X_LANES), F32)
    spec = pl.BlockSpec((rows, V7X_LANES), lambda i: (i, 0))
    return pl.pallas_call(
        _rope_table_kernel,
        grid=(SEQ // rows,),
        in_specs=[pl.BlockSpec((1, V7X_LANES), lambda i: (0, 0))],
        out_specs=[spec, spec],
        out_shape=[tab, tab],
        compiler_params=_params(("parallel",)),
        name="rope_table",
    )(freq_lane)


def _cast_specs(casts, n_steps, step_of):
    in_specs, out_specs, out_shapes = [], [], []
    for w, splits, _ in casts:
        rows, cols = w.shape
        assert splits[0] == 0 and splits[-1] == cols
        rb = next(r for r in range(BF16_ROWS, rows + 1, BF16_ROWS) if rows % r == 0 and rows // r <= n_steps)
        n_blocks = rows // rb
        block_of = lambda *idx, n_blocks=n_blocks: (jnp.minimum(step_of(*idx), n_blocks - 1), 0)
        in_specs.append(pl.BlockSpec((rb, cols), block_of))
        for c0, c1 in zip(splits[:-1], splits[1:]):
            out_specs.append(pl.BlockSpec((rb, c1 - c0), block_of))
            out_shapes.append(jax.ShapeDtypeStruct((rows, c1 - c0), BF16))
    return in_specs, out_specs, out_shapes


def _cast_blocks(cast_in, cast_out, cast_meta):
    cast_out = iter(cast_out)
    for src, (splits, rope_cols) in zip(cast_in, cast_meta):
        for c0, c1 in zip(splits[:-1], splits[1:]):
            dst = next(cast_out)
            n_rope = min(max(rope_cols - c0, 0), c1 - c0)
            if n_rope:
                dst[:, :n_rope] = _rope_lane_layout(src[:, c0:c0 + n_rope]).astype(BF16)
            if n_rope < c1 - c0:
                dst[:, n_rope:] = src[:, c0 + n_rope:c1].astype(BF16)


def _cast_meta(casts):
    return tuple((splits, rope_cols) for _, splits, rope_cols in casts)


def _ffn_kernel(x_ref, g_ref, wg_ref, wu_ref, wd_ref, pg_ref, *refs, post, cast_meta):
    n_cast = len(cast_meta)
    cast_in, refs = refs[:n_cast], refs[n_cast:]
    if post == "hn":
        o_ref, hn_ref, *cast_out, xn_ref = refs
    else:
        o_ref, *cast_out, xn_ref = refs
    j = pl.program_id(1)
    last = pl.num_programs(1) - 1

    def tile_contribution(xn):
        _cast_blocks(cast_in, cast_out, cast_meta)
        h = jnp.dot(xn, wg_ref[...], preferred_element_type=F32)
        u = jnp.dot(xn, wu_ref[...], preferred_element_type=F32)
        a = ((0.5 * h) * (1.0 + jnp.tanh(0.5 * h)) * u).astype(BF16)
        return jnp.dot(a, wd_ref[...], preferred_element_type=F32)

    @pl.when(j == 0)
    def _():
        xn = _rms(x_ref[...], g_ref[...]).astype(BF16)
        xn_ref[...] = xn
        o_ref[...] = tile_contribution(xn)

    @pl.when((j > 0) & (j < last))
    def _():
        o_ref[...] += tile_contribution(xn_ref[...])

    @pl.when(j == last)
    def _():
        y = x_ref[...] + 0.5 * (o_ref[...] + tile_contribution(xn_ref[...]))
        if post == "hn":
            o_ref[...] = y
            hn_ref[...] = _rms(y, pg_ref[...]).astype(BF16)
        elif post == "final":
            o_ref[...] = _rms(y, pg_ref[...])
        else:
            o_ref[...] = y


def _ffn(x, g, wg, wu, wd, post_g, *, post, casts=(), tm=512, tf=512):
    t, d = x.shape
    f = wg.shape[1]
    n_j = f // tf
    row = pl.BlockSpec((tm, d), lambda i, j: (i, 0))
    vec = pl.BlockSpec((1, d), lambda i, j: (0, 0))
    out_shape = [jax.ShapeDtypeStruct((t, d), F32)]
    out_specs = [row]
    if post == "hn":
        out_shape.append(jax.ShapeDtypeStruct((t, d), BF16))
        out_specs.append(row)
    cast_specs, cast_out_specs, cast_out_shapes = _cast_specs(
        casts, min(CAST_STEPS, (t // tm) * n_j), lambda i, j: i * n_j + j)
    out_specs += cast_out_specs
    out_shape += cast_out_shapes
    return pl.pallas_call(
        functools.partial(_ffn_kernel, post=post, cast_meta=_cast_meta(casts)),
        grid=(t // tm, n_j),
        in_specs=[
            row,
            vec,
            pl.BlockSpec((d, tf), lambda i, j: (0, j)),
            pl.BlockSpec((d, tf), lambda i, j: (0, j)),
            pl.BlockSpec((tf, d), lambda i, j: (j, 0)),
            vec,
        ] + cast_specs,
        out_specs=out_specs,
        out_shape=out_shape,
        scratch_shapes=[pltpu.VMEM((tm, d), BF16)],
        compiler_params=_params(("arbitrary", "arbitrary")),
        name="ffn_" + post,
    )(x, g, wg, wu, wd, post_g, *[c[0] for c in casts])


def _store_head(o_ref, h, val):
    dil = DILATIONS[h % N_ATT_HEADS // HEADS_PER_GROUP]
    if dil == 1:
        o_ref[h] = val.astype(BF16)
        return
    regrouped = pltpu.einshape("lrd->rld", val.reshape(QKV_TILE // dil, dil, HEAD_DIM))
    o_ref[h] = regrouped.reshape(QKV_TILE, HEAD_DIM).astype(BF16)


def _proj_kernel(h_ref, w_ref, *refs, epilogue, cast_meta):
    n_cast = len(cast_meta)
    n_cast_out = sum(len(splits) - 1 for splits, _ in cast_meta)
    n_tables = 2 if epilogue == "qkv" else 0
    cast_in = refs[n_tables:n_tables + n_cast]
    cast_out = refs[n_tables + n_cast + 1:n_tables + n_cast + 1 + n_cast_out]
    refs = refs[:n_tables] + refs[n_tables + n_cast:n_tables + n_cast + 1] + refs[n_tables + n_cast + 1 + n_cast_out:]
    _cast_blocks(cast_in, cast_out, cast_meta)

    if epilogue == "qkv":
        cos_ref, sin_ref, o_ref = refs
        cos, sin = cos_ref[...], sin_ref[...]
        rope = ((cos * HEAD_DIM ** -0.5, sin * HEAD_DIM ** -0.5), (cos, sin), None)
        lhs = h_ref[...]
        group_cols = HEADS_PER_GROUP * HEAD_DIM
        for kind, tables in enumerate(rope):
            for g in reversed(range(N_GROUPS)):
                h0 = kind * N_ATT_HEADS + g * HEADS_PER_GROUP
                acc = jnp.dot(lhs, w_ref[:, h0 * HEAD_DIM:h0 * HEAD_DIM + group_cols], preferred_element_type=F32)
                for hh in range(HEADS_PER_GROUP):
                    t = acc[:, hh * HEAD_DIM:(hh + 1) * HEAD_DIM]
                    if tables is not None:
                        t = t * tables[0] + pltpu.roll(t, ROPE_PAIR_LANE, 1) * tables[1]
                    _store_head(o_ref, h0 + hh, t)
        return

    (o_ref,) = refs
    acc = jnp.dot(h_ref[...], w_ref[...], preferred_element_type=F32)
    if epilogue == "gelu":
        acc = 0.5 * acc * (1.0 + lax.erf(acc * (0.5 ** 0.5)))
    elif epilogue == "sigmoid":
        acc = 0.5 * jnp.tanh(0.5 * acc) + 0.5
    o_ref[...] = acc.astype(o_ref.dtype)


def _proj(hn, w, col0, ncols, *, epilogue, tn, tm, tables=(), casts=()):
    t, d = hn.shape
    assert col0 % tn == 0 and ncols % tn == 0
    c0 = col0 // tn
    n_j = ncols // tn
    in_specs = [
        pl.BlockSpec((tm, d), lambda i, j: (i, 0)),
        pl.BlockSpec((d, tn), lambda i, j: (0, c0 + j), pipeline_mode=pl.Buffered(1) if n_j == 1 else None),
    ]
    seq_blocks = SEQ // tm
    for _ in tables:
        in_specs.append(pl.BlockSpec((tm, V7X_LANES), lambda i, j: (i % seq_blocks, 0)))
    if epilogue == "qkv":
        assert tm == QKV_TILE and tn == ncols == 3 * ATT_WIDTH
        out_spec = pl.BlockSpec((3 * N_ATT_HEADS, tm, HEAD_DIM), lambda i, j: (0, i, 0))
        out_shape = jax.ShapeDtypeStruct((3 * N_ATT_HEADS, t, HEAD_DIM), BF16)
    else:
        out_spec = pl.BlockSpec((tm, tn), lambda i, j: (i, j))
        out_shape = jax.ShapeDtypeStruct((t, ncols), BF16)
    cast_specs, cast_out_specs, cast_out_shapes = _cast_specs(casts, (t // tm) * n_j, lambda i, j: i * n_j + j)
    res = pl.pallas_call(
        functools.partial(_proj_kernel, epilogue=epilogue, cast_meta=_cast_meta(casts)),
        grid=(t // tm, n_j),
        in_specs=in_specs + cast_specs,
        out_specs=[out_spec] + cast_out_specs,
        out_shape=[out_shape] + cast_out_shapes,
        compiler_params=_params(("arbitrary", "arbitrary")),
        name="proj_" + epilogue,
    )(hn, w, *tables, *[c[0] for c in casts])
    return res if casts else res[0]


def _attn_kernel(q_ref, k_ref, v_ref, o_ref, st_ref, *, dil):
    head = pl.program_id(1)
    n_qb = SEQ // (ATT_BLOCK * dil)

    qi = lax.broadcasted_iota(jnp.int32, (ATT_BLOCK, 2 * ATT_BLOCK), 0)
    kj = lax.broadcasted_iota(jnp.int32, (ATT_BLOCK, 2 * ATT_BLOCK), 1)
    diff = qi + ATT_BLOCK - kj
    band = (diff >= 0) & (diff <= ATT_BLOCK)
    band_first = band & (kj >= ATT_BLOCK)
    lane = lax.broadcasted_iota(jnp.int32, (ATT_BLOCK, V7X_LANES), 1)
    ones = jnp.ones((2 * ATT_BLOCK, HEAD_DIM), BF16)
    zeros = jnp.zeros((ATT_BLOCK, HEAD_DIM), BF16)

    @pl.when(head == 0)
    def _():
        st_ref[...] = jnp.zeros_like(st_ref)

    rows_per = QKV_TILE // dil
    run = min(ATT_BLOCK, rows_per)

    def load(ref, qb, r):
        parts = []
        for idx in range(qb * ATT_BLOCK, (qb + 1) * ATT_BLOCK, run):
            tile, local = divmod(idx, rows_per)
            parts.append(ref[pl.ds(tile * QKV_TILE + r * rows_per + local, run), :])
        return parts[0] if len(parts) == 1 else jnp.concatenate(parts, axis=0)

    interleave = dil % 8 == 0

    def put(ref, qb, blocks, merge=None):
        if interleave:
            rows = pl.ds(qb * ATT_BLOCK * dil, ATT_BLOCK * dil)
            val = pltpu.einshape("rld->lrd", jnp.stack(blocks)).reshape(ATT_BLOCK * dil, V7X_LANES)
            ref[rows, :] = val if merge is None else merge(val, ref[rows, :])
            return
        for r, val in enumerate(blocks):
            rows = (pl.ds(qb * ATT_BLOCK * dil + r, ATT_BLOCK, stride=dil) if dil > 1
                    else pl.ds(qb * ATT_BLOCK, ATT_BLOCK))
            ref[rows, :] = val if merge is None else merge(val, ref[rows, :])

    def keep_other_heads(new, old):
        own = lax.broadcasted_iota(jnp.int32, new.shape, 1)
        return jnp.where((own == head) | (own == HEADS_PER_GROUP + head), new, old)

    k_prev, v_prev = [zeros] * dil, [zeros] * dil
    for qb in range(n_qb):
        outs, stats = [], []
        for r in range(dil):
            q, k_cur, v_cur = load(q_ref, qb, r), load(k_ref, qb, r), load(v_ref, qb, r)
            k_win = jnp.concatenate([k_prev[r], k_cur], axis=0)
            v_win = jnp.concatenate([v_prev[r], v_cur], axis=0)
            s = lax.dot_general(q, k_win, (((1,), (1,)), ((), ())), preferred_element_type=F32)
            s = jnp.where(band_first if qb == 0 else band, s, -jnp.inf)
            m = jnp.max(s, axis=-1, keepdims=True)
            p = jnp.exp(s - m).astype(BF16)
            pv = jnp.dot(p, jnp.concatenate([v_win, ones], axis=1), preferred_element_type=F32)
            l_b = pv[:, HEAD_DIM:]
            outs.append(pv[:, :HEAD_DIM] / l_b)
            stats.append(jnp.where(lane == head, m, l_b))
            k_prev[r], v_prev[r] = k_cur, v_cur
        put(o_ref, qb, outs)
        put(st_ref, qb, stats, merge=keep_other_heads)


def _attention(qkv, group, batch):
    dil = DILATIONS[group]
    t = qkv.shape[1]
    h0 = group * HEADS_PER_GROUP

    def seq(slab0):
        return pl.BlockSpec((None, SEQ, HEAD_DIM), lambda b, h: (slab0 + h, b, 0))

    return pl.pallas_call(
        functools.partial(_attn_kernel, dil=dil),
        grid=(batch, HEADS_PER_GROUP),
        in_specs=[seq(h0), seq(N_ATT_HEADS + h0), seq(2 * N_ATT_HEADS + h0)],
        out_specs=[
            pl.BlockSpec((None, SEQ, HEAD_DIM), lambda b, h: (h, b, 0)),
            pl.BlockSpec((SEQ, V7X_LANES), lambda b, h: (b, 0)),
        ],
        out_shape=[
            jax.ShapeDtypeStruct((HEADS_PER_GROUP, t, HEAD_DIM), F32),
            jax.ShapeDtypeStruct((t, V7X_LANES), F32),
        ],
        compiler_params=_params(("parallel", "arbitrary")),
        name=f"attn_d{dil}",
    )(qkv, qkv, qkv)


def _mix_prepare(o_refs, stats, u_ref, vs_ref, lng_ref, lnb_ref, sgw_ref, sgb_ref, oatt_ref, gated_ref):
    tm = oatt_ref.shape[0]
    for h in range(HEADS_PER_GROUP):
        ms = [s[:, h:h + 1] for s in stats]
        ls = [s[:, HEADS_PER_GROUP + h:HEADS_PER_GROUP + h + 1] for s in stats]
        m_all = jnp.maximum(jnp.maximum(ms[0], ms[1]), ms[2])
        ws = [l * jnp.exp(m - m_all) for l, m in zip(ls, ms)]
        num = ws[0] * o_refs[0][h] + ws[1] * o_refs[1][h] + ws[2] * o_refs[2][h]
        oatt_ref[:, h * HEAD_DIM:(h + 1) * HEAD_DIM] = (num / (ws[0] + ws[1] + ws[2])).astype(BF16)

    vs = vs_ref[...].astype(F32)
    mu = jnp.mean(vs, axis=-1, keepdims=True)
    var = jnp.mean(jnp.square(vs - mu), axis=-1, keepdims=True)
    vn = ((vs - mu) * lax.rsqrt(var + LN_EPS) * lng_ref[...] + lnb_ref[...]).astype(BF16)
    ti = lax.broadcasted_iota(jnp.int32, (SG_CHUNK, SG_CHUNK), 0)
    si = lax.broadcasted_iota(jnp.int32, (SG_CHUNK, SG_CHUNK), 1)
    causal = si <= ti
    for g in range(SG_GROUPS):
        cols = slice(g * SG_GROUP_DIM, (g + 1) * SG_GROUP_DIM)
        w_sp = jnp.where(causal, sgw_ref[g], 0.0).astype(BF16)
        bias = sgb_ref[:, g:g + 1]
        for c in range(tm // SG_CHUNK):
            rows = slice(c * SG_CHUNK, (c + 1) * SG_CHUNK)
            spatial = jnp.dot(w_sp, vn[rows, cols], preferred_element_type=F32) + bias
            gated_ref[rows, cols] = (u_ref[rows, cols].astype(F32) * spatial).astype(BF16)


def _mix_kernel(o0_ref, o1_ref, o2_ref, s0_ref, s1_ref, s2_ref, u_ref, vs_ref, ga_ref, gs_ref,
                lng_ref, lnb_ref, sgw_ref, sgb_ref, wa_ref, ws_ref, out_ref,
                oatt_a, gated_a, oatt_b, gated_b):
    i = pl.program_id(0)

    @pl.when(i == 0)
    def _():
        oatt_b[...] = jnp.zeros_like(oatt_b)
        gated_b[...] = jnp.zeros_like(gated_b)

    def step(oatt_w, gated_w, oatt_r, gated_r):
        y_att = jnp.dot(oatt_r[...], wa_ref[...], preferred_element_type=F32)
        y_sg = jnp.dot(gated_r[...], ws_ref[...], preferred_element_type=F32)
        merged = ga_ref[...].astype(F32) * y_att + gs_ref[...].astype(F32) * y_sg
        out_ref[...] = merged.astype(out_ref.dtype)
        _mix_prepare((o0_ref, o1_ref, o2_ref), (s0_ref[...], s1_ref[...], s2_ref[...]), u_ref, vs_ref,
                     lng_ref, lnb_ref, sgw_ref, sgb_ref, oatt_w, gated_w)

    @pl.when(i % 2 == 0)
    def _():
        step(oatt_a, gated_a, oatt_b, gated_b)

    @pl.when(i % 2 == 1)
    def _():
        step(oatt_b, gated_b, oatt_a, gated_a)


def _mix(o_list, st_list, uv, gates, ln_g, ln_b, sg_w, sg_b_t, w_att, w_sg, *, tm=512):
    t = uv.shape[0]
    n = t // tm
    cur = lambda i: jnp.minimum(i, n - 1)
    lag = lambda i: jnp.maximum(i - 1, 0)
    prep_row = lambda width, blk: pl.BlockSpec((tm, width), lambda i: (cur(i), blk))
    lag_row = lambda width, blk: pl.BlockSpec((tm, width), lambda i: (lag(i), blk))
    const2 = lambda shape: pl.BlockSpec(shape, lambda i: (0, 0), pipeline_mode=pl.Buffered(1))
    in_specs = (
        [pl.BlockSpec((HEADS_PER_GROUP, tm, HEAD_DIM), lambda i: (0, cur(i), 0))] * 3
        + [prep_row(V7X_LANES, 0)] * 3
        + [prep_row(SG_WIDTH, 0), prep_row(SG_WIDTH, 1), lag_row(D_MODEL, 0), lag_row(D_MODEL, 1)]
        + [const2((1, SG_WIDTH)), const2((1, SG_WIDTH)),
           pl.BlockSpec(sg_w.shape, lambda i: (0, 0, 0), pipeline_mode=pl.Buffered(1)),
           const2(sg_b_t.shape), const2(w_att.shape), const2(w_sg.shape)]
    )
    att_w = HEADS_PER_GROUP * HEAD_DIM
    return pl.pallas_call(
        _mix_kernel,
        grid=(n + 1,),
        in_specs=in_specs,
        out_specs=pl.BlockSpec((tm, D_MODEL), lambda i: (lag(i), 0)),
        out_shape=jax.ShapeDtypeStruct((t, D_MODEL), BF16),
        scratch_shapes=[pltpu.VMEM((tm, att_w), BF16), pltpu.VMEM((tm, SG_WIDTH), BF16),
                        pltpu.VMEM((tm, att_w), BF16), pltpu.VMEM((tm, SG_WIDTH), BF16)],
        compiler_params=_params(("arbitrary",)),
        name="mix",
    )(*o_list, *st_list, uv, uv, gates, gates, ln_g, ln_b, sg_w, sg_b_t, w_att, w_sg)


def _out_proj_kernel(m_ref, w_ref, x_ref, o_ref):
    o_ref[...] = x_ref[...] + jnp.dot(m_ref[...], w_ref[...], preferred_element_type=F32)


def _out_proj(merged, w, x, *, tm=1024):
    t, d = merged.shape
    n = w.shape[1]
    return pl.pallas_call(
        _out_proj_kernel,
        grid=(t // tm,),
        in_specs=[
            pl.BlockSpec((tm, d), lambda i: (i, 0)),
            pl.BlockSpec((d, n), lambda i: (0, 0), pipeline_mode=pl.Buffered(1)),
            pl.BlockSpec((tm, n), lambda i: (i, 0)),
        ],
        out_specs=pl.BlockSpec((tm, n), lambda i: (i, 0)),
        out_shape=jax.ShapeDtypeStruct((t, n), F32),
        compiler_params=_params(("parallel",)),
        name="out_proj",
    )(merged, w, x)


def kernel(x, ffn1_norm, ffn1_w_gate, ffn1_w_up, ffn1_w_down, mix_norm, w_in, sg_ln_g, sg_ln_b, sg_w, sg_b, w_att_out, w_sg_out, w_out, ffn2_norm, ffn2_w_gate, ffn2_w_up, ffn2_w_down, final_norm):
    batch, seq, d = x.shape
    assert (seq, d) == (SEQ, D_MODEL)
    t = batch * seq
    depth = ffn1_norm.shape[0]
    xt = x.reshape(t, d)
    tables = _rope_tables()
    bf = lambda a: a.astype(BF16)
    for i in range(depth):
        last = i == depth - 1
        c_v, c_u, c_g = 2 * ATT_WIDTH, 3 * ATT_WIDTH, 3 * ATT_WIDTH + 2 * SG_WIDTH
        whole = lambda w: (w, (0, w.shape[1]), 0)
        x1, hn, w_in_b, w_gates_b = _ffn(
            xt, ffn1_norm[i][None], bf(ffn1_w_gate[i]), bf(ffn1_w_up[i]), bf(ffn1_w_down[i]),
            mix_norm[i][None], post="hn", casts=((w_in[i], (0, c_g, w_in.shape[2]), c_v),))
        qkv, w_out_b, w_sg_b, w_att_b = _proj(
            hn, w_in_b, 0, c_u, epilogue="qkv", tn=c_u, tm=QKV_TILE, tables=tables,
            casts=(whole(w_out[i]), whole(w_sg_out[i]), whole(w_att_out[i])))
        uv, w2_gate, w2_up = _proj(hn, w_in_b, c_u, 2 * SG_WIDTH, epilogue="gelu", tn=SG_WIDTH, tm=1024,
                                   casts=(whole(ffn2_w_gate[i]), whole(ffn2_w_up[i])))
        gates, w2_down = _proj(hn, w_gates_b, 0, 2 * D_MODEL, epilogue="sigmoid", tn=D_MODEL, tm=1024,
                               casts=(whole(ffn2_w_down[i]),))
        o_list, st_list = [], []
        for g in range(N_GROUPS):
            o_g, st_g = _attention(qkv, g, batch)
            o_list.append(o_g)
            st_list.append(st_g)
        merged = _mix(o_list, st_list, uv, gates, sg_ln_g[i][None], sg_ln_b[i][None], sg_w[i], sg_b[i].T,
                      w_att_b, w_sg_b)
        x2 = _out_proj(merged, w_out_b, x1)
        (xt,) = _ffn(x2, ffn2_norm[i][None], w2_gate, w2_up, w2_down,
                     final_norm[None], post="final" if last else "plain")
    return xt.reshape(batch, seq, d)
```

```python
import functools

import jax
import jax.numpy as jnp
from jax import lax
from jax.experimental import pallas as pl
from jax.experimental.pallas import tpu as pltpu

F32 = jnp.float32
BF16 = jnp.bfloat16

D_MODEL = 2048
SEQ = 4096
HEAD_DIM = 128
HEADS_PER_GROUP = 4
DILATIONS = (1, 4, 16)
N_GROUPS = len(DILATIONS)
N_ATT_HEADS = N_GROUPS * HEADS_PER_GROUP
ATT_WIDTH = N_ATT_HEADS * HEAD_DIM
ATT_BLOCK = 128
QKV_TILE = 512
ROPE_DIM = HEAD_DIM // 4
ROPE_HALF = ROPE_DIM // 2
ROPE_THETA = 500000.0
SG_CHUNK = 128
SG_GROUPS = 12
SG_GROUP_DIM = 128
SG_WIDTH = SG_GROUPS * SG_GROUP_DIM
NORM_EPS = 1e-6
LN_EPS = 1e-5

V7X_VMEM_BYTES = 64 * 1024 * 1024
V7X_LANES = 128
BF16_ROWS = 16
CAST_STEPS = 128
VMEM_LIMIT = 52 * 1024 * 1024


def _params(semantics, vmem_limit=VMEM_LIMIT):
    return pltpu.CompilerParams(dimension_semantics=semantics, vmem_limit_bytes=vmem_limit)


def _rms(x, g):
    return x * lax.rsqrt(jnp.mean(x * x, axis=-1, keepdims=True) + NORM_EPS) * g


ROPE_PAIR_LANE = V7X_LANES // 2


def _rope_lane_masks(lane):
    first = lane < ROPE_HALF
    second = (lane >= ROPE_PAIR_LANE) & (lane < ROPE_PAIR_LANE + ROPE_HALF)
    return first, second


def _rope_lane_layout(x):
    cols = x.shape[1]
    lane = lax.broadcasted_iota(jnp.int32, x.shape, 1) % V7X_LANES
    shift = ROPE_PAIR_LANE - ROPE_HALF
    to_pair_lane = (lane >= ROPE_PAIR_LANE) & (lane < ROPE_PAIR_LANE + ROPE_HALF)
    from_pair_lane = (lane >= ROPE_HALF) & (lane < ROPE_DIM)
    return jnp.where(to_pair_lane, pltpu.roll(x, shift, 1),
                     jnp.where(from_pair_lane, pltpu.roll(x, cols - shift, 1), x))


def _rope_table_kernel(freq_ref, cos_ref, sin_ref):
    rows = cos_ref.shape[0]
    pos = (pl.program_id(0) * rows + lax.broadcasted_iota(jnp.int32, (rows, V7X_LANES), 0)).astype(F32)
    first, second = _rope_lane_masks(lax.broadcasted_iota(jnp.int32, (rows, V7X_LANES), 1))
    ang = pos * freq_ref[...]
    c = jnp.cos(ang)
    s = jnp.sin(ang)
    cos_ref[...] = jnp.where(first | second, c, 1.0)
    sin_ref[...] = jnp.where(first, -s, jnp.where(second, s, 0.0))


def _rope_tables():
    inv_freq = ROPE_THETA ** (-jnp.arange(0, ROPE_DIM, 2, dtype=F32) / ROPE_DIM)
    gap = jnp.zeros((ROPE_PAIR_LANE - ROPE_HALF,), F32)
    freq_lane = jnp.concatenate([inv_freq, gap, inv_freq, gap])[None, :]
    rows = 512
    tab = jax.ShapeDtypeStruct((SEQ, V7X_LANES), F32)
    spec = pl.BlockSpec((rows, V7X_LANES), lambda i: (i, 0))
    return pl.pallas_call(
        _rope_table_kernel,
        grid=(SEQ // rows,),
        in_specs=[pl.BlockSpec((1, V7X_LANES), lambda i: (0, 0))],
        out_specs=[spec, spec],
        out_shape=[tab, tab],
        compiler_params=_params(("parallel",)),
        name="rope_table",
    )(freq_lane)


def _cast_specs(casts, n_steps, step_of):
    in_specs, out_specs, out_shapes = [], [], []
    for w, splits, _ in casts:
        rows, cols = w.shape
        assert splits[0] == 0 and splits[-1] == cols
        rb = next(r for r in range(BF16_ROWS, rows + 1, BF16_ROWS) if rows % r == 0 and rows // r <= n_steps)
        n_blocks = rows // rb
        block_of = lambda *idx, n_blocks=n_blocks: (jnp.minimum(step_of(*idx), n_blocks - 1), 0)
        in_specs.append(pl.BlockSpec((rb, cols), block_of))
        for c0, c1 in zip(splits[:-1], splits[1:]):
            out_specs.append(pl.BlockSpec((rb, c1 - c0), block_of))
            out_shapes.append(jax.ShapeDtypeStruct((rows, c1 - c0), BF16))
    return in_specs, out_specs, out_shapes


def _cast_blocks(cast_in, cast_out, cast_meta):
    cast_out = iter(cast_out)
    for src, (splits, rope_cols) in zip(cast_in, cast_meta):
        for c0, c1 in zip(splits[:-1], splits[1:]):
            dst = next(cast_out)
            n_rope = min(max(rope_cols - c0, 0), c1 - c0)
            if n_rope:
                dst[:, :n_rope] = _rope_lane_layout(src[:, c0:c0 + n_rope]).astype(BF16)
            if n_rope < c1 - c0:
                dst[:, n_rope:] = src[:, c0 + n_rope:c1].astype(BF16)


def _cast_meta(casts):
    return tuple((splits, rope_cols) for _, splits, rope_cols in casts)


def _ffn_kernel(x_ref, g_ref, wg_ref, wu_ref, wd_ref, pg_ref, *refs, post, cast_meta):
    n_cast = len(cast_meta)
    cast_in, refs = refs[:n_cast], refs[n_cast:]
    if post == "hn":
        o_ref, hn_ref, *cast_out, xn_ref = refs
    else:
        o_ref, *cast_out, xn_ref = refs
    j = pl.program_id(1)
    last = pl.num_programs(1) - 1

    def tile_contribution(xn):
        _cast_blocks(cast_in, cast_out, cast_meta)
        h = jnp.dot(xn, wg_ref[...], preferred_element_type=F32)
        u = jnp.dot(xn, wu_ref[...], preferred_element_type=F32)
        a = ((0.5 * h) * (1.0 + jnp.tanh(0.5 * h)) * u).astype(BF16)
        return jnp.dot(a, wd_ref[...].astype(BF16), preferred_element_type=F32)

    @pl.when(j == 0)
    def _():
        xn = _rms(x_ref[...], g_ref[...]).astype(BF16)
        xn_ref[...] = xn
        o_ref[...] = tile_contribution(xn)

    @pl.when((j > 0) & (j < last))
    def _():
        o_ref[...] += tile_contribution(xn_ref[...])

    @pl.when(j == last)
    def _():
        y = x_ref[...] + 0.5 * (o_ref[...] + tile_contribution(xn_ref[...]))
        if post == "hn":
            o_ref[...] = y
            hn_ref[...] = _rms(y, pg_ref[...]).astype(BF16)
        elif post == "final":
            o_ref[...] = _rms(y, pg_ref[...])
        else:
            o_ref[...] = y


def _ffn(x, g, wg, wu, wd, post_g, *, post, casts=(), tm=512, tf=512):
    t, d = x.shape
    f = wg.shape[1]
    n_j = f // tf
    row = pl.BlockSpec((tm, d), lambda i, j: (i, 0))
    vec = pl.BlockSpec((1, d), lambda i, j: (0, 0))
    out_shape = [jax.ShapeDtypeStruct((t, d), F32)]
    out_specs = [row]
    if post == "hn":
        out_shape.append(jax.ShapeDtypeStruct((t, d), BF16))
        out_specs.append(row)
    cast_specs, cast_out_specs, cast_out_shapes = _cast_specs(
        casts, min(CAST_STEPS, (t // tm) * n_j), lambda i, j: i * n_j + j)
    out_specs += cast_out_specs
    out_shape += cast_out_shapes
    return pl.pallas_call(
        functools.partial(_ffn_kernel, post=post, cast_meta=_cast_meta(casts)),
        grid=(t // tm, n_j),
        in_specs=[
            row,
            vec,
            pl.BlockSpec((d, tf), lambda i, j: (0, j)),
            pl.BlockSpec((d, tf), lambda i, j: (0, j)),
            pl.BlockSpec((tf, d), lambda i, j: (j, 0)),
            vec,
        ] + cast_specs,
        out_specs=out_specs,
        out_shape=out_shape,
        scratch_shapes=[pltpu.VMEM((tm, d), BF16)],
        compiler_params=_params(("arbitrary", "arbitrary")),
        name="ffn_" + post,
    )(x, g, wg, wu, wd, post_g, *[c[0] for c in casts])


def _store_head(o_ref, h, val):
    dil = DILATIONS[h % N_ATT_HEADS // HEADS_PER_GROUP]
    if dil == 1:
        o_ref[h] = val.astype(BF16)
        return
    regrouped = pltpu.einshape("lrd->rld", val.reshape(QKV_TILE // dil, dil, HEAD_DIM))
    o_ref[h] = regrouped.reshape(QKV_TILE, HEAD_DIM).astype(BF16)


def _proj_kernel(h_ref, w_ref, *refs, epilogue, cast_meta):
    n_cast = len(cast_meta)
    n_cast_out = sum(len(splits) - 1 for splits, _ in cast_meta)
    n_tables = 2 if epilogue == "qkv" else 0
    cast_in = refs[n_tables:n_tables + n_cast]
    cast_out = refs[n_tables + n_cast + 1:n_tables + n_cast + 1 + n_cast_out]
    refs = refs[:n_tables] + refs[n_tables + n_cast:n_tables + n_cast + 1] + refs[n_tables + n_cast + 1 + n_cast_out:]
    _cast_blocks(cast_in, cast_out, cast_meta)

    if epilogue == "qkv":
        cos_ref, sin_ref, o_ref = refs
        cos, sin = cos_ref[...], sin_ref[...]
        rope = ((cos * HEAD_DIM ** -0.5, sin * HEAD_DIM ** -0.5), (cos, sin), None)
        lhs = h_ref[...]
        group_cols = HEADS_PER_GROUP * HEAD_DIM
        for kind, tables in enumerate(rope):
            for g in reversed(range(N_GROUPS)):
                h0 = kind * N_ATT_HEADS + g * HEADS_PER_GROUP
                acc = jnp.dot(lhs, w_ref[:, h0 * HEAD_DIM:h0 * HEAD_DIM + group_cols], preferred_element_type=F32)
                for hh in range(HEADS_PER_GROUP):
                    t = acc[:, hh * HEAD_DIM:(hh + 1) * HEAD_DIM]
                    if tables is not None:
                        t = t * tables[0] + pltpu.roll(t, ROPE_PAIR_LANE, 1) * tables[1]
                    _store_head(o_ref, h0 + hh, t)
        return

    (o_ref,) = refs
    acc = jnp.dot(h_ref[...], w_ref[...], preferred_element_type=F32)
    if epilogue == "gelu":
        acc = 0.5 * acc * (1.0 + lax.erf(acc * (0.5 ** 0.5)))
    elif epilogue == "sigmoid":
        acc = 0.5 * jnp.tanh(0.5 * acc) + 0.5
    o_ref[...] = acc.astype(o_ref.dtype)


def _proj(hn, w, col0, ncols, *, epilogue, tn, tm, tables=(), casts=()):
    t, d = hn.shape
    assert col0 % tn == 0 and ncols % tn == 0
    c0 = col0 // tn
    n_j = ncols // tn
    in_specs = [
        pl.BlockSpec((tm, d), lambda i, j: (i, 0)),
        pl.BlockSpec((d, tn), lambda i, j: (0, c0 + j), pipeline_mode=pl.Buffered(1) if n_j == 1 else None),
    ]
    seq_blocks = SEQ // tm
    for _ in tables:
        in_specs.append(pl.BlockSpec((tm, V7X_LANES), lambda i, j: (i % seq_blocks, 0)))
    if epilogue == "qkv":
        assert tm == QKV_TILE and tn == ncols == 3 * ATT_WIDTH
        out_spec = pl.BlockSpec((3 * N_ATT_HEADS, tm, HEAD_DIM), lambda i, j: (0, i, 0))
        out_shape = jax.ShapeDtypeStruct((3 * N_ATT_HEADS, t, HEAD_DIM), BF16)
    else:
        out_spec = pl.BlockSpec((tm, tn), lambda i, j: (i, j))
        out_shape = jax.ShapeDtypeStruct((t, ncols), BF16)
    cast_specs, cast_out_specs, cast_out_shapes = _cast_specs(casts, (t // tm) * n_j, lambda i, j: i * n_j + j)
    res = pl.pallas_call(
        functools.partial(_proj_kernel, epilogue=epilogue, cast_meta=_cast_meta(casts)),
        grid=(t // tm, n_j),
        in_specs=in_specs + cast_specs,
        out_specs=[out_spec] + cast_out_specs,
        out_shape=[out_shape] + cast_out_shapes,
        compiler_params=_params(("arbitrary", "arbitrary")),
        name="proj_" + epilogue,
    )(hn, w, *tables, *[c[0] for c in casts])
    return res if casts else res[0]


def _attn_kernel(q_ref, k_ref, v_ref, o_ref, st_ref, *, dil):
    head = pl.program_id(1)
    n_qb = SEQ // (ATT_BLOCK * dil)

    qi = lax.broadcasted_iota(jnp.int32, (ATT_BLOCK, 2 * ATT_BLOCK), 0)
    kj = lax.broadcasted_iota(jnp.int32, (ATT_BLOCK, 2 * ATT_BLOCK), 1)
    diff = qi + ATT_BLOCK - kj
    band = (diff >= 0) & (diff <= ATT_BLOCK)
    band_first = band & (kj >= ATT_BLOCK)
    lane = lax.broadcasted_iota(jnp.int32, (ATT_BLOCK, V7X_LANES), 1)
    ones = jnp.ones((2 * ATT_BLOCK, HEAD_DIM), BF16)
    zeros = jnp.zeros((ATT_BLOCK, HEAD_DIM), BF16)

    @pl.when(head == 0)
    def _():
        st_ref[...] = jnp.zeros_like(st_ref)

    rows_per = QKV_TILE // dil
    run = min(ATT_BLOCK, rows_per)

    def load(ref, qb, r):
        parts = []
        for idx in range(qb * ATT_BLOCK, (qb + 1) * ATT_BLOCK, run):
            tile, local = divmod(idx, rows_per)
            parts.append(ref[pl.ds(tile * QKV_TILE + r * rows_per + local, run), :])
        return parts[0] if len(parts) == 1 else jnp.concatenate(parts, axis=0)

    interleave = dil % 8 == 0

    def put(ref, qb, blocks, merge=None):
        if interleave:
            rows = pl.ds(qb * ATT_BLOCK * dil, ATT_BLOCK * dil)
            val = pltpu.einshape("rld->lrd", jnp.stack(blocks)).reshape(ATT_BLOCK * dil, V7X_LANES)
            ref[rows, :] = val if merge is None else merge(val, ref[rows, :])
            return
        for r, val in enumerate(blocks):
            rows = (pl.ds(qb * ATT_BLOCK * dil + r, ATT_BLOCK, stride=dil) if dil > 1
                    else pl.ds(qb * ATT_BLOCK, ATT_BLOCK))
            ref[rows, :] = val if merge is None else merge(val, ref[rows, :])

    def keep_other_heads(new, old):
        own = lax.broadcasted_iota(jnp.int32, new.shape, 1)
        return jnp.where((own == head) | (own == HEADS_PER_GROUP + head), new, old)

    k_prev, v_prev = [zeros] * dil, [zeros] * dil
    for qb in range(n_qb):
        outs, stats = [], []
        for r in range(dil):
            q, k_cur, v_cur = load(q_ref, qb, r), load(k_ref, qb, r), load(v_ref, qb, r)
            k_win = jnp.concatenate([k_prev[r], k_cur], axis=0)
            v_win = jnp.concatenate([v_prev[r], v_cur], axis=0)
            s = lax.dot_general(q, k_win, (((1,), (1,)), ((), ())), preferred_element_type=F32)
            s = jnp.where(band_first if qb == 0 else band, s, -jnp.inf)
            m = jnp.max(s, axis=-1, keepdims=True)
            p = jnp.exp(s - m).astype(BF16)
            pv = jnp.dot(p, jnp.concatenate([v_win, ones], axis=1), preferred_element_type=F32)
            l_b = pv[:, HEAD_DIM:]
            outs.append(pv[:, :HEAD_DIM] / l_b)
            stats.append(jnp.where(lane == head, m, l_b))
            k_prev[r], v_prev[r] = k_cur, v_cur
        put(o_ref, qb, outs)
        put(st_ref, qb, stats, merge=keep_other_heads)


def _attention(qkv, group, batch):
    dil = DILATIONS[group]
    t = qkv.shape[1]
    h0 = group * HEADS_PER_GROUP

    def seq(slab0):
        return pl.BlockSpec((None, SEQ, HEAD_DIM), lambda b, h: (slab0 + h, b, 0))

    return pl.pallas_call(
        functools.partial(_attn_kernel, dil=dil),
        grid=(batch, HEADS_PER_GROUP),
        in_specs=[seq(h0), seq(N_ATT_HEADS + h0), seq(2 * N_ATT_HEADS + h0)],
        out_specs=[
            pl.BlockSpec((None, SEQ, HEAD_DIM), lambda b, h: (h, b, 0)),
            pl.BlockSpec((SEQ, V7X_LANES), lambda b, h: (b, 0)),
        ],
        out_shape=[
            jax.ShapeDtypeStruct((HEADS_PER_GROUP, t, HEAD_DIM), F32),
            jax.ShapeDtypeStruct((t, V7X_LANES), F32),
        ],
        compiler_params=_params(("parallel", "arbitrary")),
        name=f"attn_d{dil}",
    )(qkv, qkv, qkv)


def _mix_prepare(o_refs, stats, u_ref, vs_ref, lng_ref, lnb_ref, sgw_ref, sgb_ref, oatt_ref, gated_ref):
    tm = oatt_ref.shape[0]
    for h in range(HEADS_PER_GROUP):
        ms = [s[:, h:h + 1] for s in stats]
        ls = [s[:, HEADS_PER_GROUP + h:HEADS_PER_GROUP + h + 1] for s in stats]
        m_all = jnp.maximum(jnp.maximum(ms[0], ms[1]), ms[2])
        ws = [l * jnp.exp(m - m_all) for l, m in zip(ls, ms)]
        num = ws[0] * o_refs[0][h] + ws[1] * o_refs[1][h] + ws[2] * o_refs[2][h]
        oatt_ref[:, h * HEAD_DIM:(h + 1) * HEAD_DIM] = (num / (ws[0] + ws[1] + ws[2])).astype(BF16)

    vs = vs_ref[...].astype(F32)
    mu = jnp.mean(vs, axis=-1, keepdims=True)
    var = jnp.mean(jnp.square(vs - mu), axis=-1, keepdims=True)
    vn = ((vs - mu) * lax.rsqrt(var + LN_EPS) * lng_ref[...] + lnb_ref[...]).astype(BF16)
    ti = lax.broadcasted_iota(jnp.int32, (SG_CHUNK, SG_CHUNK), 0)
    si = lax.broadcasted_iota(jnp.int32, (SG_CHUNK, SG_CHUNK), 1)
    causal = si <= ti
    for g in range(SG_GROUPS):
        cols = slice(g * SG_GROUP_DIM, (g + 1) * SG_GROUP_DIM)
        w_sp = jnp.where(causal, sgw_ref[g], 0.0).astype(BF16)
        bias = sgb_ref[:, g:g + 1]
        for c in range(tm // SG_CHUNK):
            rows = slice(c * SG_CHUNK, (c + 1) * SG_CHUNK)
            spatial = jnp.dot(w_sp, vn[rows, cols], preferred_element_type=F32) + bias
            gated_ref[rows, cols] = (u_ref[rows, cols].astype(F32) * spatial).astype(BF16)


def _mix_kernel(o0_ref, o1_ref, o2_ref, s0_ref, s1_ref, s2_ref, u_ref, vs_ref, ga_ref, gs_ref,
                lng_ref, lnb_ref, sgw_ref, sgb_ref, wa_ref, ws_ref, out_ref,
                oatt_a, gated_a, oatt_b, gated_b):
    i = pl.program_id(0)

    @pl.when(i == 0)
    def _():
        oatt_b[...] = jnp.zeros_like(oatt_b)
        gated_b[...] = jnp.zeros_like(gated_b)

    def step(oatt_w, gated_w, oatt_r, gated_r):
        y_att = jnp.dot(oatt_r[...], wa_ref[...], preferred_element_type=F32)
        y_sg = jnp.dot(gated_r[...], ws_ref[...], preferred_element_type=F32)
        merged = ga_ref[...].astype(F32) * y_att + gs_ref[...].astype(F32) * y_sg
        out_ref[...] = merged.astype(out_ref.dtype)
        _mix_prepare((o0_ref, o1_ref, o2_ref), (s0_ref[...], s1_ref[...], s2_ref[...]), u_ref, vs_ref,
                     lng_ref, lnb_ref, sgw_ref, sgb_ref, oatt_w, gated_w)

    @pl.when(i % 2 == 0)
    def _():
        step(oatt_a, gated_a, oatt_b, gated_b)

    @pl.when(i % 2 == 1)
    def _():
        step(oatt_b, gated_b, oatt_a, gated_a)


def _mix(o_list, st_list, uv, gates, ln_g, ln_b, sg_w, sg_b_t, w_att, w_sg, *, tm=512):
    t = uv.shape[0]
    n = t // tm
    cur = lambda i: jnp.minimum(i, n - 1)
    lag = lambda i: jnp.maximum(i - 1, 0)
    prep_row = lambda width, blk: pl.BlockSpec((tm, width), lambda i: (cur(i), blk))
    lag_row = lambda width, blk: pl.BlockSpec((tm, width), lambda i: (lag(i), blk))
    const2 = lambda shape: pl.BlockSpec(shape, lambda i: (0, 0), pipeline_mode=pl.Buffered(1))
    in_specs = (
        [pl.BlockSpec((HEADS_PER_GROUP, tm, HEAD_DIM), lambda i: (0, cur(i), 0))] * 3
        + [prep_row(V7X_LANES, 0)] * 3
        + [prep_row(SG_WIDTH, 0), prep_row(SG_WIDTH, 1), lag_row(D_MODEL, 0), lag_row(D_MODEL, 1)]
        + [const2((1, SG_WIDTH)), const2((1, SG_WIDTH)),
           pl.BlockSpec(sg_w.shape, lambda i: (0, 0, 0), pipeline_mode=pl.Buffered(1)),
           const2(sg_b_t.shape), const2(w_att.shape), const2(w_sg.shape)]
    )
    att_w = HEADS_PER_GROUP * HEAD_DIM
    return pl.pallas_call(
        _mix_kernel,
        grid=(n + 1,),
        in_specs=in_specs,
        out_specs=pl.BlockSpec((tm, D_MODEL), lambda i: (lag(i), 0)),
        out_shape=jax.ShapeDtypeStruct((t, D_MODEL), BF16),
        scratch_shapes=[pltpu.VMEM((tm, att_w), BF16), pltpu.VMEM((tm, SG_WIDTH), BF16),
                        pltpu.VMEM((tm, att_w), BF16), pltpu.VMEM((tm, SG_WIDTH), BF16)],
        compiler_params=_params(("arbitrary",)),
        name="mix",
    )(*o_list, *st_list, uv, uv, gates, gates, ln_g, ln_b, sg_w, sg_b_t, w_att, w_sg)


def _out_proj_kernel(m_ref, w_ref, x_ref, o_ref):
    o_ref[...] = x_ref[...] + jnp.dot(m_ref[...], w_ref[...], preferred_element_type=F32)


def _out_proj(merged, w, x, *, tm=1024):
    t, d = merged.shape
    n = w.shape[1]
    return pl.pallas_call(
        _out_proj_kernel,
        grid=(t // tm,),
        in_specs=[
            pl.BlockSpec((tm, d), lambda i: (i, 0)),
            pl.BlockSpec((d, n), lambda i: (0, 0), pipeline_mode=pl.Buffered(1)),
            pl.BlockSpec((tm, n), lambda i: (i, 0)),
        ],
        out_specs=pl.BlockSpec((tm, n), lambda i: (i, 0)),
        out_shape=jax.ShapeDtypeStruct((t, n), F32),
        compiler_params=_params(("parallel",)),
        name="out_proj",
    )(merged, w, x)


def kernel(x, ffn1_norm, ffn1_w_gate, ffn1_w_up, ffn1_w_down, mix_norm, w_in, sg_ln_g, sg_ln_b, sg_w, sg_b, w_att_out, w_sg_out, w_out, ffn2_norm, ffn2_w_gate, ffn2_w_up, ffn2_w_down, final_norm):
    batch, seq, d = x.shape
    assert (seq, d) == (SEQ, D_MODEL)
    t = batch * seq
    depth = ffn1_norm.shape[0]
    xt = x.reshape(t, d)
    tables = _rope_tables()
    bf = lambda a: a.astype(BF16)
    for i in range(depth):
        last = i == depth - 1
        c_v, c_u, c_g = 2 * ATT_WIDTH, 3 * ATT_WIDTH, 3 * ATT_WIDTH + 2 * SG_WIDTH
        whole = lambda w: (w, (0, w.shape[1]), 0)
        x1, hn, w_in_b, w_gates_b = _ffn(
            xt, ffn1_norm[i][None], bf(ffn1_w_gate[i]), bf(ffn1_w_up[i]), ffn1_w_down[i],
            mix_norm[i][None], post="hn", casts=((w_in[i], (0, c_g, w_in.shape[2]), c_v),))
        qkv, w_out_b, w_sg_b, w_att_b = _proj(
            hn, w_in_b, 0, c_u, epilogue="qkv", tn=c_u, tm=QKV_TILE, tables=tables,
            casts=(whole(w_out[i]), whole(w_sg_out[i]), whole(w_att_out[i])))
        uv, w2_gate, w2_up = _proj(hn, w_in_b, c_u, 2 * SG_WIDTH, epilogue="gelu", tn=SG_WIDTH, tm=1024,
                                   casts=(whole(ffn2_w_gate[i]), whole(ffn2_w_up[i])))
        gates, w2_down = _proj(hn, w_gates_b, 0, 2 * D_MODEL, epilogue="sigmoid", tn=D_MODEL, tm=1024,
                               casts=(whole(ffn2_w_down[i]),))
        o_list, st_list = [], []
        for g in range(N_GROUPS):
            o_g, st_g = _attention(qkv, g, batch)
            o_list.append(o_g)
            st_list.append(st_g)
        merged = _mix(o_list, st_list, uv, gates, sg_ln_g[i][None], sg_ln_b[i][None], sg_w[i], sg_b[i].T,
                      w_att_b, w_sg_b)
        x2 = _out_proj(merged, w_out_b, x1)
        (xt,) = _ffn(x2, ffn2_norm[i][None], w2_gate, w2_up, w2_down,
                     final_norm[None], post="final" if last else "plain")
    return xt.reshape(batch, seq, d)
```

```python
import functools

import jax
import jax.numpy as jnp
from jax import lax
from jax.experimental import pallas as pl
from jax.experimental.pallas import tpu as pltpu

F32 = jnp.float32
BF16 = jnp.bfloat16

D_MODEL = 2048
SEQ = 4096
HEAD_DIM = 128
HEADS_PER_GROUP = 4
DILATIONS = (1, 4, 16)
N_GROUPS = len(DILATIONS)
N_ATT_HEADS = N_GROUPS * HEADS_PER_GROUP
ATT_WIDTH = N_ATT_HEADS * HEAD_DIM
ATT_BLOCK = 128
QKV_TILE = 512
ROPE_DIM = HEAD_DIM // 4
ROPE_HALF = ROPE_DIM // 2
ROPE_THETA = 500000.0
SG_CHUNK = 128
SG_GROUPS = 12
SG_GROUP_DIM = 128
SG_WIDTH = SG_GROUPS * SG_GROUP_DIM
NORM_EPS = 1e-6
LN_EPS = 1e-5

V7X_VMEM_BYTES = 64 * 1024 * 1024
V7X_LANES = 128
BF16_ROWS = 16
CAST_STEPS = 128
VMEM_LIMIT = V7X_VMEM_BYTES - 12 * 1024 * 1024


def _params(semantics, vmem_limit=VMEM_LIMIT):
    return pltpu.CompilerParams(dimension_semantics=semantics, vmem_limit_bytes=vmem_limit)


def _rms(x, g):
    return x * lax.rsqrt(jnp.mean(x * x, axis=-1, keepdims=True) + NORM_EPS) * g


ROPE_PAIR_LANE = V7X_LANES // 2


def _rope_lane_masks(lane):
    first = lane < ROPE_HALF
    second = (lane >= ROPE_PAIR_LANE) & (lane < ROPE_PAIR_LANE + ROPE_HALF)
    return first, second


def _rope_lane_layout(x):
    cols = x.shape[1]
    lane = lax.broadcasted_iota(jnp.int32, x.shape, 1) % V7X_LANES
    shift = ROPE_PAIR_LANE - ROPE_HALF
    to_pair_lane = (lane >= ROPE_PAIR_LANE) & (lane < ROPE_PAIR_LANE + ROPE_HALF)
    from_pair_lane = (lane >= ROPE_HALF) & (lane < ROPE_DIM)
    return jnp.where(to_pair_lane, pltpu.roll(x, shift, 1),
                     jnp.where(from_pair_lane, pltpu.roll(x, cols - shift, 1), x))


def _rope_table_kernel(freq_ref, cos_ref, sin_ref):
    rows = cos_ref.shape[0]
    pos = (pl.program_id(0) * rows + lax.broadcasted_iota(jnp.int32, (rows, V7X_LANES), 0)).astype(F32)
    first, second = _rope_lane_masks(lax.broadcasted_iota(jnp.int32, (rows, V7X_LANES), 1))
    ang = pos * freq_ref[...]
    c = jnp.cos(ang)
    s = jnp.sin(ang)
    cos_ref[...] = jnp.where(first | second, c, 1.0)
    sin_ref[...] = jnp.where(first, -s, jnp.where(second, s, 0.0))


def _rope_tables():
    inv_freq = ROPE_THETA ** (-jnp.arange(0, ROPE_DIM, 2, dtype=F32) / ROPE_DIM)
    gap = jnp.zeros((ROPE_PAIR_LANE - ROPE_HALF,), F32)
    freq_lane = jnp.concatenate([inv_freq, gap, inv_freq, gap])[None, :]
    rows = 512
    tab = jax.ShapeDtypeStruct((SEQ, V7X_LANES), F32)
    spec = pl.BlockSpec((rows, V7X_LANES), lambda i: (i, 0))
    return pl.pallas_call(
        _rope_table_kernel,
        grid=(SEQ // rows,),
        in_specs=[pl.BlockSpec((1, V7X_LANES), lambda i: (0, 0))],
        out_specs=[spec, spec],
        out_shape=[tab, tab],
        compiler_params=_params(("parallel",)),
        name="rope_table",
    )(freq_lane)


def _cast_specs(casts, n_steps, step_of):
    in_specs, out_specs, out_shapes = [], [], []
    for w, splits, _ in casts:
        rows, cols = w.shape
        assert splits[0] == 0 and splits[-1] == cols
        rb = next(r for r in range(BF16_ROWS, rows + 1, BF16_ROWS) if rows % r == 0 and rows // r <= n_steps)
        n_blocks = rows // rb
        block_of = lambda *idx, n_blocks=n_blocks: (jnp.minimum(step_of(*idx), n_blocks - 1), 0)
        in_specs.append(pl.BlockSpec((rb, cols), block_of))
        for c0, c1 in zip(splits[:-1], splits[1:]):
            out_specs.append(pl.BlockSpec((rb, c1 - c0), block_of))
            out_shapes.append(jax.ShapeDtypeStruct((rows, c1 - c0), BF16))
    return in_specs, out_specs, out_shapes


def _cast_blocks(cast_in, cast_out, cast_meta):
    cast_out = iter(cast_out)
    for src, (splits, rope_cols) in zip(cast_in, cast_meta):
        for c0, c1 in zip(splits[:-1], splits[1:]):
            dst = next(cast_out)
            n_rope = min(max(rope_cols - c0, 0), c1 - c0)
            if n_rope:
                dst[:, :n_rope] = _rope_lane_layout(src[:, c0:c0 + n_rope]).astype(BF16)
            if n_rope < c1 - c0:
                dst[:, n_rope:] = src[:, c0 + n_rope:c1].astype(BF16)


def _cast_meta(casts):
    return tuple((splits, rope_cols) for _, splits, rope_cols in casts)


def _ffn_kernel(x_ref, g_ref, wg_ref, wu_ref, wd_ref, pg_ref, *refs, post, cast_meta):
    n_cast = len(cast_meta)
    cast_in, refs = refs[:n_cast], refs[n_cast:]
    if post == "hn":
        o_ref, hn_ref, *cast_out, xn_ref = refs
    else:
        o_ref, *cast_out, xn_ref = refs
    j = pl.program_id(1)
    last = pl.num_programs(1) - 1

    def tile_contribution(xn):
        _cast_blocks(cast_in, cast_out, cast_meta)
        h = jnp.dot(xn, wg_ref[...], preferred_element_type=F32)
        u = jnp.dot(xn, wu_ref[...], preferred_element_type=F32)
        a = ((0.5 * h) * (1.0 + jnp.tanh(0.5 * h)) * u).astype(BF16)
        return jnp.dot(a, wd_ref[...].astype(BF16), preferred_element_type=F32)

    @pl.when(j == 0)
    def _():
        xn = _rms(x_ref[...], g_ref[...]).astype(BF16)
        xn_ref[...] = xn
        o_ref[...] = tile_contribution(xn)

    @pl.when((j > 0) & (j < last))
    def _():
        o_ref[...] += tile_contribution(xn_ref[...])

    @pl.when(j == last)
    def _():
        y = x_ref[...] + 0.5 * (o_ref[...] + tile_contribution(xn_ref[...]))
        if post == "hn":
            o_ref[...] = y
            hn_ref[...] = _rms(y, pg_ref[...]).astype(BF16)
        elif post == "final":
            o_ref[...] = _rms(y, pg_ref[...])
        else:
            o_ref[...] = y


def _ffn(x, g, wg, wu, wd, post_g, *, post, casts=(), tm=512, tf=512):
    t, d = x.shape
    f = wg.shape[1]
    n_j = f // tf
    row = pl.BlockSpec((tm, d), lambda i, j: (i, 0))
    vec = pl.BlockSpec((1, d), lambda i, j: (0, 0))
    out_shape = [jax.ShapeDtypeStruct((t, d), F32)]
    out_specs = [row]
    if post == "hn":
        out_shape.append(jax.ShapeDtypeStruct((t, d), BF16))
        out_specs.append(row)
    cast_specs, cast_out_specs, cast_out_shapes = _cast_specs(
        casts, min(CAST_STEPS, (t // tm) * n_j), lambda i, j: i * n_j + j)
    out_specs += cast_out_specs
    out_shape += cast_out_shapes
    return pl.pallas_call(
        functools.partial(_ffn_kernel, post=post, cast_meta=_cast_meta(casts)),
        grid=(t // tm, n_j),
        in_specs=[
            row,
            vec,
            pl.BlockSpec((d, tf), lambda i, j: (0, j)),
            pl.BlockSpec((d, tf), lambda i, j: (0, j)),
            pl.BlockSpec((tf, d), lambda i, j: (j, 0)),
            vec,
        ] + cast_specs,
        out_specs=out_specs,
        out_shape=out_shape,
        scratch_shapes=[pltpu.VMEM((tm, d), BF16)],
        compiler_params=_params(("arbitrary", "arbitrary")),
        name="ffn_" + post,
    )(x, g, wg, wu, wd, post_g, *[c[0] for c in casts])


def _store_head(o_ref, h, val):
    dil = DILATIONS[h % N_ATT_HEADS // HEADS_PER_GROUP]
    if dil == 1:
        o_ref[h] = val.astype(BF16)
        return
    regrouped = pltpu.einshape("lrd->rld", val.reshape(QKV_TILE // dil, dil, HEAD_DIM))
    o_ref[h] = regrouped.reshape(QKV_TILE, HEAD_DIM).astype(BF16)


def _proj_kernel(h_ref, w_ref, *refs, epilogue, cast_meta):
    n_cast = len(cast_meta)
    n_cast_out = sum(len(splits) - 1 for splits, _ in cast_meta)
    n_tables = 2 if epilogue == "qkv" else 0
    cast_in = refs[n_tables:n_tables + n_cast]
    cast_out = refs[n_tables + n_cast + 1:n_tables + n_cast + 1 + n_cast_out]
    refs = refs[:n_tables] + refs[n_tables + n_cast:n_tables + n_cast + 1]
    _cast_blocks(cast_in, cast_out, cast_meta)

    if epilogue == "qkv":
        cos_ref, sin_ref, o_ref = refs
        cos, sin = cos_ref[...], sin_ref[...]
        rope = ((cos * HEAD_DIM ** -0.5, sin * HEAD_DIM ** -0.5), (cos, sin), None)
        lhs = h_ref[...]
        group_cols = HEADS_PER_GROUP * HEAD_DIM
        for kind, tables in enumerate(rope):
            for g in reversed(range(N_GROUPS)):
                h0 = kind * N_ATT_HEADS + g * HEADS_PER_GROUP
                acc = jnp.dot(lhs, w_ref[:, h0 * HEAD_DIM:h0 * HEAD_DIM + group_cols], preferred_element_type=F32)
                for hh in range(HEADS_PER_GROUP):
                    t = acc[:, hh * HEAD_DIM:(hh + 1) * HEAD_DIM]
                    if tables is not None:
                        t = t * tables[0] + pltpu.roll(t, ROPE_PAIR_LANE, 1) * tables[1]
                    _store_head(o_ref, h0 + hh, t)
        return

    (o_ref,) = refs
    acc = jnp.dot(h_ref[...], w_ref[...], preferred_element_type=F32)
    if epilogue == "gelu":
        acc = 0.5 * acc * (1.0 + lax.erf(acc * (0.5 ** 0.5)))
    elif epilogue == "sigmoid":
        acc = 0.5 * jnp.tanh(0.5 * acc) + 0.5
    o_ref[...] = acc.astype(o_ref.dtype)


def _proj(hn, w, col0, ncols, *, epilogue, tn, tm, tables=(), casts=()):
    t, d = hn.shape
    assert col0 % tn == 0 and ncols % tn == 0
    c0 = col0 // tn
    n_j = ncols // tn
    in_specs = [
        pl.BlockSpec((tm, d), lambda i, j: (i, 0)),
        pl.BlockSpec((d, tn), lambda i, j: (0, c0 + j), pipeline_mode=pl.Buffered(1) if n_j == 1 else None),
    ]
    seq_blocks = SEQ // tm
    for _ in tables:
        in_specs.append(pl.BlockSpec((tm, V7X_LANES), lambda i, j: (i % seq_blocks, 0)))
    if epilogue == "qkv":
        assert tm == QKV_TILE and tn == ncols == 3 * ATT_WIDTH
        out_spec = pl.BlockSpec((3 * N_ATT_HEADS, tm, HEAD_DIM), lambda i, j: (0, i, 0))
        out_shape = jax.ShapeDtypeStruct((3 * N_ATT_HEADS, t, HEAD_DIM), BF16)
    else:
        out_spec = pl.BlockSpec((tm, tn), lambda i, j: (i, j))
        out_shape = jax.ShapeDtypeStruct((t, ncols), BF16)
    cast_specs, cast_out_specs, cast_out_shapes = _cast_specs(casts, (t // tm) * n_j, lambda i, j: i * n_j + j)
    res = pl.pallas_call(
        functools.partial(_proj_kernel, epilogue=epilogue, cast_meta=_cast_meta(casts)),
        grid=(t // tm, n_j),
        in_specs=in_specs + cast_specs,
        out_specs=[out_spec] + cast_out_specs,
        out_shape=[out_shape] + cast_out_shapes,
        compiler_params=_params(("arbitrary", "arbitrary")),
        name="proj_" + epilogue,
    )(hn, w, *tables, *[c[0] for c in casts])
    return res if casts else res[0]


ATT_HEADS_PER_STEP = 2


def _attn_kernel(q_ref, k_ref, v_ref, o_ref, st_ref, *, dil):
    step = pl.program_id(1)

    @pl.when(step == 0)
    def _():
        st_ref[...] = jnp.zeros_like(st_ref)

    for hh in range(ATT_HEADS_PER_STEP):
        _attn_head(q_ref.at[hh], k_ref.at[hh], v_ref.at[hh], o_ref.at[hh], st_ref,
                   step * ATT_HEADS_PER_STEP + hh, dil)


def _attn_head(q_ref, k_ref, v_ref, o_ref, st_ref, head, dil):
    n_qb = SEQ // (ATT_BLOCK * dil)

    qi = lax.broadcasted_iota(jnp.int32, (ATT_BLOCK, 2 * ATT_BLOCK), 0)
    kj = lax.broadcasted_iota(jnp.int32, (ATT_BLOCK, 2 * ATT_BLOCK), 1)
    diff = qi + ATT_BLOCK - kj
    band = (diff >= 0) & (diff <= ATT_BLOCK)
    band_first = band & (kj >= ATT_BLOCK)
    lane = lax.broadcasted_iota(jnp.int32, (ATT_BLOCK, V7X_LANES), 1)
    ones = jnp.ones((2 * ATT_BLOCK, HEAD_DIM), BF16)
    zeros = jnp.zeros((ATT_BLOCK, HEAD_DIM), BF16)

    rows_per = QKV_TILE // dil
    run = min(ATT_BLOCK, rows_per)

    def load(ref, qb, r):
        parts = []
        for idx in range(qb * ATT_BLOCK, (qb + 1) * ATT_BLOCK, run):
            tile, local = divmod(idx, rows_per)
            parts.append(ref[pl.ds(tile * QKV_TILE + r * rows_per + local, run), :])
        return parts[0] if len(parts) == 1 else jnp.concatenate(parts, axis=0)

    interleave = dil % 8 == 0

    def put(ref, qb, blocks, merge=None):
        if interleave:
            rows = pl.ds(qb * ATT_BLOCK * dil, ATT_BLOCK * dil)
            val = pltpu.einshape("rld->lrd", jnp.stack(blocks)).reshape(ATT_BLOCK * dil, V7X_LANES)
            ref[rows, :] = val if merge is None else merge(val, ref[rows, :])
            return
        for r, val in enumerate(blocks):
            rows = (pl.ds(qb * ATT_BLOCK * dil + r, ATT_BLOCK, stride=dil) if dil > 1
                    else pl.ds(qb * ATT_BLOCK, ATT_BLOCK))
            ref[rows, :] = val if merge is None else merge(val, ref[rows, :])

    def keep_other_heads(new, old):
        own = lax.broadcasted_iota(jnp.int32, new.shape, 1)
        return jnp.where((own == head) | (own == HEADS_PER_GROUP + head), new, old)

    k_prev, v_prev = [zeros] * dil, [zeros] * dil
    for qb in range(n_qb):
        outs, stats = [], []
        for r in range(dil):
            q, k_cur, v_cur = load(q_ref, qb, r), load(k_ref, qb, r), load(v_ref, qb, r)
            k_win = jnp.concatenate([k_prev[r], k_cur], axis=0)
            v_win = jnp.concatenate([v_prev[r], v_cur], axis=0)
            s = lax.dot_general(q, k_win, (((1,), (1,)), ((), ())), preferred_element_type=F32)
            s = jnp.where(band_first if qb == 0 else band, s, -jnp.inf)
            m = jnp.max(s, axis=-1, keepdims=True)
            p = jnp.exp(s - m).astype(BF16)
            pv = jnp.dot(p, jnp.concatenate([v_win, ones], axis=1), preferred_element_type=F32)
            l_b = pv[:, HEAD_DIM:]
            outs.append(pv[:, :HEAD_DIM] / l_b)
            stats.append(jnp.where(lane == head, m, l_b))
            k_prev[r], v_prev[r] = k_cur, v_cur
        put(o_ref, qb, outs)
        put(st_ref, qb, stats, merge=keep_other_heads)


def _attention(qkv, group, batch):
    dil = DILATIONS[group]
    t = qkv.shape[1]
    h0 = group * HEADS_PER_GROUP

    hps = ATT_HEADS_PER_STEP

    def seq(slab0):
        assert slab0 % hps == 0
        return pl.BlockSpec((hps, SEQ, HEAD_DIM), lambda b, s: (slab0 // hps + s, b, 0))

    return pl.pallas_call(
        functools.partial(_attn_kernel, dil=dil),
        grid=(batch, HEADS_PER_GROUP // hps),
        in_specs=[seq(h0), seq(N_ATT_HEADS + h0), seq(2 * N_ATT_HEADS + h0)],
        out_specs=[
            pl.BlockSpec((hps, SEQ, HEAD_DIM), lambda b, s: (s, b, 0)),
            pl.BlockSpec((SEQ, V7X_LANES), lambda b, s: (b, 0)),
        ],
        out_shape=[
            jax.ShapeDtypeStruct((HEADS_PER_GROUP, t, HEAD_DIM), F32),
            jax.ShapeDtypeStruct((t, V7X_LANES), F32),
        ],
        compiler_params=_params(("parallel", "arbitrary")),
        name=f"attn_d{dil}",
    )(qkv, qkv, qkv)


def _mix_prepare(o_refs, stats, u_ref, vs_ref, lng_ref, lnb_ref, sgw_ref, sgb_ref, oatt_ref, gated_ref):
    tm = oatt_ref.shape[0]
    for h in range(HEADS_PER_GROUP):
        ms = [s[:, h:h + 1] for s in stats]
        ls = [s[:, HEADS_PER_GROUP + h:HEADS_PER_GROUP + h + 1] for s in stats]
        m_all = jnp.maximum(jnp.maximum(ms[0], ms[1]), ms[2])
        ws = [l * jnp.exp(m - m_all) for l, m in zip(ls, ms)]
        num = ws[0] * o_refs[0][h] + ws[1] * o_refs[1][h] + ws[2] * o_refs[2][h]
        oatt_ref[:, h * HEAD_DIM:(h + 1) * HEAD_DIM] = (num / (ws[0] + ws[1] + ws[2])).astype(BF16)

    vs = vs_ref[...].astype(F32)
    mu = jnp.mean(vs, axis=-1, keepdims=True)
    var = jnp.mean(jnp.square(vs - mu), axis=-1, keepdims=True)
    vn = ((vs - mu) * lax.rsqrt(var + LN_EPS) * lng_ref[...] + lnb_ref[...]).astype(BF16)
    ti = lax.broadcasted_iota(jnp.int32, (SG_CHUNK, SG_CHUNK), 0)
    si = lax.broadcasted_iota(jnp.int32, (SG_CHUNK, SG_CHUNK), 1)
    causal = si <= ti
    for g in range(SG_GROUPS):
        cols = slice(g * SG_GROUP_DIM, (g + 1) * SG_GROUP_DIM)
        w_sp = jnp.where(causal, sgw_ref[g], 0.0).astype(BF16)
        bias = sgb_ref[:, g:g + 1]
        for c in range(tm // SG_CHUNK):
            rows = slice(c * SG_CHUNK, (c + 1) * SG_CHUNK)
            spatial = jnp.dot(w_sp, vn[rows, cols], preferred_element_type=F32) + bias
            gated_ref[rows, cols] = (u_ref[rows, cols].astype(F32) * spatial).astype(BF16)


def _mix_kernel(o0_ref, o1_ref, o2_ref, s0_ref, s1_ref, s2_ref, u_ref, vs_ref, ga_ref, gs_ref,
                lng_ref, lnb_ref, sgw_ref, sgb_ref, wa_ref, ws_ref, out_ref,
                oatt_a, gated_a, oatt_b, gated_b):
    i = pl.program_id(0)

    @pl.when(i == 0)
    def _():
        oatt_b[...] = jnp.zeros_like(oatt_b)
        gated_b[...] = jnp.zeros_like(gated_b)

    def step(oatt_w, gated_w, oatt_r, gated_r):
        y_att = jnp.dot(oatt_r[...], wa_ref[...], preferred_element_type=F32)
        y_sg = jnp.dot(gated_r[...], ws_ref[...], preferred_element_type=F32)
        merged = ga_ref[...].astype(F32) * y_att + gs_ref[...].astype(F32) * y_sg
        out_ref[...] = merged.astype(out_ref.dtype)
        _mix_prepare((o0_ref, o1_ref, o2_ref), (s0_ref[...], s1_ref[...], s2_ref[...]), u_ref, vs_ref,
                     lng_ref, lnb_ref, sgw_ref, sgb_ref, oatt_w, gated_w)

    @pl.when(i % 2 == 0)
    def _():
        step(oatt_a, gated_a, oatt_b, gated_b)

    @pl.when(i % 2 == 1)
    def _():
        step(oatt_b, gated_b, oatt_a, gated_a)


def _mix(o_list, st_list, uv, gates, ln_g, ln_b, sg_w, sg_b_t, w_att, w_sg, *, tm=512):
    t = uv.shape[0]
    n = t // tm
    cur = lambda i: jnp.minimum(i, n - 1)
    lag = lambda i: jnp.maximum(i - 1, 0)
    prep_row = lambda width, blk: pl.BlockSpec((tm, width), lambda i: (cur(i), blk))
    lag_row = lambda width, blk: pl.BlockSpec((tm, width), lambda i: (lag(i), blk))
    const2 = lambda shape: pl.BlockSpec(shape, lambda i: (0, 0), pipeline_mode=pl.Buffered(1))
    in_specs = (
        [pl.BlockSpec((HEADS_PER_GROUP, tm, HEAD_DIM), lambda i: (0, cur(i), 0))] * 3
        + [prep_row(V7X_LANES, 0)] * 3
        + [prep_row(SG_WIDTH, 0), prep_row(SG_WIDTH, 1), lag_row(D_MODEL, 0), lag_row(D_MODEL, 1)]
        + [const2((1, SG_WIDTH)), const2((1, SG_WIDTH)),
           pl.BlockSpec(sg_w.shape, lambda i: (0, 0, 0), pipeline_mode=pl.Buffered(1)),
           const2(sg_b_t.shape), const2(w_att.shape), const2(w_sg.shape)]
    )
    att_w = HEADS_PER_GROUP * HEAD_DIM
    return pl.pallas_call(
        _mix_kernel,
        grid=(n + 1,),
        in_specs=in_specs,
        out_specs=pl.BlockSpec((tm, D_MODEL), lambda i: (lag(i), 0)),
        out_shape=jax.ShapeDtypeStruct((t, D_MODEL), BF16),
        scratch_shapes=[pltpu.VMEM((tm, att_w), BF16), pltpu.VMEM((tm, SG_WIDTH), BF16),
                        pltpu.VMEM((tm, att_w), BF16), pltpu.VMEM((tm, SG_WIDTH), BF16)],
        compiler_params=_params(("arbitrary",)),
        name="mix",
    )(*o_list, *st_list, uv, uv, gates, gates, ln_g, ln_b, sg_w, sg_b_t, w_att, w_sg)


def _out_proj_kernel(m_ref, w_ref, x_ref, o_ref):
    o_ref[...] = x_ref[...] + jnp.dot(m_ref[...], w_ref[...], preferred_element_type=F32)


def _out_proj(merged, w, x, *, tm=1024):
    t, d = merged.shape
    n = w.shape[1]
    return pl.pallas_call(
        _out_proj_kernel,
        grid=(t // tm,),
        in_specs=[
            pl.BlockSpec((tm, d), lambda i: (i, 0)),
            pl.BlockSpec((d, n), lambda i: (0, 0), pipeline_mode=pl.Buffered(1)),
            pl.BlockSpec((tm, n), lambda i: (i, 0)),
        ],
        out_specs=pl.BlockSpec((tm, n), lambda i: (i, 0)),
        out_shape=jax.ShapeDtypeStruct((t, n), F32),
        compiler_params=_params(("parallel",)),
        name="out_proj",
    )(merged, w, x)


def kernel(x, ffn1_norm, ffn1_w_gate, ffn1_w_up, ffn1_w_down, mix_norm, w_in, sg_ln_g, sg_ln_b, sg_w, sg_b, w_att_out, w_sg_out, w_out, ffn2_norm, ffn2_w_gate, ffn2_w_up, ffn2_w_down, final_norm):
    batch, seq, d = x.shape
    assert (seq, d) == (SEQ, D_MODEL)
    t = batch * seq
    depth = ffn1_norm.shape[0]
    xt = x.reshape(t, d)
    tables = _rope_tables()
    bf = lambda a: a.astype(BF16)
    for i in range(depth):
        last = i == depth - 1
        c_v, c_u, c_g = 2 * ATT_WIDTH, 3 * ATT_WIDTH, 3 * ATT_WIDTH + 2 * SG_WIDTH
        whole = lambda w: (w, (0, w.shape[1]), 0)
        x1, hn, w_in_b, w_gates_b = _ffn(
            xt, ffn1_norm[i][None], bf(ffn1_w_gate[i]), bf(ffn1_w_up[i]), ffn1_w_down[i],
            mix_norm[i][None], post="hn", casts=((w_in[i], (0, c_g, w_in.shape[2]), c_v),))
        qkv, w_out_b, w_sg_b, w_att_b = _proj(
            hn, w_in_b, 0, c_u, epilogue="qkv", tn=c_u, tm=QKV_TILE, tables=tables,
            casts=(whole(w_out[i]), whole(w_sg_out[i]), whole(w_att_out[i])))
        uv, w2_gate, w2_up = _proj(hn, w_in_b, c_u, 2 * SG_WIDTH, epilogue="gelu", tn=SG_WIDTH, tm=1024,
                                   casts=(whole(ffn2_w_gate[i]), whole(ffn2_w_up[i])))
        gates, w2_down = _proj(hn, w_gates_b, 0, 2 * D_MODEL, epilogue="sigmoid", tn=D_MODEL, tm=1024,
                               casts=(whole(ffn2_w_down[i]),))
        o_list, st_list = [], []
        for g in range(N_GROUPS):
            o_g, st_g = _attention(qkv, g, batch)
            o_list.append(o_g)
            st_list.append(st_g)
        merged = _mix(o_list, st_list, uv, gates, sg_ln_g[i][None], sg_ln_b[i][None], sg_w[i], sg_b[i].T,
                      w_att_b, w_sg_b)
        x2 = _out_proj(merged, w_out_b, x1)
        (xt,) = _ffn(x2, ffn2_norm[i][None], w2_gate, w2_up, w2_down,
                     final_norm[None], post="final" if last else "plain")
    return xt.reshape(batch, seq, d)
```

```python
import functools

import jax
import jax.numpy as jnp
from jax import lax
from jax.experimental import pallas as pl
from jax.experimental.pallas import tpu as pltpu

F32 = jnp.float32
BF16 = jnp.bfloat16

D_MODEL = 2048
SEQ = 4096
HEAD_DIM = 128
HEADS_PER_GROUP = 4
DILATIONS = (1, 4, 16)
N_GROUPS = len(DILATIONS)
N_ATT_HEADS = N_GROUPS * HEADS_PER_GROUP
ATT_WIDTH = N_ATT_HEADS * HEAD_DIM
ATT_BLOCK = 128
QKV_TILE = 512
ROPE_DIM = HEAD_DIM // 4
ROPE_HALF = ROPE_DIM // 2
ROPE_THETA = 500000.0
SG_CHUNK = 128
SG_GROUPS = 12
SG_GROUP_DIM = 128
SG_WIDTH = SG_GROUPS * SG_GROUP_DIM
NORM_EPS = 1e-6
LN_EPS = 1e-5

V7X_VMEM_BYTES = 64 * 1024 * 1024
V7X_LANES = 128
BF16_ROWS = 16
CAST_STEPS = 128
VMEM_LIMIT = V7X_VMEM_BYTES - 12 * 1024 * 1024


def _params(semantics, vmem_limit=VMEM_LIMIT):
    return pltpu.CompilerParams(dimension_semantics=semantics, vmem_limit_bytes=vmem_limit)


def _rms(x, g):
    return x * lax.rsqrt(jnp.mean(x * x, axis=-1, keepdims=True) + NORM_EPS) * g


ROPE_PAIR_LANE = V7X_LANES // 2


def _rope_lane_masks(lane):
    first = lane < ROPE_HALF
    second = (lane >= ROPE_PAIR_LANE) & (lane < ROPE_PAIR_LANE + ROPE_HALF)
    return first, second


def _rope_lane_layout(x):
    cols = x.shape[1]
    lane = lax.broadcasted_iota(jnp.int32, x.shape, 1) % V7X_LANES
    shift = ROPE_PAIR_LANE - ROPE_HALF
    to_pair_lane = (lane >= ROPE_PAIR_LANE) & (lane < ROPE_PAIR_LANE + ROPE_HALF)
    from_pair_lane = (lane >= ROPE_HALF) & (lane < ROPE_DIM)
    return jnp.where(to_pair_lane, pltpu.roll(x, shift, 1),
                     jnp.where(from_pair_lane, pltpu.roll(x, cols - shift, 1), x))


def _rope_table_kernel(freq_ref, cos_ref, sin_ref):
    rows = cos_ref.shape[0]
    pos = (pl.program_id(0) * rows + lax.broadcasted_iota(jnp.int32, (rows, V7X_LANES), 0)).astype(F32)
    first, second = _rope_lane_masks(lax.broadcasted_iota(jnp.int32, (rows, V7X_LANES), 1))
    ang = pos * freq_ref[...]
    c = jnp.cos(ang)
    s = jnp.sin(ang)
    cos_ref[...] = jnp.where(first | second, c, 1.0)
    sin_ref[...] = jnp.where(first, -s, jnp.where(second, s, 0.0))


def _rope_tables():
    inv_freq = ROPE_THETA ** (-jnp.arange(0, ROPE_DIM, 2, dtype=F32) / ROPE_DIM)
    gap = jnp.zeros((ROPE_PAIR_LANE - ROPE_HALF,), F32)
    freq_lane = jnp.concatenate([inv_freq, gap, inv_freq, gap])[None, :]
    rows = 512
    tab = jax.ShapeDtypeStruct((SEQ, V7X_LANES), F32)
    spec = pl.BlockSpec((rows, V7X_LANES), lambda i: (i, 0))
    return pl.pallas_call(
        _rope_table_kernel,
        grid=(SEQ // rows,),
        in_specs=[pl.BlockSpec((1, V7X_LANES), lambda i: (0, 0))],
        out_specs=[spec, spec],
        out_shape=[tab, tab],
        compiler_params=_params(("parallel",)),
        name="rope_table",
    )(freq_lane)


def _cast_specs(casts, n_steps, step_of):
    in_specs, out_specs, out_shapes = [], [], []
    for w, splits, _ in casts:
        rows, cols = w.shape
        assert splits[0] == 0 and splits[-1] == cols
        rb = next(r for r in range(BF16_ROWS, rows + 1, BF16_ROWS) if rows % r == 0 and rows // r <= n_steps)
        n_blocks = rows // rb
        block_of = lambda *idx, n_blocks=n_blocks: (jnp.minimum(step_of(*idx), n_blocks - 1), 0)
        in_specs.append(pl.BlockSpec((rb, cols), block_of))
        for c0, c1 in zip(splits[:-1], splits[1:]):
            out_specs.append(pl.BlockSpec((rb, c1 - c0), block_of))
            out_shapes.append(jax.ShapeDtypeStruct((rows, c1 - c0), BF16))
    return in_specs, out_specs, out_shapes


def _cast_blocks(cast_in, cast_out, cast_meta):
    cast_out = iter(cast_out)
    for src, (splits, rope_cols) in zip(cast_in, cast_meta):
        for c0, c1 in zip(splits[:-1], splits[1:]):
            dst = next(cast_out)
            n_rope = min(max(rope_cols - c0, 0), c1 - c0)
            if n_rope:
                dst[:, :n_rope] = _rope_lane_layout(src[:, c0:c0 + n_rope]).astype(BF16)
            if n_rope < c1 - c0:
                dst[:, n_rope:] = src[:, c0 + n_rope:c1].astype(BF16)


def _cast_meta(casts):
    return tuple((splits, rope_cols) for _, splits, rope_cols in casts)


def _ffn_kernel(x_ref, g_ref, wg_ref, wu_ref, wd_ref, pg_ref, *refs, post, cast_meta):
    n_cast = len(cast_meta)
    cast_in, refs = refs[:n_cast], refs[n_cast:]
    if post == "hn":
        o_ref, hn_ref, *cast_out, xn_ref = refs
    else:
        o_ref, *cast_out, xn_ref = refs
    j = pl.program_id(1)
    last = pl.num_programs(1) - 1

    def tile_contribution(xn):
        _cast_blocks(cast_in, cast_out, cast_meta)
        h = jnp.dot(xn, wg_ref[...], preferred_element_type=F32)
        u = jnp.dot(xn, wu_ref[...], preferred_element_type=F32)
        a = ((0.5 * h) * (1.0 + jnp.tanh(0.5 * h)) * u).astype(BF16)
        return jnp.dot(a, wd_ref[...].astype(BF16), preferred_element_type=F32)

    @pl.when(j == 0)
    def _():
        xn = _rms(x_ref[...], g_ref[...]).astype(BF16)
        xn_ref[...] = xn
        o_ref[...] = tile_contribution(xn)

    @pl.when((j > 0) & (j < last))
    def _():
        o_ref[...] += tile_contribution(xn_ref[...])

    @pl.when(j == last)
    def _():
        y = x_ref[...] + 0.5 * (o_ref[...] + tile_contribution(xn_ref[...]))
        if post == "hn":
            o_ref[...] = y
            hn_ref[...] = _rms(y, pg_ref[...]).astype(BF16)
        elif post == "final":
            o_ref[...] = _rms(y, pg_ref[...])
        else:
            o_ref[...] = y


def _ffn(x, g, wg, wu, wd, post_g, *, post, casts=(), tm=512, tf=512):
    t, d = x.shape
    f = wg.shape[1]
    n_j = f // tf
    row = pl.BlockSpec((tm, d), lambda i, j: (i, 0))
    vec = pl.BlockSpec((1, d), lambda i, j: (0, 0))
    out_shape = [jax.ShapeDtypeStruct((t, d), F32)]
    out_specs = [row]
    if post == "hn":
        out_shape.append(jax.ShapeDtypeStruct((t, d), BF16))
        out_specs.append(row)
    cast_specs, cast_out_specs, cast_out_shapes = _cast_specs(
        casts, min(CAST_STEPS, (t // tm) * n_j), lambda i, j: i * n_j + j)
    out_specs += cast_out_specs
    out_shape += cast_out_shapes
    return pl.pallas_call(
        functools.partial(_ffn_kernel, post=post, cast_meta=_cast_meta(casts)),
        grid=(t // tm, n_j),
        in_specs=[
            row,
            vec,
            pl.BlockSpec((d, tf), lambda i, j: (0, j)),
            pl.BlockSpec((d, tf), lambda i, j: (0, j)),
            pl.BlockSpec((tf, d), lambda i, j: (j, 0)),
            vec,
        ] + cast_specs,
        out_specs=out_specs,
        out_shape=out_shape,
        scratch_shapes=[pltpu.VMEM((tm, d), BF16)],
        compiler_params=_params(("arbitrary", "arbitrary")),
        name="ffn_" + post,
    )(x, g, wg, wu, wd, post_g, *[c[0] for c in casts])


def _store_head(o_ref, h, val):
    dil = DILATIONS[h % N_ATT_HEADS // HEADS_PER_GROUP]
    if dil == 1:
        o_ref[h] = val.astype(BF16)
        return
    regrouped = pltpu.einshape("lrd->rld", val.reshape(QKV_TILE // dil, dil, HEAD_DIM))
    o_ref[h] = regrouped.reshape(QKV_TILE, HEAD_DIM).astype(BF16)


def _proj_kernel(h_ref, w_ref, *refs, epilogue, cast_meta):
    n_cast = len(cast_meta)
    n_cast_out = sum(len(splits) - 1 for splits, _ in cast_meta)
    n_tables = 2 if epilogue == "qkv" else 0
    cast_in = refs[n_tables:n_tables + n_cast]
    cast_out = refs[n_tables + n_cast + 1:n_tables + n_cast + 1 + n_cast_out]
    refs = refs[:n_tables] + refs[n_tables + n_cast:n_tables + n_cast + 1]
    _cast_blocks(cast_in, cast_out, cast_meta)

    if epilogue == "qkv":
        cos_ref, sin_ref, o_ref = refs
        cos, sin = cos_ref[...], sin_ref[...]
        rope = ((cos * HEAD_DIM ** -0.5, sin * HEAD_DIM ** -0.5), (cos, sin), None)
        lhs = h_ref[...]
        group_cols = HEADS_PER_GROUP * HEAD_DIM
        for kind, tables in enumerate(rope):
            for g in reversed(range(N_GROUPS)):
                h0 = kind * N_ATT_HEADS + g * HEADS_PER_GROUP
                acc = jnp.dot(lhs, w_ref[:, h0 * HEAD_DIM:h0 * HEAD_DIM + group_cols], preferred_element_type=F32)
                for hh in range(HEADS_PER_GROUP):
                    t = acc[:, hh * HEAD_DIM:(hh + 1) * HEAD_DIM]
                    if tables is not None:
                        t = t * tables[0] + pltpu.roll(t, ROPE_PAIR_LANE, 1) * tables[1]
                    _store_head(o_ref, h0 + hh, t)
        return

    (o_ref,) = refs
    acc = jnp.dot(h_ref[...], w_ref[...], preferred_element_type=F32)
    if epilogue == "gelu":
        acc = 0.5 * acc * (1.0 + lax.erf(acc * (0.5 ** 0.5)))
    elif epilogue == "sigmoid":
        acc = 0.5 * jnp.tanh(0.5 * acc) + 0.5
    o_ref[...] = acc.astype(o_ref.dtype)


def _proj(hn, w, col0, ncols, *, epilogue, tn, tm, tables=(), casts=()):
    t, d = hn.shape
    assert col0 % tn == 0 and ncols % tn == 0
    c0 = col0 // tn
    n_j = ncols // tn
    in_specs = [
        pl.BlockSpec((tm, d), lambda i, j: (i, 0)),
        pl.BlockSpec((d, tn), lambda i, j: (0, c0 + j), pipeline_mode=pl.Buffered(1) if n_j == 1 else None),
    ]
    seq_blocks = SEQ // tm
    for _ in tables:
        in_specs.append(pl.BlockSpec((tm, V7X_LANES), lambda i, j: (i % seq_blocks, 0)))
    if epilogue == "qkv":
        assert tm == QKV_TILE and tn == ncols == 3 * ATT_WIDTH
        out_spec = pl.BlockSpec((3 * N_ATT_HEADS, tm, HEAD_DIM), lambda i, j: (0, i, 0))
        out_shape = jax.ShapeDtypeStruct((3 * N_ATT_HEADS, t, HEAD_DIM), BF16)
    else:
        out_spec = pl.BlockSpec((tm, tn), lambda i, j: (i, j))
        out_shape = jax.ShapeDtypeStruct((t, ncols), BF16)
    cast_specs, cast_out_specs, cast_out_shapes = _cast_specs(casts, (t // tm) * n_j, lambda i, j: i * n_j + j)
    res = pl.pallas_call(
        functools.partial(_proj_kernel, epilogue=epilogue, cast_meta=_cast_meta(casts)),
        grid=(t // tm, n_j),
        in_specs=in_specs + cast_specs,
        out_specs=[out_spec] + cast_out_specs,
        out_shape=[out_shape] + cast_out_shapes,
        compiler_params=_params(("arbitrary", "arbitrary")),
        name="proj_" + epilogue,
    )(hn, w, *tables, *[c[0] for c in casts])
    return res if casts else res[0]


ATT_HEADS_PER_STEP = 2


def _strided_out(dil):
    return dil > 1 and dil % 8 != 0


def _attn_kernel(q_ref, k_ref, v_ref, o_ref, st_ref, *, dil):
    step = pl.program_id(1)

    @pl.when(step == 0)
    def _():
        st_ref[...] = jnp.zeros_like(st_ref)

    for hh in range(ATT_HEADS_PER_STEP):
        _attn_head(q_ref.at[hh], k_ref.at[hh], v_ref.at[hh], o_ref.at[hh], st_ref,
                   step * ATT_HEADS_PER_STEP + hh, dil)


def _attn_head(q_ref, k_ref, v_ref, o_ref, st_ref, head, dil):
    n_qb = SEQ // (ATT_BLOCK * dil)

    qi = lax.broadcasted_iota(jnp.int32, (ATT_BLOCK, 2 * ATT_BLOCK), 0)
    kj = lax.broadcasted_iota(jnp.int32, (ATT_BLOCK, 2 * ATT_BLOCK), 1)
    diff = qi + ATT_BLOCK - kj
    band = (diff >= 0) & (diff <= ATT_BLOCK)
    band_first = band & (kj >= ATT_BLOCK)
    lane = lax.broadcasted_iota(jnp.int32, (ATT_BLOCK, V7X_LANES), 1)
    ones = jnp.ones((2 * ATT_BLOCK, HEAD_DIM), BF16)
    zeros = jnp.zeros((ATT_BLOCK, HEAD_DIM), BF16)

    rows_per = QKV_TILE // dil
    run = min(ATT_BLOCK, rows_per)

    def load(ref, qb, r):
        parts = []
        for idx in range(qb * ATT_BLOCK, (qb + 1) * ATT_BLOCK, run):
            tile, local = divmod(idx, rows_per)
            parts.append(ref[pl.ds(tile * QKV_TILE + r * rows_per + local, run), :])
        return parts[0] if len(parts) == 1 else jnp.concatenate(parts, axis=0)

    interleave = dil > 1 and not _strided_out(dil)

    def put(ref, qb, blocks, merge=None):
        if interleave:
            rows = pl.ds(qb * ATT_BLOCK * dil, ATT_BLOCK * dil)
            val = pltpu.einshape("rld->lrd", jnp.stack(blocks)).reshape(ATT_BLOCK * dil, V7X_LANES)
            ref[rows, :] = (val if merge is None else merge(val, ref[rows, :])).astype(ref.dtype)
            return
        for r, val in enumerate(blocks):
            rows = (pl.ds(qb * ATT_BLOCK * dil + r, ATT_BLOCK, stride=dil) if dil > 1
                    else pl.ds(qb * ATT_BLOCK, ATT_BLOCK))
            ref[rows, :] = (val if merge is None else merge(val, ref[rows, :])).astype(ref.dtype)

    def keep_other_heads(new, old):
        own = lax.broadcasted_iota(jnp.int32, new.shape, 1)
        return jnp.where((own == head) | (own == HEADS_PER_GROUP + head), new, old)

    k_prev, v_prev = [zeros] * dil, [zeros] * dil
    for qb in range(n_qb):
        outs, stats = [], []
        for r in range(dil):
            q, k_cur, v_cur = load(q_ref, qb, r), load(k_ref, qb, r), load(v_ref, qb, r)
            k_win = jnp.concatenate([k_prev[r], k_cur], axis=0)
            v_win = jnp.concatenate([v_prev[r], v_cur], axis=0)
            s = lax.dot_general(q, k_win, (((1,), (1,)), ((), ())), preferred_element_type=F32)
            s = jnp.where(band_first if qb == 0 else band, s, -jnp.inf)
            m = jnp.max(s, axis=-1, keepdims=True)
            p = jnp.exp(s - m).astype(BF16)
            pv = jnp.dot(p, jnp.concatenate([v_win, ones], axis=1), preferred_element_type=F32)
            l_b = pv[:, HEAD_DIM:]
            outs.append(pv[:, :HEAD_DIM] / l_b)
            stats.append(jnp.where(lane == head, m, l_b))
            k_prev[r], v_prev[r] = k_cur, v_cur
        put(o_ref, qb, outs)
        put(st_ref, qb, stats, merge=keep_other_heads)


def _attention(qkv, group, batch):
    dil = DILATIONS[group]
    t = qkv.shape[1]
    h0 = group * HEADS_PER_GROUP

    hps = ATT_HEADS_PER_STEP

    def seq(slab0):
        assert slab0 % hps == 0
        return pl.BlockSpec((hps, SEQ, HEAD_DIM), lambda b, s: (slab0 // hps + s, b, 0))

    return pl.pallas_call(
        functools.partial(_attn_kernel, dil=dil),
        grid=(batch, HEADS_PER_GROUP // hps),
        in_specs=[seq(h0), seq(N_ATT_HEADS + h0), seq(2 * N_ATT_HEADS + h0)],
        out_specs=[
            pl.BlockSpec((hps, SEQ, HEAD_DIM), lambda b, s: (s, b, 0)),
            pl.BlockSpec((SEQ, V7X_LANES), lambda b, s: (b, 0)),
        ],
        out_shape=[
            jax.ShapeDtypeStruct((HEADS_PER_GROUP, t, HEAD_DIM), F32 if _strided_out(dil) else BF16),
            jax.ShapeDtypeStruct((t, V7X_LANES), F32),
        ],
        compiler_params=_params(("parallel", "arbitrary")),
        name=f"attn_d{dil}",
    )(qkv, qkv, qkv)


def _mix_prepare(o_refs, stats, u_ref, vs_ref, lng_ref, lnb_ref, sgw_ref, sgb_ref, oatt_ref, gated_ref):
    tm = oatt_ref.shape[0]
    for h in range(HEADS_PER_GROUP):
        ms = [s[:, h:h + 1] for s in stats]
        ls = [s[:, HEADS_PER_GROUP + h:HEADS_PER_GROUP + h + 1] for s in stats]
        m_all = jnp.maximum(jnp.maximum(ms[0], ms[1]), ms[2])
        ws = [l * jnp.exp(m - m_all) for l, m in zip(ls, ms)]
        num = sum(w * o_ref[h].astype(F32) for w, o_ref in zip(ws, o_refs))
        oatt_ref[:, h * HEAD_DIM:(h + 1) * HEAD_DIM] = (num / (ws[0] + ws[1] + ws[2])).astype(BF16)

    vs = vs_ref[...].astype(F32)
    mu = jnp.mean(vs, axis=-1, keepdims=True)
    var = jnp.mean(jnp.square(vs - mu), axis=-1, keepdims=True)
    vn = ((vs - mu) * lax.rsqrt(var + LN_EPS) * lng_ref[...] + lnb_ref[...]).astype(BF16)
    ti = lax.broadcasted_iota(jnp.int32, (SG_CHUNK, SG_CHUNK), 0)
    si = lax.broadcasted_iota(jnp.int32, (SG_CHUNK, SG_CHUNK), 1)
    causal = si <= ti
    for g in range(SG_GROUPS):
        cols = slice(g * SG_GROUP_DIM, (g + 1) * SG_GROUP_DIM)
        w_sp = jnp.where(causal, sgw_ref[g], 0.0).astype(BF16)
        bias = sgb_ref[:, g:g + 1]
        for c in range(tm // SG_CHUNK):
            rows = slice(c * SG_CHUNK, (c + 1) * SG_CHUNK)
            spatial = jnp.dot(w_sp, vn[rows, cols], preferred_element_type=F32) + bias
            gated_ref[rows, cols] = (u_ref[rows, cols].astype(F32) * spatial).astype(BF16)


def _mix_kernel(o0_ref, o1_ref, o2_ref, s0_ref, s1_ref, s2_ref, u_ref, vs_ref, ga_ref, gs_ref,
                lng_ref, lnb_ref, sgw_ref, sgb_ref, wa_ref, ws_ref, out_ref,
                oatt_a, gated_a, oatt_b, gated_b):
    i = pl.program_id(0)

    @pl.when(i == 0)
    def _():
        oatt_b[...] = jnp.zeros_like(oatt_b)
        gated_b[...] = jnp.zeros_like(gated_b)

    def step(oatt_w, gated_w, oatt_r, gated_r):
        y_att = jnp.dot(oatt_r[...], wa_ref[...], preferred_element_type=F32)
        y_sg = jnp.dot(gated_r[...], ws_ref[...], preferred_element_type=F32)
        merged = ga_ref[...].astype(F32) * y_att + gs_ref[...].astype(F32) * y_sg
        out_ref[...] = merged.astype(out_ref.dtype)
        _mix_prepare((o0_ref, o1_ref, o2_ref), (s0_ref[...], s1_ref[...], s2_ref[...]), u_ref, vs_ref,
                     lng_ref, lnb_ref, sgw_ref, sgb_ref, oatt_w, gated_w)

    @pl.when(i % 2 == 0)
    def _():
        step(oatt_a, gated_a, oatt_b, gated_b)

    @pl.when(i % 2 == 1)
    def _():
        step(oatt_b, gated_b, oatt_a, gated_a)


def _mix(o_list, st_list, uv, gates, ln_g, ln_b, sg_w, sg_b_t, w_att, w_sg, *, tm=512):
    t = uv.shape[0]
    n = t // tm
    cur = lambda i: jnp.minimum(i, n - 1)
    lag = lambda i: jnp.maximum(i - 1, 0)
    prep_row = lambda width, blk: pl.BlockSpec((tm, width), lambda i: (cur(i), blk))
    lag_row = lambda width, blk: pl.BlockSpec((tm, width), lambda i: (lag(i), blk))
    const2 = lambda shape: pl.BlockSpec(shape, lambda i: (0, 0), pipeline_mode=pl.Buffered(1))
    in_specs = (
        [pl.BlockSpec((HEADS_PER_GROUP, tm, HEAD_DIM), lambda i: (0, cur(i), 0))] * 3
        + [prep_row(V7X_LANES, 0)] * 3
        + [prep_row(SG_WIDTH, 0), prep_row(SG_WIDTH, 1), lag_row(D_MODEL, 0), lag_row(D_MODEL, 1)]
        + [const2((1, SG_WIDTH)), const2((1, SG_WIDTH)),
           pl.BlockSpec(sg_w.shape, lambda i: (0, 0, 0), pipeline_mode=pl.Buffered(1)),
           const2(sg_b_t.shape), const2(w_att.shape), const2(w_sg.shape)]
    )
    att_w = HEADS_PER_GROUP * HEAD_DIM
    return pl.pallas_call(
        _mix_kernel,
        grid=(n + 1,),
        in_specs=in_specs,
        out_specs=pl.BlockSpec((tm, D_MODEL), lambda i: (lag(i), 0)),
        out_shape=jax.ShapeDtypeStruct((t, D_MODEL), BF16),
        scratch_shapes=[pltpu.VMEM((tm, att_w), BF16), pltpu.VMEM((tm, SG_WIDTH), BF16),
                        pltpu.VMEM((tm, att_w), BF16), pltpu.VMEM((tm, SG_WIDTH), BF16)],
        compiler_params=_params(("arbitrary",)),
        name="mix",
    )(*o_list, *st_list, uv, uv, gates, gates, ln_g, ln_b, sg_w, sg_b_t, w_att, w_sg)


def _out_proj_kernel(m_ref, w_ref, x_ref, o_ref):
    o_ref[...] = x_ref[...] + jnp.dot(m_ref[...], w_ref[...], preferred_element_type=F32)


def _out_proj(merged, w, x, *, tm=1024):
    t, d = merged.shape
    n = w.shape[1]
    return pl.pallas_call(
        _out_proj_kernel,
        grid=(t // tm,),
        in_specs=[
            pl.BlockSpec((tm, d), lambda i: (i, 0)),
            pl.BlockSpec((d, n), lambda i: (0, 0), pipeline_mode=pl.Buffered(1)),
            pl.BlockSpec((tm, n), lambda i: (i, 0)),
        ],
        out_specs=pl.BlockSpec((tm, n), lambda i: (i, 0)),
        out_shape=jax.ShapeDtypeStruct((t, n), F32),
        compiler_params=_params(("parallel",)),
        name="out_proj",
    )(merged, w, x)


def kernel(x, ffn1_norm, ffn1_w_gate, ffn1_w_up, ffn1_w_down, mix_norm, w_in, sg_ln_g, sg_ln_b, sg_w, sg_b, w_att_out, w_sg_out, w_out, ffn2_norm, ffn2_w_gate, ffn2_w_up, ffn2_w_down, final_norm):
    batch, seq, d = x.shape
    assert (seq, d) == (SEQ, D_MODEL)
    t = batch * seq
    depth = ffn1_norm.shape[0]
    xt = x.reshape(t, d)
    tables = _rope_tables()
    bf = lambda a: a.astype(BF16)
    for i in range(depth):
        last = i == depth - 1
        c_v, c_u, c_g = 2 * ATT_WIDTH, 3 * ATT_WIDTH, 3 * ATT_WIDTH + 2 * SG_WIDTH
        whole = lambda w: (w, (0, w.shape[1]), 0)
        x1, hn, w_in_b, w_gates_b = _ffn(
            xt, ffn1_norm[i][None], bf(ffn1_w_gate[i]), bf(ffn1_w_up[i]), ffn1_w_down[i],
            mix_norm[i][None], post="hn", casts=((w_in[i], (0, c_g, w_in.shape[2]), c_v),))
        qkv, w_out_b, w_sg_b, w_att_b = _proj(
            hn, w_in_b, 0, c_u, epilogue="qkv", tn=c_u, tm=QKV_TILE, tables=tables,
            casts=(whole(w_out[i]), whole(w_sg_out[i]), whole(w_att_out[i])))
        uv, w2_gate, w2_up = _proj(hn, w_in_b, c_u, 2 * SG_WIDTH, epilogue="gelu", tn=SG_WIDTH, tm=1024,
                                   casts=(whole(ffn2_w_gate[i]), whole(ffn2_w_up[i])))
        gates, w2_down = _proj(hn, w_gates_b, 0, 2 * D_MODEL, epilogue="sigmoid", tn=D_MODEL, tm=1024,
                               casts=(whole(ffn2_w_down[i]),))
        o_list, st_list = [], []
        for g in range(N_GROUPS):
            o_g, st_g = _attention(qkv, g, batch)
            o_list.append(o_g)
            st_list.append(st_g)
        merged = _mix(o_list, st_list, uv, gates, sg_ln_g[i][None], sg_ln_b[i][None], sg_w[i], sg_b[i].T,
                      w_att_b, w_sg_b)
        x2 = _out_proj(merged, w_out_b, x1)
        (xt,) = _ffn(x2, ffn2_norm[i][None], w2_gate, w2_up, w2_down,
                     final_norm[None], post="final" if last else "plain")
    return xt.reshape(batch, seq, d)
```

```python
import functools

import jax
import jax.numpy as jnp
from jax import lax
from jax.experimental import pallas as pl
from jax.experimental.pallas import tpu as pltpu

F32 = jnp.float32
BF16 = jnp.bfloat16

D_MODEL = 2048
SEQ = 4096
HEAD_DIM = 128
HEADS_PER_GROUP = 4
DILATIONS = (1, 4, 16)
N_GROUPS = len(DILATIONS)
N_ATT_HEADS = N_GROUPS * HEADS_PER_GROUP
ATT_WIDTH = N_ATT_HEADS * HEAD_DIM
ATT_BLOCK = 128
QKV_TILE = 512
ROPE_DIM = HEAD_DIM // 4
ROPE_HALF = ROPE_DIM // 2
ROPE_THETA = 500000.0
SG_CHUNK = 128
SG_GROUPS = 12
SG_GROUP_DIM = 128
SG_WIDTH = SG_GROUPS * SG_GROUP_DIM
NORM_EPS = 1e-6
LN_EPS = 1e-5

V7X_VMEM_BYTES = 64 * 1024 * 1024
V7X_LANES = 128
BF16_ROWS = 16
CAST_STEPS = 128
VMEM_LIMIT = V7X_VMEM_BYTES - 12 * 1024 * 1024


def _params(semantics, vmem_limit=VMEM_LIMIT):
    return pltpu.CompilerParams(dimension_semantics=semantics, vmem_limit_bytes=vmem_limit)


def _rms(x, g):
    return x * lax.rsqrt(jnp.mean(x * x, axis=-1, keepdims=True) + NORM_EPS) * g


ROPE_PAIR_LANE = V7X_LANES // 2


def _rope_lane_masks(lane):
    first = lane < ROPE_HALF
    second = (lane >= ROPE_PAIR_LANE) & (lane < ROPE_PAIR_LANE + ROPE_HALF)
    return first, second


def _rope_lane_layout(x):
    cols = x.shape[1]
    lane = lax.broadcasted_iota(jnp.int32, x.shape, 1) % V7X_LANES
    shift = ROPE_PAIR_LANE - ROPE_HALF
    to_pair_lane = (lane >= ROPE_PAIR_LANE) & (lane < ROPE_PAIR_LANE + ROPE_HALF)
    from_pair_lane = (lane >= ROPE_HALF) & (lane < ROPE_DIM)
    return jnp.where(to_pair_lane, pltpu.roll(x, shift, 1),
                     jnp.where(from_pair_lane, pltpu.roll(x, cols - shift, 1), x))


def _rope_table_kernel(freq_ref, cos_ref, sin_ref):
    rows = cos_ref.shape[0]
    pos = (pl.program_id(0) * rows + lax.broadcasted_iota(jnp.int32, (rows, V7X_LANES), 0)).astype(F32)
    first, second = _rope_lane_masks(lax.broadcasted_iota(jnp.int32, (rows, V7X_LANES), 1))
    ang = pos * freq_ref[...]
    c = jnp.cos(ang)
    s = jnp.sin(ang)
    cos_ref[...] = jnp.where(first | second, c, 1.0)
    sin_ref[...] = jnp.where(first, -s, jnp.where(second, s, 0.0))


def _rope_tables():
    inv_freq = ROPE_THETA ** (-jnp.arange(0, ROPE_DIM, 2, dtype=F32) / ROPE_DIM)
    gap = jnp.zeros((ROPE_PAIR_LANE - ROPE_HALF,), F32)
    freq_lane = jnp.concatenate([inv_freq, gap, inv_freq, gap])[None, :]
    rows = 512
    tab = jax.ShapeDtypeStruct((SEQ, V7X_LANES), F32)
    spec = pl.BlockSpec((rows, V7X_LANES), lambda i: (i, 0))
    return pl.pallas_call(
        _rope_table_kernel,
        grid=(SEQ // rows,),
        in_specs=[pl.BlockSpec((1, V7X_LANES), lambda i: (0, 0))],
        out_specs=[spec, spec],
        out_shape=[tab, tab],
        compiler_params=_params(("parallel",)),
        name="rope_table",
    )(freq_lane)


def _cast_specs(casts, n_steps, step_of):
    in_specs, out_specs, out_shapes = [], [], []
    for w, splits, _ in casts:
        rows, cols = w.shape
        assert splits[0] == 0 and splits[-1] == cols
        rb = next(r for r in range(BF16_ROWS, rows + 1, BF16_ROWS) if rows % r == 0 and rows // r <= n_steps)
        n_blocks = rows // rb
        block_of = lambda *idx, n_blocks=n_blocks: (jnp.minimum(step_of(*idx), n_blocks - 1), 0)
        in_specs.append(pl.BlockSpec((rb, cols), block_of))
        for c0, c1 in zip(splits[:-1], splits[1:]):
            out_specs.append(pl.BlockSpec((rb, c1 - c0), block_of))
            out_shapes.append(jax.ShapeDtypeStruct((rows, c1 - c0), BF16))
    return in_specs, out_specs, out_shapes


def _cast_blocks(cast_in, cast_out, cast_meta):
    cast_out = iter(cast_out)
    for src, (splits, rope_cols) in zip(cast_in, cast_meta):
        for c0, c1 in zip(splits[:-1], splits[1:]):
            dst = next(cast_out)
            n_rope = min(max(rope_cols - c0, 0), c1 - c0)
            if n_rope:
                dst[:, :n_rope] = _rope_lane_layout(src[:, c0:c0 + n_rope]).astype(BF16)
            if n_rope < c1 - c0:
                dst[:, n_rope:] = src[:, c0 + n_rope:c1].astype(BF16)


def _cast_meta(casts):
    return tuple((splits, rope_cols) for _, splits, rope_cols in casts)


def _ffn_kernel(x_ref, g_ref, wg_ref, wu_ref, wd_ref, pg_ref, *refs, post, cast_meta):
    n_cast = len(cast_meta)
    cast_in, refs = refs[:n_cast], refs[n_cast:]
    if post == "hn":
        o_ref, hn_ref, *cast_out, xn_ref = refs
    else:
        o_ref, *cast_out, xn_ref = refs
    j = pl.program_id(1)
    last = pl.num_programs(1) - 1

    def tile_contribution(xn):
        _cast_blocks(cast_in, cast_out, cast_meta)
        h = jnp.dot(xn, wg_ref[...], preferred_element_type=F32)
        u = jnp.dot(xn, wu_ref[...], preferred_element_type=F32)
        a = ((0.5 * h) * (1.0 + jnp.tanh(0.5 * h)) * u).astype(BF16)
        return jnp.dot(a, wd_ref[...].astype(BF16), preferred_element_type=F32)

    @pl.when(j == 0)
    def _():
        xn = _rms(x_ref[...], g_ref[...]).astype(BF16)
        xn_ref[...] = xn
        o_ref[...] = tile_contribution(xn)

    @pl.when((j > 0) & (j < last))
    def _():
        o_ref[...] += tile_contribution(xn_ref[...])

    @pl.when(j == last)
    def _():
        y = x_ref[...] + 0.5 * (o_ref[...] + tile_contribution(xn_ref[...]))
        if post == "hn":
            o_ref[...] = y
            hn_ref[...] = _rms(y, pg_ref[...]).astype(BF16)
        elif post == "final":
            o_ref[...] = _rms(y, pg_ref[...])
        else:
            o_ref[...] = y


def _ffn(x, g, wg, wu, wd, post_g, *, post, casts=(), tm=512, tf=512):
    t, d = x.shape
    f = wg.shape[1]
    n_j = f // tf
    row = pl.BlockSpec((tm, d), lambda i, j: (i, 0))
    vec = pl.BlockSpec((1, d), lambda i, j: (0, 0))
    out_shape = [jax.ShapeDtypeStruct((t, d), F32)]
    out_specs = [row]
    if post == "hn":
        out_shape.append(jax.ShapeDtypeStruct((t, d), BF16))
        out_specs.append(row)
    cast_specs, cast_out_specs, cast_out_shapes = _cast_specs(
        casts, min(CAST_STEPS, (t // tm) * n_j), lambda i, j: i * n_j + j)
    out_specs += cast_out_specs
    out_shape += cast_out_shapes
    return pl.pallas_call(
        functools.partial(_ffn_kernel, post=post, cast_meta=_cast_meta(casts)),
        grid=(t // tm, n_j),
        in_specs=[
            row,
            vec,
            pl.BlockSpec((d, tf), lambda i, j: (0, j)),
            pl.BlockSpec((d, tf), lambda i, j: (0, j)),
            pl.BlockSpec((tf, d), lambda i, j: (j, 0)),
            vec,
        ] + cast_specs,
        out_specs=out_specs,
        out_shape=out_shape,
        scratch_shapes=[pltpu.VMEM((tm, d), BF16)],
        compiler_params=_params(("arbitrary", "arbitrary")),
        name="ffn_" + post,
    )(x, g, wg, wu, wd, post_g, *[c[0] for c in casts])


def _store_head(o_ref, h, val):
    dil = DILATIONS[h % N_ATT_HEADS // HEADS_PER_GROUP]
    if dil == 1:
        o_ref[h] = val.astype(BF16)
        return
    regrouped = pltpu.einshape("lrd->rld", val.reshape(QKV_TILE // dil, dil, HEAD_DIM))
    o_ref[h] = regrouped.reshape(QKV_TILE, HEAD_DIM).astype(BF16)


def _proj_kernel(h_ref, w_ref, *refs, epilogue, cast_meta):
    n_cast = len(cast_meta)
    n_cast_out = sum(len(splits) - 1 for splits, _ in cast_meta)
    n_tables = 2 if epilogue == "qkv" else 0
    cast_in = refs[n_tables:n_tables + n_cast]
    cast_out = refs[n_tables + n_cast + 1:n_tables + n_cast + 1 + n_cast_out]
    refs = refs[:n_tables] + refs[n_tables + n_cast:n_tables + n_cast + 1]
    _cast_blocks(cast_in, cast_out, cast_meta)

    if epilogue == "qkv":
        cos_ref, sin_ref, o_ref = refs
        cos, sin = cos_ref[...], sin_ref[...]
        rope = ((cos * HEAD_DIM ** -0.5, sin * HEAD_DIM ** -0.5), (cos, sin), None)
        lhs = h_ref[...]
        group_cols = HEADS_PER_GROUP * HEAD_DIM
        for kind, tables in enumerate(rope):
            for g in reversed(range(N_GROUPS)):
                h0 = kind * N_ATT_HEADS + g * HEADS_PER_GROUP
                acc = jnp.dot(lhs, w_ref[:, h0 * HEAD_DIM:h0 * HEAD_DIM + group_cols], preferred_element_type=F32)
                for hh in range(HEADS_PER_GROUP):
                    t = acc[:, hh * HEAD_DIM:(hh + 1) * HEAD_DIM]
                    if tables is not None:
                        t = t * tables[0] + pltpu.roll(t, ROPE_PAIR_LANE, 1) * tables[1]
                    _store_head(o_ref, h0 + hh, t)
        return

    (o_ref,) = refs
    acc = jnp.dot(h_ref[...], w_ref[...], preferred_element_type=F32)
    if epilogue == "gelu":
        acc = 0.5 * acc * (1.0 + lax.erf(acc * (0.5 ** 0.5)))
    elif epilogue == "sigmoid":
        acc = 0.5 * jnp.tanh(0.5 * acc) + 0.5
    o_ref[...] = acc.astype(o_ref.dtype)


def _proj(hn, w, col0, ncols, *, epilogue, tn, tm, tables=(), casts=()):
    t, d = hn.shape
    assert col0 % tn == 0 and ncols % tn == 0
    c0 = col0 // tn
    n_j = ncols // tn
    in_specs = [
        pl.BlockSpec((tm, d), lambda i, j: (i, 0)),
        pl.BlockSpec((d, tn), lambda i, j: (0, c0 + j), pipeline_mode=pl.Buffered(1) if n_j == 1 else None),
    ]
    seq_blocks = SEQ // tm
    for _ in tables:
        in_specs.append(pl.BlockSpec((tm, V7X_LANES), lambda i, j: (i % seq_blocks, 0)))
    if epilogue == "qkv":
        assert tm == QKV_TILE and tn == ncols == 3 * ATT_WIDTH
        out_spec = pl.BlockSpec((3 * N_ATT_HEADS, tm, HEAD_DIM), lambda i, j: (0, i, 0))
        out_shape = jax.ShapeDtypeStruct((3 * N_ATT_HEADS, t, HEAD_DIM), BF16)
    else:
        out_spec = pl.BlockSpec((tm, tn), lambda i, j: (i, j))
        out_shape = jax.ShapeDtypeStruct((t, ncols), BF16)
    cast_specs, cast_out_specs, cast_out_shapes = _cast_specs(casts, (t // tm) * n_j, lambda i, j: i * n_j + j)
    res = pl.pallas_call(
        functools.partial(_proj_kernel, epilogue=epilogue, cast_meta=_cast_meta(casts)),
        grid=(t // tm, n_j),
        in_specs=in_specs + cast_specs,
        out_specs=[out_spec] + cast_out_specs,
        out_shape=[out_shape] + cast_out_shapes,
        compiler_params=_params(("arbitrary", "arbitrary")),
        name="proj_" + epilogue,
    )(hn, w, *tables, *[c[0] for c in casts])
    return res if casts else res[0]


ATT_HEADS_PER_STEP = 4


def _strided_out(dil):
    return dil > 1 and dil % 8 != 0


def _attn_kernel(q_ref, k_ref, v_ref, o_ref, st_ref, *, dil):
    step = pl.program_id(1)

    @pl.when(step == 0)
    def _():
        st_ref[...] = jnp.zeros_like(st_ref)

    for hh in range(ATT_HEADS_PER_STEP):
        _attn_head(q_ref.at[hh], k_ref.at[hh], v_ref.at[hh], o_ref.at[hh], st_ref,
                   step * ATT_HEADS_PER_STEP + hh, dil)


def _attn_head(q_ref, k_ref, v_ref, o_ref, st_ref, head, dil):
    n_qb = SEQ // (ATT_BLOCK * dil)

    qi = lax.broadcasted_iota(jnp.int32, (ATT_BLOCK, 2 * ATT_BLOCK), 0)
    kj = lax.broadcasted_iota(jnp.int32, (ATT_BLOCK, 2 * ATT_BLOCK), 1)
    diff = qi + ATT_BLOCK - kj
    band = (diff >= 0) & (diff <= ATT_BLOCK)
    band_first = band & (kj >= ATT_BLOCK)
    lane = lax.broadcasted_iota(jnp.int32, (ATT_BLOCK, V7X_LANES), 1)
    ones = jnp.ones((2 * ATT_BLOCK, HEAD_DIM), BF16)
    zeros = jnp.zeros((ATT_BLOCK, HEAD_DIM), BF16)

    rows_per = QKV_TILE // dil
    run = min(ATT_BLOCK, rows_per)

    def load(ref, qb, r):
        parts = []
        for idx in range(qb * ATT_BLOCK, (qb + 1) * ATT_BLOCK, run):
            tile, local = divmod(idx, rows_per)
            parts.append(ref[pl.ds(tile * QKV_TILE + r * rows_per + local, run), :])
        return parts[0] if len(parts) == 1 else jnp.concatenate(parts, axis=0)

    interleave = dil > 1 and not _strided_out(dil)

    def put(ref, qb, blocks, merge=None):
        if interleave:
            rows = pl.ds(qb * ATT_BLOCK * dil, ATT_BLOCK * dil)
            val = pltpu.einshape("rld->lrd", jnp.stack(blocks)).reshape(ATT_BLOCK * dil, V7X_LANES)
            ref[rows, :] = (val if merge is None else merge(val, ref[rows, :])).astype(ref.dtype)
            return
        for r, val in enumerate(blocks):
            rows = (pl.ds(qb * ATT_BLOCK * dil + r, ATT_BLOCK, stride=dil) if dil > 1
                    else pl.ds(qb * ATT_BLOCK, ATT_BLOCK))
            ref[rows, :] = (val if merge is None else merge(val, ref[rows, :])).astype(ref.dtype)

    def keep_other_heads(new, old):
        own = lax.broadcasted_iota(jnp.int32, new.shape, 1)
        return jnp.where((own == head) | (own == HEADS_PER_GROUP + head), new, old)

    k_prev, v_prev = [zeros] * dil, [zeros] * dil
    for qb in range(n_qb):
        outs, stats = [], []
        for r in range(dil):
            q, k_cur, v_cur = load(q_ref, qb, r), load(k_ref, qb, r), load(v_ref, qb, r)
            k_win = jnp.concatenate([k_prev[r], k_cur], axis=0)
            v_win = jnp.concatenate([v_prev[r], v_cur], axis=0)
            s = lax.dot_general(q, k_win, (((1,), (1,)), ((), ())), preferred_element_type=F32)
            s = jnp.where(band_first if qb == 0 else band, s, -jnp.inf)
            m = jnp.max(s, axis=-1, keepdims=True)
            p = jnp.exp(s - m).astype(BF16)
            pv = jnp.dot(p, jnp.concatenate([v_win, ones], axis=1), preferred_element_type=F32)
            l_b = pv[:, HEAD_DIM:]
            outs.append(pv[:, :HEAD_DIM] / l_b)
            stats.append(jnp.where(lane == head, m, l_b))
            k_prev[r], v_prev[r] = k_cur, v_cur
        put(o_ref, qb, outs)
        put(st_ref, qb, stats, merge=keep_other_heads)


def _attention(qkv, group, batch):
    dil = DILATIONS[group]
    t = qkv.shape[1]
    h0 = group * HEADS_PER_GROUP

    hps = ATT_HEADS_PER_STEP

    def seq(slab0):
        assert slab0 % hps == 0
        return pl.BlockSpec((hps, SEQ, HEAD_DIM), lambda b, s: (slab0 // hps + s, b, 0))

    return pl.pallas_call(
        functools.partial(_attn_kernel, dil=dil),
        grid=(batch, HEADS_PER_GROUP // hps),
        in_specs=[seq(h0), seq(N_ATT_HEADS + h0), seq(2 * N_ATT_HEADS + h0)],
        out_specs=[
            pl.BlockSpec((hps, SEQ, HEAD_DIM), lambda b, s: (s, b, 0)),
            pl.BlockSpec((SEQ, V7X_LANES), lambda b, s: (b, 0)),
        ],
        out_shape=[
            jax.ShapeDtypeStruct((HEADS_PER_GROUP, t, HEAD_DIM), F32 if _strided_out(dil) else BF16),
            jax.ShapeDtypeStruct((t, V7X_LANES), F32),
        ],
        compiler_params=_params(("parallel", "arbitrary")),
        name=f"attn_d{dil}",
    )(qkv, qkv, qkv)


def _mix_prepare(o_refs, stats, u_ref, vs_ref, lng_ref, lnb_ref, sgw_ref, sgb_ref, oatt_ref, gated_ref):
    tm = oatt_ref.shape[0]
    for h in range(HEADS_PER_GROUP):
        ms = [s[:, h:h + 1] for s in stats]
        ls = [s[:, HEADS_PER_GROUP + h:HEADS_PER_GROUP + h + 1] for s in stats]
        m_all = jnp.maximum(jnp.maximum(ms[0], ms[1]), ms[2])
        ws = [l * jnp.exp(m - m_all) for l, m in zip(ls, ms)]
        num = sum(w * o_ref[h].astype(F32) for w, o_ref in zip(ws, o_refs))
        oatt_ref[:, h * HEAD_DIM:(h + 1) * HEAD_DIM] = (num / (ws[0] + ws[1] + ws[2])).astype(BF16)

    vs = vs_ref[...].astype(F32)
    mu = jnp.mean(vs, axis=-1, keepdims=True)
    var = jnp.mean(jnp.square(vs - mu), axis=-1, keepdims=True)
    vn = ((vs - mu) * lax.rsqrt(var + LN_EPS) * lng_ref[...] + lnb_ref[...]).astype(BF16)
    ti = lax.broadcasted_iota(jnp.int32, (SG_CHUNK, SG_CHUNK), 0)
    si = lax.broadcasted_iota(jnp.int32, (SG_CHUNK, SG_CHUNK), 1)
    causal = si <= ti
    for g in range(SG_GROUPS):
        cols = slice(g * SG_GROUP_DIM, (g + 1) * SG_GROUP_DIM)
        w_sp = jnp.where(causal, sgw_ref[g], 0.0).astype(BF16)
        bias = sgb_ref[:, g:g + 1]
        for c in range(tm // SG_CHUNK):
            rows = slice(c * SG_CHUNK, (c + 1) * SG_CHUNK)
            spatial = jnp.dot(w_sp, vn[rows, cols], preferred_element_type=F32) + bias
            gated_ref[rows, cols] = (u_ref[rows, cols].astype(F32) * spatial).astype(BF16)


def _mix_kernel(o0_ref, o1_ref, o2_ref, s0_ref, s1_ref, s2_ref, u_ref, vs_ref, ga_ref, gs_ref,
                lng_ref, lnb_ref, sgw_ref, sgb_ref, wa_ref, ws_ref, out_ref,
                oatt_a, gated_a, oatt_b, gated_b):
    i = pl.program_id(0)

    @pl.when(i == 0)
    def _():
        oatt_b[...] = jnp.zeros_like(oatt_b)
        gated_b[...] = jnp.zeros_like(gated_b)

    def step(oatt_w, gated_w, oatt_r, gated_r):
        y_att = jnp.dot(oatt_r[...], wa_ref[...], preferred_element_type=F32)
        y_sg = jnp.dot(gated_r[...], ws_ref[...], preferred_element_type=F32)
        merged = ga_ref[...].astype(F32) * y_att + gs_ref[...].astype(F32) * y_sg
        out_ref[...] = merged.astype(out_ref.dtype)
        _mix_prepare((o0_ref, o1_ref, o2_ref), (s0_ref[...], s1_ref[...], s2_ref[...]), u_ref, vs_ref,
                     lng_ref, lnb_ref, sgw_ref, sgb_ref, oatt_w, gated_w)

    @pl.when(i % 2 == 0)
    def _():
        step(oatt_a, gated_a, oatt_b, gated_b)

    @pl.when(i % 2 == 1)
    def _():
        step(oatt_b, gated_b, oatt_a, gated_a)


def _mix(o_list, st_list, uv, gates, ln_g, ln_b, sg_w, sg_b_t, w_att, w_sg, *, tm=512):
    t = uv.shape[0]
    n = t // tm
    cur = lambda i: jnp.minimum(i, n - 1)
    lag = lambda i: jnp.maximum(i - 1, 0)
    prep_row = lambda width, blk: pl.BlockSpec((tm, width), lambda i: (cur(i), blk))
    lag_row = lambda width, blk: pl.BlockSpec((tm, width), lambda i: (lag(i), blk))
    const2 = lambda shape: pl.BlockSpec(shape, lambda i: (0, 0), pipeline_mode=pl.Buffered(1))
    in_specs = (
        [pl.BlockSpec((HEADS_PER_GROUP, tm, HEAD_DIM), lambda i: (0, cur(i), 0))] * 3
        + [prep_row(V7X_LANES, 0)] * 3
        + [prep_row(SG_WIDTH, 0), prep_row(SG_WIDTH, 1), lag_row(D_MODEL, 0), lag_row(D_MODEL, 1)]
        + [const2((1, SG_WIDTH)), const2((1, SG_WIDTH)),
           pl.BlockSpec(sg_w.shape, lambda i: (0, 0, 0), pipeline_mode=pl.Buffered(1)),
           const2(sg_b_t.shape), const2(w_att.shape), const2(w_sg.shape)]
    )
    att_w = HEADS_PER_GROUP * HEAD_DIM
    return pl.pallas_call(
        _mix_kernel,
        grid=(n + 1,),
        in_specs=in_specs,
        out_specs=pl.BlockSpec((tm, D_MODEL), lambda i: (lag(i), 0)),
        out_shape=jax.ShapeDtypeStruct((t, D_MODEL), BF16),
        scratch_shapes=[pltpu.VMEM((tm, att_w), BF16), pltpu.VMEM((tm, SG_WIDTH), BF16),
                        pltpu.VMEM((tm, att_w), BF16), pltpu.VMEM((tm, SG_WIDTH), BF16)],
        compiler_params=_params(("arbitrary",)),
        name="mix",
    )(*o_list, *st_list, uv, uv, gates, gates, ln_g, ln_b, sg_w, sg_b_t, w_att, w_sg)


def _out_proj_kernel(m_ref, w_ref, x_ref, o_ref):
    o_ref[...] = x_ref[...] + jnp.dot(m_ref[...], w_ref[...], preferred_element_type=F32)


def _out_proj(merged, w, x, *, tm=1024):
    t, d = merged.shape
    n = w.shape[1]
    return pl.pallas_call(
        _out_proj_kernel,
        grid=(t // tm,),
        in_specs=[
            pl.BlockSpec((tm, d), lambda i: (i, 0)),
            pl.BlockSpec((d, n), lambda i: (0, 0), pipeline_mode=pl.Buffered(1)),
            pl.BlockSpec((tm, n), lambda i: (i, 0)),
        ],
        out_specs=pl.BlockSpec((tm, n), lambda i: (i, 0)),
        out_shape=jax.ShapeDtypeStruct((t, n), F32),
        compiler_params=_params(("parallel",)),
        name="out_proj",
    )(merged, w, x)


def kernel(x, ffn1_norm, ffn1_w_gate, ffn1_w_up, ffn1_w_down, mix_norm, w_in, sg_ln_g, sg_ln_b, sg_w, sg_b, w_att_out, w_sg_out, w_out, ffn2_norm, ffn2_w_gate, ffn2_w_up, ffn2_w_down, final_norm):
    batch, seq, d = x.shape
    assert (seq, d) == (SEQ, D_MODEL)
    t = batch * seq
    depth = ffn1_norm.shape[0]
    xt = x.reshape(t, d)
    tables = _rope_tables()
    bf = lambda a: a.astype(BF16)
    for i in range(depth):
        last = i == depth - 1
        c_v, c_u, c_g = 2 * ATT_WIDTH, 3 * ATT_WIDTH, 3 * ATT_WIDTH + 2 * SG_WIDTH
        whole = lambda w: (w, (0, w.shape[1]), 0)
        x1, hn, w_in_b, w_gates_b = _ffn(
            xt, ffn1_norm[i][None], bf(ffn1_w_gate[i]), bf(ffn1_w_up[i]), ffn1_w_down[i],
            mix_norm[i][None], post="hn", casts=((w_in[i], (0, c_g, w_in.shape[2]), c_v),))
        qkv, w_out_b, w_sg_b, w_att_b = _proj(
            hn, w_in_b, 0, c_u, epilogue="qkv", tn=c_u, tm=QKV_TILE, tables=tables,
            casts=(whole(w_out[i]), whole(w_sg_out[i]), whole(w_att_out[i])))
        uv, w2_gate, w2_up = _proj(hn, w_in_b, c_u, 2 * SG_WIDTH, epilogue="gelu", tn=SG_WIDTH, tm=1024,
                                   casts=(whole(ffn2_w_gate[i]), whole(ffn2_w_up[i])))
        gates, w2_down = _proj(hn, w_gates_b, 0, 2 * D_MODEL, epilogue="sigmoid", tn=D_MODEL, tm=1024,
                               casts=(whole(ffn2_w_down[i]),))
        o_list, st_list = [], []
        for g in range(N_GROUPS):
            o_g, st_g = _attention(qkv, g, batch)
            o_list.append(o_g)
            st_list.append(st_g)
        merged = _mix(o_list, st_list, uv, gates, sg_ln_g[i][None], sg_ln_b[i][None], sg_w[i], sg_b[i].T,
                      w_att_b, w_sg_b)
        x2 = _out_proj(merged, w_out_b, x1)
        (xt,) = _ffn(x2, ffn2_norm[i][None], w2_gate, w2_up, w2_down,
                     final_norm[None], post="final" if last else "plain")
    return xt.reshape(batch, seq, d)
```

```python
import functools

import jax
import jax.numpy as jnp
from jax import lax
from jax.experimental import pallas as pl
from jax.experimental.pallas import tpu as pltpu

F32 = jnp.float32
BF16 = jnp.bfloat16

D_MODEL = 2048
SEQ = 4096
HEAD_DIM = 128
HEADS_PER_GROUP = 4
DILATIONS = (1, 4, 16)
N_GROUPS = len(DILATIONS)
N_ATT_HEADS = N_GROUPS * HEADS_PER_GROUP
ATT_WIDTH = N_ATT_HEADS * HEAD_DIM
ATT_BLOCK = 128
QKV_TILE = 512
ROPE_DIM = HEAD_DIM // 4
ROPE_HALF = ROPE_DIM // 2
ROPE_THETA = 500000.0
SG_CHUNK = 128
SG_GROUPS = 12
SG_GROUP_DIM = 128
SG_WIDTH = SG_GROUPS * SG_GROUP_DIM
NORM_EPS = 1e-6
LN_EPS = 1e-5

V7X_VMEM_BYTES = 64 * 1024 * 1024
V7X_LANES = 128
BF16_ROWS = 16
CAST_STEPS = 128
VMEM_LIMIT = V7X_VMEM_BYTES - 12 * 1024 * 1024


def _params(semantics, vmem_limit=VMEM_LIMIT):
    return pltpu.CompilerParams(dimension_semantics=semantics, vmem_limit_bytes=vmem_limit)


def _rms(x, g):
    return x * lax.rsqrt(jnp.mean(x * x, axis=-1, keepdims=True) + NORM_EPS) * g


ROPE_PAIR_LANE = V7X_LANES // 2


def _rope_lane_masks(lane):
    first = lane < ROPE_HALF
    second = (lane >= ROPE_PAIR_LANE) & (lane < ROPE_PAIR_LANE + ROPE_HALF)
    return first, second


def _rope_lane_layout(x):
    cols = x.shape[1]
    lane = lax.broadcasted_iota(jnp.int32, x.shape, 1) % V7X_LANES
    shift = ROPE_PAIR_LANE - ROPE_HALF
    to_pair_lane = (lane >= ROPE_PAIR_LANE) & (lane < ROPE_PAIR_LANE + ROPE_HALF)
    from_pair_lane = (lane >= ROPE_HALF) & (lane < ROPE_DIM)
    return jnp.where(to_pair_lane, pltpu.roll(x, shift, 1),
                     jnp.where(from_pair_lane, pltpu.roll(x, cols - shift, 1), x))


def _rope_table_kernel(freq_ref, cos_ref, sin_ref):
    rows = cos_ref.shape[0]
    pos = (pl.program_id(0) * rows + lax.broadcasted_iota(jnp.int32, (rows, V7X_LANES), 0)).astype(F32)
    first, second = _rope_lane_masks(lax.broadcasted_iota(jnp.int32, (rows, V7X_LANES), 1))
    ang = pos * freq_ref[...]
    c = jnp.cos(ang)
    s = jnp.sin(ang)
    cos_ref[...] = jnp.where(first | second, c, 1.0)
    sin_ref[...] = jnp.where(first, -s, jnp.where(second, s, 0.0))


def _rope_tables():
    inv_freq = ROPE_THETA ** (-jnp.arange(0, ROPE_DIM, 2, dtype=F32) / ROPE_DIM)
    gap = jnp.zeros((ROPE_PAIR_LANE - ROPE_HALF,), F32)
    freq_lane = jnp.concatenate([inv_freq, gap, inv_freq, gap])[None, :]
    rows = 512
    tab = jax.ShapeDtypeStruct((SEQ, V7X_LANES), F32)
    spec = pl.BlockSpec((rows, V7X_LANES), lambda i: (i, 0))
    return pl.pallas_call(
        _rope_table_kernel,
        grid=(SEQ // rows,),
        in_specs=[pl.BlockSpec((1, V7X_LANES), lambda i: (0, 0))],
        out_specs=[spec, spec],
        out_shape=[tab, tab],
        compiler_params=_params(("parallel",)),
        name="rope_table",
    )(freq_lane)


def _cast_specs(casts, n_steps, step_of):
    in_specs, out_specs, out_shapes = [], [], []
    for w, splits, _ in casts:
        rows, cols = w.shape
        assert splits[0] == 0 and splits[-1] == cols
        rb = next(r for r in range(BF16_ROWS, rows + 1, BF16_ROWS) if rows % r == 0 and rows // r <= n_steps)
        n_blocks = rows // rb
        block_of = lambda *idx, n_blocks=n_blocks: (jnp.minimum(step_of(*idx), n_blocks - 1), 0)
        in_specs.append(pl.BlockSpec((rb, cols), block_of))
        for c0, c1 in zip(splits[:-1], splits[1:]):
            out_specs.append(pl.BlockSpec((rb, c1 - c0), block_of))
            out_shapes.append(jax.ShapeDtypeStruct((rows, c1 - c0), BF16))
    return in_specs, out_specs, out_shapes


def _cast_blocks(cast_in, cast_out, cast_meta):
    cast_out = iter(cast_out)
    for src, (splits, rope_cols) in zip(cast_in, cast_meta):
        for c0, c1 in zip(splits[:-1], splits[1:]):
            dst = next(cast_out)
            n_rope = min(max(rope_cols - c0, 0), c1 - c0)
            if n_rope:
                dst[:, :n_rope] = _rope_lane_layout(src[:, c0:c0 + n_rope]).astype(BF16)
            if n_rope < c1 - c0:
                dst[:, n_rope:] = src[:, c0 + n_rope:c1].astype(BF16)


def _cast_meta(casts):
    return tuple((splits, rope_cols) for _, splits, rope_cols in casts)


def _ffn_kernel(x_ref, g_ref, wg_ref, wu_ref, wd_ref, pg_ref, *refs, post, cast_meta):
    n_cast = len(cast_meta)
    cast_in, refs = refs[:n_cast], refs[n_cast:]
    if post == "hn":
        o_ref, hn_ref, *cast_out, xn_ref = refs
    else:
        o_ref, *cast_out, xn_ref = refs
    j = pl.program_id(1)
    last = pl.num_programs(1) - 1

    def tile_contribution(xn):
        _cast_blocks(cast_in, cast_out, cast_meta)
        h = jnp.dot(xn, wg_ref[...], preferred_element_type=F32)
        u = jnp.dot(xn, wu_ref[...].astype(BF16), preferred_element_type=F32)
        a = ((0.5 * h) * (1.0 + jnp.tanh(0.5 * h)) * u).astype(BF16)
        return jnp.dot(a, wd_ref[...].astype(BF16), preferred_element_type=F32)

    @pl.when(j == 0)
    def _():
        xn = _rms(x_ref[...], g_ref[...]).astype(BF16)
        xn_ref[...] = xn
        o_ref[...] = tile_contribution(xn)

    @pl.when((j > 0) & (j < last))
    def _():
        o_ref[...] += tile_contribution(xn_ref[...])

    @pl.when(j == last)
    def _():
        y = x_ref[...] + 0.5 * (o_ref[...] + tile_contribution(xn_ref[...]))
        if post == "hn":
            o_ref[...] = y
            hn_ref[...] = _rms(y, pg_ref[...]).astype(BF16)
        elif post == "final":
            o_ref[...] = _rms(y, pg_ref[...])
        else:
            o_ref[...] = y


def _ffn(x, g, wg, wu, wd, post_g, *, post, casts=(), tm=512, tf=512):
    t, d = x.shape
    f = wg.shape[1]
    n_j = f // tf
    row = pl.BlockSpec((tm, d), lambda i, j: (i, 0))
    vec = pl.BlockSpec((1, d), lambda i, j: (0, 0))
    out_shape = [jax.ShapeDtypeStruct((t, d), F32)]
    out_specs = [row]
    if post == "hn":
        out_shape.append(jax.ShapeDtypeStruct((t, d), BF16))
        out_specs.append(row)
    cast_specs, cast_out_specs, cast_out_shapes = _cast_specs(
        casts, min(CAST_STEPS, (t // tm) * n_j), lambda i, j: i * n_j + j)
    out_specs += cast_out_specs
    out_shape += cast_out_shapes
    return pl.pallas_call(
        functools.partial(_ffn_kernel, post=post, cast_meta=_cast_meta(casts)),
        grid=(t // tm, n_j),
        in_specs=[
            row,
            vec,
            pl.BlockSpec((d, tf), lambda i, j: (0, j)),
            pl.BlockSpec((d, tf), lambda i, j: (0, j)),
            pl.BlockSpec((tf, d), lambda i, j: (j, 0)),
            vec,
        ] + cast_specs,
        out_specs=out_specs,
        out_shape=out_shape,
        scratch_shapes=[pltpu.VMEM((tm, d), BF16)],
        compiler_params=_params(("arbitrary", "arbitrary")),
        name="ffn_" + post,
    )(x, g, wg, wu, wd, post_g, *[c[0] for c in casts])


def _store_head(o_ref, h, val):
    dil = DILATIONS[h % N_ATT_HEADS // HEADS_PER_GROUP]
    if dil == 1:
        o_ref[h] = val.astype(BF16)
        return
    regrouped = pltpu.einshape("lrd->rld", val.reshape(QKV_TILE // dil, dil, HEAD_DIM))
    o_ref[h] = regrouped.reshape(QKV_TILE, HEAD_DIM).astype(BF16)


def _proj_kernel(h_ref, w_ref, *refs, epilogue, cast_meta):
    n_cast = len(cast_meta)
    n_cast_out = sum(len(splits) - 1 for splits, _ in cast_meta)
    n_tables = 2 if epilogue == "qkv" else 0
    cast_in = refs[n_tables:n_tables + n_cast]
    cast_out = refs[n_tables + n_cast + 1:n_tables + n_cast + 1 + n_cast_out]
    refs = refs[:n_tables] + refs[n_tables + n_cast:n_tables + n_cast + 1]
    _cast_blocks(cast_in, cast_out, cast_meta)

    if epilogue == "qkv":
        cos_ref, sin_ref, o_ref = refs
        cos, sin = cos_ref[...], sin_ref[...]
        rope = ((cos * HEAD_DIM ** -0.5, sin * HEAD_DIM ** -0.5), (cos, sin), None)
        lhs = h_ref[...]
        group_cols = HEADS_PER_GROUP * HEAD_DIM
        for kind, tables in enumerate(rope):
            for g in reversed(range(N_GROUPS)):
                h0 = kind * N_ATT_HEADS + g * HEADS_PER_GROUP
                acc = jnp.dot(lhs, w_ref[:, h0 * HEAD_DIM:h0 * HEAD_DIM + group_cols], preferred_element_type=F32)
                for hh in range(HEADS_PER_GROUP):
                    t = acc[:, hh * HEAD_DIM:(hh + 1) * HEAD_DIM]
                    if tables is not None:
                        t = t * tables[0] + pltpu.roll(t, ROPE_PAIR_LANE, 1) * tables[1]
                    _store_head(o_ref, h0 + hh, t)
        return

    (o_ref,) = refs
    acc = jnp.dot(h_ref[...], w_ref[...], preferred_element_type=F32)
    if epilogue == "gelu":
        acc = 0.5 * acc * (1.0 + lax.erf(acc * (0.5 ** 0.5)))
    elif epilogue == "sigmoid":
        acc = 0.5 * jnp.tanh(0.5 * acc) + 0.5
    o_ref[...] = acc.astype(o_ref.dtype)


def _proj(hn, w, col0, ncols, *, epilogue, tn, tm, tables=(), casts=()):
    t, d = hn.shape
    assert col0 % tn == 0 and ncols % tn == 0
    c0 = col0 // tn
    n_j = ncols // tn
    in_specs = [
        pl.BlockSpec((tm, d), lambda i, j: (i, 0)),
        pl.BlockSpec((d, tn), lambda i, j: (0, c0 + j), pipeline_mode=pl.Buffered(1) if n_j == 1 else None),
    ]
    seq_blocks = SEQ // tm
    for _ in tables:
        in_specs.append(pl.BlockSpec((tm, V7X_LANES), lambda i, j: (i % seq_blocks, 0)))
    if epilogue == "qkv":
        assert tm == QKV_TILE and tn == ncols == 3 * ATT_WIDTH
        out_spec = pl.BlockSpec((3 * N_ATT_HEADS, tm, HEAD_DIM), lambda i, j: (0, i, 0))
        out_shape = jax.ShapeDtypeStruct((3 * N_ATT_HEADS, t, HEAD_DIM), BF16)
    else:
        out_spec = pl.BlockSpec((tm, tn), lambda i, j: (i, j))
        out_shape = jax.ShapeDtypeStruct((t, ncols), BF16)
    cast_specs, cast_out_specs, cast_out_shapes = _cast_specs(casts, (t // tm) * n_j, lambda i, j: i * n_j + j)
    res = pl.pallas_call(
        functools.partial(_proj_kernel, epilogue=epilogue, cast_meta=_cast_meta(casts)),
        grid=(t // tm, n_j),
        in_specs=in_specs + cast_specs,
        out_specs=[out_spec] + cast_out_specs,
        out_shape=[out_shape] + cast_out_shapes,
        compiler_params=_params(("arbitrary", "arbitrary")),
        name="proj_" + epilogue,
    )(hn, w, *tables, *[c[0] for c in casts])
    return res if casts else res[0]


ATT_HEADS_PER_STEP = 4


def _strided_out(dil):
    return dil > 1 and dil % 8 != 0


def _attn_kernel(q_ref, k_ref, v_ref, o_ref, st_ref, *, dil):
    step = pl.program_id(1)

    @pl.when(step == 0)
    def _():
        st_ref[...] = jnp.zeros_like(st_ref)

    for hh in range(ATT_HEADS_PER_STEP):
        _attn_head(q_ref.at[hh], k_ref.at[hh], v_ref.at[hh], o_ref.at[hh], st_ref,
                   step * ATT_HEADS_PER_STEP + hh, dil)


def _attn_head(q_ref, k_ref, v_ref, o_ref, st_ref, head, dil):
    n_qb = SEQ // (ATT_BLOCK * dil)

    qi = lax.broadcasted_iota(jnp.int32, (ATT_BLOCK, 2 * ATT_BLOCK), 0)
    kj = lax.broadcasted_iota(jnp.int32, (ATT_BLOCK, 2 * ATT_BLOCK), 1)
    diff = qi + ATT_BLOCK - kj
    band = (diff >= 0) & (diff <= ATT_BLOCK)
    band_first = band & (kj >= ATT_BLOCK)
    lane = lax.broadcasted_iota(jnp.int32, (ATT_BLOCK, V7X_LANES), 1)
    ones = jnp.ones((2 * ATT_BLOCK, HEAD_DIM), BF16)
    zeros = jnp.zeros((ATT_BLOCK, HEAD_DIM), BF16)

    rows_per = QKV_TILE // dil
    run = min(ATT_BLOCK, rows_per)

    def load(ref, qb, r):
        parts = []
        for idx in range(qb * ATT_BLOCK, (qb + 1) * ATT_BLOCK, run):
            tile, local = divmod(idx, rows_per)
            parts.append(ref[pl.ds(tile * QKV_TILE + r * rows_per + local, run), :])
        return parts[0] if len(parts) == 1 else jnp.concatenate(parts, axis=0)

    interleave = dil > 1 and not _strided_out(dil)

    def put(ref, qb, blocks, merge=None):
        if interleave:
            rows = pl.ds(qb * ATT_BLOCK * dil, ATT_BLOCK * dil)
            val = pltpu.einshape("rld->lrd", jnp.stack(blocks)).reshape(ATT_BLOCK * dil, V7X_LANES)
            ref[rows, :] = (val if merge is None else merge(val, ref[rows, :])).astype(ref.dtype)
            return
        for r, val in enumerate(blocks):
            rows = (pl.ds(qb * ATT_BLOCK * dil + r, ATT_BLOCK, stride=dil) if dil > 1
                    else pl.ds(qb * ATT_BLOCK, ATT_BLOCK))
            ref[rows, :] = (val if merge is None else merge(val, ref[rows, :])).astype(ref.dtype)

    def keep_other_heads(new, old):
        own = lax.broadcasted_iota(jnp.int32, new.shape, 1)
        return jnp.where((own == head) | (own == HEADS_PER_GROUP + head), new, old)

    k_prev, v_prev = [zeros] * dil, [zeros] * dil
    for qb in range(n_qb):
        outs, stats = [], []
        for r in range(dil):
            q, k_cur, v_cur = load(q_ref, qb, r), load(k_ref, qb, r), load(v_ref, qb, r)
            k_win = jnp.concatenate([k_prev[r], k_cur], axis=0)
            v_win = jnp.concatenate([v_prev[r], v_cur], axis=0)
            s = lax.dot_general(q, k_win, (((1,), (1,)), ((), ())), preferred_element_type=F32)
            s = jnp.where(band_first if qb == 0 else band, s, -jnp.inf)
            m = jnp.max(s, axis=-1, keepdims=True)
            p = jnp.exp(s - m).astype(BF16)
            pv = jnp.dot(p, jnp.concatenate([v_win, ones], axis=1), preferred_element_type=F32)
            l_b = pv[:, HEAD_DIM:]
            outs.append(pv[:, :HEAD_DIM] / l_b)
            stats.append(jnp.where(lane == head, m, l_b))
            k_prev[r], v_prev[r] = k_cur, v_cur
        put(o_ref, qb, outs)
        put(st_ref, qb, stats, merge=keep_other_heads)


def _attention(qkv, group, batch):
    dil = DILATIONS[group]
    t = qkv.shape[1]
    h0 = group * HEADS_PER_GROUP

    hps = ATT_HEADS_PER_STEP

    def seq(slab0):
        assert slab0 % hps == 0
        return pl.BlockSpec((hps, SEQ, HEAD_DIM), lambda b, s: (slab0 // hps + s, b, 0))

    return pl.pallas_call(
        functools.partial(_attn_kernel, dil=dil),
        grid=(batch, HEADS_PER_GROUP // hps),
        in_specs=[seq(h0), seq(N_ATT_HEADS + h0), seq(2 * N_ATT_HEADS + h0)],
        out_specs=[
            pl.BlockSpec((hps, SEQ, HEAD_DIM), lambda b, s: (s, b, 0)),
            pl.BlockSpec((SEQ, V7X_LANES), lambda b, s: (b, 0)),
        ],
        out_shape=[
            jax.ShapeDtypeStruct((HEADS_PER_GROUP, t, HEAD_DIM), F32 if _strided_out(dil) else BF16),
            jax.ShapeDtypeStruct((t, V7X_LANES), F32),
        ],
        compiler_params=_params(("parallel", "arbitrary")),
        name=f"attn_d{dil}",
    )(qkv, qkv, qkv)


def _mix_prepare(o_refs, stats, u_ref, vs_ref, lng_ref, lnb_ref, sgw_ref, sgb_ref, oatt_ref, gated_ref):
    tm = oatt_ref.shape[0]
    for h in range(HEADS_PER_GROUP):
        ms = [s[:, h:h + 1] for s in stats]
        ls = [s[:, HEADS_PER_GROUP + h:HEADS_PER_GROUP + h + 1] for s in stats]
        m_all = jnp.maximum(jnp.maximum(ms[0], ms[1]), ms[2])
        ws = [l * jnp.exp(m - m_all) for l, m in zip(ls, ms)]
        num = sum(w * o_ref[h].astype(F32) for w, o_ref in zip(ws, o_refs))
        oatt_ref[:, h * HEAD_DIM:(h + 1) * HEAD_DIM] = (num / (ws[0] + ws[1] + ws[2])).astype(BF16)

    vs = vs_ref[...].astype(F32)
    mu = jnp.mean(vs, axis=-1, keepdims=True)
    var = jnp.mean(jnp.square(vs - mu), axis=-1, keepdims=True)
    vn = ((vs - mu) * lax.rsqrt(var + LN_EPS) * lng_ref[...] + lnb_ref[...]).astype(BF16)
    ti = lax.broadcasted_iota(jnp.int32, (SG_CHUNK, SG_CHUNK), 0)
    si = lax.broadcasted_iota(jnp.int32, (SG_CHUNK, SG_CHUNK), 1)
    causal = si <= ti
    for g in range(SG_GROUPS):
        cols = slice(g * SG_GROUP_DIM, (g + 1) * SG_GROUP_DIM)
        w_sp = jnp.where(causal, sgw_ref[g], 0.0).astype(BF16)
        bias = sgb_ref[:, g:g + 1]
        for c in range(tm // SG_CHUNK):
            rows = slice(c * SG_CHUNK, (c + 1) * SG_CHUNK)
            spatial = jnp.dot(w_sp, vn[rows, cols], preferred_element_type=F32) + bias
            gated_ref[rows, cols] = (u_ref[rows, cols].astype(F32) * spatial).astype(BF16)


def _mix_kernel(o0_ref, o1_ref, o2_ref, s0_ref, s1_ref, s2_ref, u_ref, vs_ref, ga_ref, gs_ref,
                lng_ref, lnb_ref, sgw_ref, sgb_ref, wa_ref, ws_ref, out_ref,
                oatt_a, gated_a, oatt_b, gated_b):
    i = pl.program_id(0)

    @pl.when(i == 0)
    def _():
        oatt_b[...] = jnp.zeros_like(oatt_b)
        gated_b[...] = jnp.zeros_like(gated_b)

    def step(oatt_w, gated_w, oatt_r, gated_r):
        y_att = jnp.dot(oatt_r[...], wa_ref[...], preferred_element_type=F32)
        y_sg = jnp.dot(gated_r[...], ws_ref[...], preferred_element_type=F32)
        merged = ga_ref[...].astype(F32) * y_att + gs_ref[...].astype(F32) * y_sg
        out_ref[...] = merged.astype(out_ref.dtype)
        _mix_prepare((o0_ref, o1_ref, o2_ref), (s0_ref[...], s1_ref[...], s2_ref[...]), u_ref, vs_ref,
                     lng_ref, lnb_ref, sgw_ref, sgb_ref, oatt_w, gated_w)

    @pl.when(i % 2 == 0)
    def _():
        step(oatt_a, gated_a, oatt_b, gated_b)

    @pl.when(i % 2 == 1)
    def _():
        step(oatt_b, gated_b, oatt_a, gated_a)


def _mix(o_list, st_list, uv, gates, ln_g, ln_b, sg_w, sg_b_t, w_att, w_sg, *, tm=512):
    t = uv.shape[0]
    n = t // tm
    cur = lambda i: jnp.minimum(i, n - 1)
    lag = lambda i: jnp.maximum(i - 1, 0)
    prep_row = lambda width, blk: pl.BlockSpec((tm, width), lambda i: (cur(i), blk))
    lag_row = lambda width, blk: pl.BlockSpec((tm, width), lambda i: (lag(i), blk))
    const2 = lambda shape: pl.BlockSpec(shape, lambda i: (0, 0), pipeline_mode=pl.Buffered(1))
    in_specs = (
        [pl.BlockSpec((HEADS_PER_GROUP, tm, HEAD_DIM), lambda i: (0, cur(i), 0))] * 3
        + [prep_row(V7X_LANES, 0)] * 3
        + [prep_row(SG_WIDTH, 0), prep_row(SG_WIDTH, 1), lag_row(D_MODEL, 0), lag_row(D_MODEL, 1)]
        + [const2((1, SG_WIDTH)), const2((1, SG_WIDTH)),
           pl.BlockSpec(sg_w.shape, lambda i: (0, 0, 0), pipeline_mode=pl.Buffered(1)),
           const2(sg_b_t.shape), const2(w_att.shape), const2(w_sg.shape)]
    )
    att_w = HEADS_PER_GROUP * HEAD_DIM
    return pl.pallas_call(
        _mix_kernel,
        grid=(n + 1,),
        in_specs=in_specs,
        out_specs=pl.BlockSpec((tm, D_MODEL), lambda i: (lag(i), 0)),
        out_shape=jax.ShapeDtypeStruct((t, D_MODEL), BF16),
        scratch_shapes=[pltpu.VMEM((tm, att_w), BF16), pltpu.VMEM((tm, SG_WIDTH), BF16),
                        pltpu.VMEM((tm, att_w), BF16), pltpu.VMEM((tm, SG_WIDTH), BF16)],
        compiler_params=_params(("arbitrary",)),
        name="mix",
    )(*o_list, *st_list, uv, uv, gates, gates, ln_g, ln_b, sg_w, sg_b_t, w_att, w_sg)


def _out_proj_kernel(m_ref, w_ref, x_ref, o_ref):
    o_ref[...] = x_ref[...] + jnp.dot(m_ref[...], w_ref[...], preferred_element_type=F32)


def _out_proj(merged, w, x, *, tm=1024):
    t, d = merged.shape
    n = w.shape[1]
    return pl.pallas_call(
        _out_proj_kernel,
        grid=(t // tm,),
        in_specs=[
            pl.BlockSpec((tm, d), lambda i: (i, 0)),
            pl.BlockSpec((d, n), lambda i: (0, 0), pipeline_mode=pl.Buffered(1)),
            pl.BlockSpec((tm, n), lambda i: (i, 0)),
        ],
        out_specs=pl.BlockSpec((tm, n), lambda i: (i, 0)),
        out_shape=jax.ShapeDtypeStruct((t, n), F32),
        compiler_params=_params(("parallel",)),
        name="out_proj",
    )(merged, w, x)


def kernel(x, ffn1_norm, ffn1_w_gate, ffn1_w_up, ffn1_w_down, mix_norm, w_in, sg_ln_g, sg_ln_b, sg_w, sg_b, w_att_out, w_sg_out, w_out, ffn2_norm, ffn2_w_gate, ffn2_w_up, ffn2_w_down, final_norm):
    batch, seq, d = x.shape
    assert (seq, d) == (SEQ, D_MODEL)
    t = batch * seq
    depth = ffn1_norm.shape[0]
    xt = x.reshape(t, d)
    tables = _rope_tables()
    bf = lambda a: a.astype(BF16)
    for i in range(depth):
        last = i == depth - 1
        c_v, c_u, c_g = 2 * ATT_WIDTH, 3 * ATT_WIDTH, 3 * ATT_WIDTH + 2 * SG_WIDTH
        whole = lambda w: (w, (0, w.shape[1]), 0)
        x1, hn, w_in_b, w_gates_b = _ffn(
            xt, ffn1_norm[i][None], bf(ffn1_w_gate[i]), ffn1_w_up[i], ffn1_w_down[i],
            mix_norm[i][None], post="hn", casts=((w_in[i], (0, c_g, w_in.shape[2]), c_v),))
        qkv, w_out_b, w_sg_b, w_att_b = _proj(
            hn, w_in_b, 0, c_u, epilogue="qkv", tn=c_u, tm=QKV_TILE, tables=tables,
            casts=(whole(w_out[i]), whole(w_sg_out[i]), whole(w_att_out[i])))
        uv, w2_gate, w2_up = _proj(hn, w_in_b, c_u, 2 * SG_WIDTH, epilogue="gelu", tn=SG_WIDTH, tm=1024,
                                   casts=(whole(ffn2_w_gate[i]), whole(ffn2_w_up[i])))
        gates, w2_down = _proj(hn, w_gates_b, 0, 2 * D_MODEL, epilogue="sigmoid", tn=D_MODEL, tm=1024,
                               casts=(whole(ffn2_w_down[i]),))
        o_list, st_list = [], []
        for g in range(N_GROUPS):
            o_g, st_g = _attention(qkv, g, batch)
            o_list.append(o_g)
            st_list.append(st_g)
        merged = _mix(o_list, st_list, uv, gates, sg_ln_g[i][None], sg_ln_b[i][None], sg_w[i], sg_b[i].T,
                      w_att_b, w_sg_b)
        x2 = _out_proj(merged, w_out_b, x1)
        (xt,) = _ffn(x2, ffn2_norm[i][None], w2_gate, w2_up, w2_down,
                     final_norm[None], post="final" if last else "plain")
    return xt.reshape(batch, seq, d)
```

```python
import functools

import jax
import jax.numpy as jnp
from jax import lax
from jax.experimental import pallas as pl
from jax.experimental.pallas import tpu as pltpu

F32 = jnp.float32
BF16 = jnp.bfloat16

D_MODEL = 2048
SEQ = 4096
HEAD_DIM = 128
HEADS_PER_GROUP = 4
DILATIONS = (1, 4, 16)
N_GROUPS = len(DILATIONS)
N_ATT_HEADS = N_GROUPS * HEADS_PER_GROUP
ATT_WIDTH = N_ATT_HEADS * HEAD_DIM
ATT_BLOCK = 128
QKV_TILE = 512
ROPE_DIM = HEAD_DIM // 4
ROPE_HALF = ROPE_DIM // 2
ROPE_THETA = 500000.0
SG_CHUNK = 128
SG_GROUPS = 12
SG_GROUP_DIM = 128
SG_WIDTH = SG_GROUPS * SG_GROUP_DIM
NORM_EPS = 1e-6
LN_EPS = 1e-5

V7X_VMEM_BYTES = 64 * 1024 * 1024
V7X_LANES = 128
BF16_ROWS = 16
CAST_STEPS = 128
VMEM_LIMIT = V7X_VMEM_BYTES - 12 * 1024 * 1024


def _params(semantics, vmem_limit=VMEM_LIMIT):
    return pltpu.CompilerParams(dimension_semantics=semantics, vmem_limit_bytes=vmem_limit)


def _rms(x, g):
    return x * lax.rsqrt(jnp.mean(x * x, axis=-1, keepdims=True) + NORM_EPS) * g


ROPE_PAIR_LANE = V7X_LANES // 2


def _rope_lane_masks(lane):
    first = lane < ROPE_HALF
    second = (lane >= ROPE_PAIR_LANE) & (lane < ROPE_PAIR_LANE + ROPE_HALF)
    return first, second


def _rope_lane_layout(x):
    cols = x.shape[1]
    lane = lax.broadcasted_iota(jnp.int32, x.shape, 1) % V7X_LANES
    shift = ROPE_PAIR_LANE - ROPE_HALF
    to_pair_lane = (lane >= ROPE_PAIR_LANE) & (lane < ROPE_PAIR_LANE + ROPE_HALF)
    from_pair_lane = (lane >= ROPE_HALF) & (lane < ROPE_DIM)
    return jnp.where(to_pair_lane, pltpu.roll(x, shift, 1),
                     jnp.where(from_pair_lane, pltpu.roll(x, cols - shift, 1), x))


def _rope_table_kernel(freq_ref, cos_ref, sin_ref):
    rows = cos_ref.shape[0]
    pos = (pl.program_id(0) * rows + lax.broadcasted_iota(jnp.int32, (rows, V7X_LANES), 0)).astype(F32)
    first, second = _rope_lane_masks(lax.broadcasted_iota(jnp.int32, (rows, V7X_LANES), 1))
    ang = pos * freq_ref[...]
    c = jnp.cos(ang)
    s = jnp.sin(ang)
    cos_ref[...] = jnp.where(first | second, c, 1.0)
    sin_ref[...] = jnp.where(first, -s, jnp.where(second, s, 0.0))


def _rope_tables():
    inv_freq = ROPE_THETA ** (-jnp.arange(0, ROPE_DIM, 2, dtype=F32) / ROPE_DIM)
    gap = jnp.zeros((ROPE_PAIR_LANE - ROPE_HALF,), F32)
    freq_lane = jnp.concatenate([inv_freq, gap, inv_freq, gap])[None, :]
    rows = 512
    tab = jax.ShapeDtypeStruct((SEQ, V7X_LANES), F32)
    spec = pl.BlockSpec((rows, V7X_LANES), lambda i: (i, 0))
    return pl.pallas_call(
        _rope_table_kernel,
        grid=(SEQ // rows,),
        in_specs=[pl.BlockSpec((1, V7X_LANES), lambda i: (0, 0))],
        out_specs=[spec, spec],
        out_shape=[tab, tab],
        compiler_params=_params(("parallel",)),
        name="rope_table",
    )(freq_lane)


def _cast_specs(casts, n_steps, step_of):
    in_specs, out_specs, out_shapes = [], [], []
    for w, splits, _ in casts:
        rows, cols = w.shape
        assert splits[0] == 0 and splits[-1] == cols
        rb = next(r for r in range(BF16_ROWS, rows + 1, BF16_ROWS) if rows % r == 0 and rows // r <= n_steps)
        n_blocks = rows // rb
        block_of = lambda *idx, n_blocks=n_blocks: (jnp.minimum(step_of(*idx), n_blocks - 1), 0)
        in_specs.append(pl.BlockSpec((rb, cols), block_of))
        for c0, c1 in zip(splits[:-1], splits[1:]):
            out_specs.append(pl.BlockSpec((rb, c1 - c0), block_of))
            out_shapes.append(jax.ShapeDtypeStruct((rows, c1 - c0), BF16))
    return in_specs, out_specs, out_shapes


def _cast_blocks(cast_in, cast_out, cast_meta):
    cast_out = iter(cast_out)
    for src, (splits, rope_cols) in zip(cast_in, cast_meta):
        for c0, c1 in zip(splits[:-1], splits[1:]):
            dst = next(cast_out)
            n_rope = min(max(rope_cols - c0, 0), c1 - c0)
            if n_rope:
                dst[:, :n_rope] = _rope_lane_layout(src[:, c0:c0 + n_rope]).astype(BF16)
            if n_rope < c1 - c0:
                dst[:, n_rope:] = src[:, c0 + n_rope:c1].astype(BF16)


def _cast_meta(casts):
    return tuple((splits, rope_cols) for _, splits, rope_cols in casts)


def _ffn_kernel(x_ref, g_ref, wg_ref, wu_ref, wd_ref, pg_ref, *refs, post, cast_meta):
    n_cast = len(cast_meta)
    cast_in, refs = refs[:n_cast], refs[n_cast:]
    if post == "hn":
        o_ref, hn_ref, *cast_out, xn_ref = refs
    else:
        o_ref, *cast_out, xn_ref = refs
    j = pl.program_id(1)
    last = pl.num_programs(1) - 1

    def tile_contribution(xn):
        _cast_blocks(cast_in, cast_out, cast_meta)
        h = jnp.dot(xn, wg_ref[...], preferred_element_type=F32)
        u = jnp.dot(xn, wu_ref[...], preferred_element_type=F32)
        a = ((0.5 * h) * (1.0 + jnp.tanh(0.5 * h)) * u).astype(BF16)
        return jnp.dot(a, wd_ref[...].astype(BF16), preferred_element_type=F32)

    @pl.when(j == 0)
    def _():
        xn = _rms(x_ref[...], g_ref[...]).astype(BF16)
        xn_ref[...] = xn
        o_ref[...] = tile_contribution(xn)

    @pl.when((j > 0) & (j < last))
    def _():
        o_ref[...] += tile_contribution(xn_ref[...])

    @pl.when(j == last)
    def _():
        y = x_ref[...] + 0.5 * (o_ref[...] + tile_contribution(xn_ref[...]))
        if post == "hn":
            o_ref[...] = y
            hn_ref[...] = _rms(y, pg_ref[...]).astype(BF16)
        elif post == "final":
            o_ref[...] = _rms(y, pg_ref[...])
        else:
            o_ref[...] = y


def _ffn(x, g, wg, wu, wd, post_g, *, post, casts=(), tm=512, tf=512):
    t, d = x.shape
    f = wg.shape[1]
    n_j = f // tf
    row = pl.BlockSpec((tm, d), lambda i, j: (i, 0))
    vec = pl.BlockSpec((1, d), lambda i, j: (0, 0))
    out_shape = [jax.ShapeDtypeStruct((t, d), F32)]
    out_specs = [row]
    if post == "hn":
        out_shape.append(jax.ShapeDtypeStruct((t, d), BF16))
        out_specs.append(row)
    cast_specs, cast_out_specs, cast_out_shapes = _cast_specs(
        casts, min(CAST_STEPS, (t // tm) * n_j), lambda i, j: i * n_j + j)
    out_specs += cast_out_specs
    out_shape += cast_out_shapes
    return pl.pallas_call(
        functools.partial(_ffn_kernel, post=post, cast_meta=_cast_meta(casts)),
        grid=(t // tm, n_j),
        in_specs=[
            row,
            vec,
            pl.BlockSpec((d, tf), lambda i, j: (0, j)),
            pl.BlockSpec((d, tf), lambda i, j: (0, j)),
            pl.BlockSpec((tf, d), lambda i, j: (j, 0)),
            vec,
        ] + cast_specs,
        out_specs=out_specs,
        out_shape=out_shape,
        scratch_shapes=[pltpu.VMEM((tm, d), BF16)],
        compiler_params=_params(("arbitrary", "arbitrary")),
        name="ffn_" + post,
    )(x, g, wg, wu, wd, post_g, *[c[0] for c in casts])


def _store_head(o_ref, h, val):
    dil = DILATIONS[h % N_ATT_HEADS // HEADS_PER_GROUP]
    if dil == 1:
        o_ref[h] = val.astype(BF16)
        return
    regrouped = pltpu.einshape("lrd->rld", val.reshape(QKV_TILE // dil, dil, HEAD_DIM))
    o_ref[h] = regrouped.reshape(QKV_TILE, HEAD_DIM).astype(BF16)


def _proj_kernel(h_ref, w_ref, *refs, epilogue, cast_meta):
    n_cast = len(cast_meta)
    n_cast_out = sum(len(splits) - 1 for splits, _ in cast_meta)
    n_tables = 2 if epilogue == "qkv" else 0
    cast_in = refs[n_tables:n_tables + n_cast]
    cast_out = refs[n_tables + n_cast + 1:n_tables + n_cast + 1 + n_cast_out]
    refs = refs[:n_tables] + refs[n_tables + n_cast:n_tables + n_cast + 1]
    _cast_blocks(cast_in, cast_out, cast_meta)

    if epilogue == "qkv":
        cos_ref, sin_ref, o_ref = refs
        cos, sin = cos_ref[...], sin_ref[...]
        rope = ((cos * HEAD_DIM ** -0.5, sin * HEAD_DIM ** -0.5), (cos, sin), None)
        lhs = h_ref[...]
        group_cols = HEADS_PER_GROUP * HEAD_DIM
        for kind, tables in enumerate(rope):
            for g in reversed(range(N_GROUPS)):
                h0 = kind * N_ATT_HEADS + g * HEADS_PER_GROUP
                acc = jnp.dot(lhs, w_ref[:, h0 * HEAD_DIM:h0 * HEAD_DIM + group_cols], preferred_element_type=F32)
                for hh in range(HEADS_PER_GROUP):
                    t = acc[:, hh * HEAD_DIM:(hh + 1) * HEAD_DIM]
                    if tables is not None:
                        t = t * tables[0] + pltpu.roll(t, ROPE_PAIR_LANE, 1) * tables[1]
                    _store_head(o_ref, h0 + hh, t)
        return

    (o_ref,) = refs
    acc = jnp.dot(h_ref[...], w_ref[...], preferred_element_type=F32)
    if epilogue == "gelu":
        acc = 0.5 * acc * (1.0 + lax.erf(acc * (0.5 ** 0.5)))
    elif epilogue == "sigmoid":
        acc = 0.5 * jnp.tanh(0.5 * acc) + 0.5
    o_ref[...] = acc.astype(o_ref.dtype)


def _proj(hn, w, col0, ncols, *, epilogue, tn, tm, tables=(), casts=()):
    t, d = hn.shape
    assert col0 % tn == 0 and ncols % tn == 0
    c0 = col0 // tn
    n_j = ncols // tn
    in_specs = [
        pl.BlockSpec((tm, d), lambda i, j: (i, 0)),
        pl.BlockSpec((d, tn), lambda i, j: (0, c0 + j), pipeline_mode=pl.Buffered(1) if n_j == 1 else None),
    ]
    seq_blocks = SEQ // tm
    for _ in tables:
        in_specs.append(pl.BlockSpec((tm, V7X_LANES), lambda i, j: (i % seq_blocks, 0)))
    if epilogue == "qkv":
        assert tm == QKV_TILE and tn == ncols == 3 * ATT_WIDTH
        out_spec = pl.BlockSpec((3 * N_ATT_HEADS, tm, HEAD_DIM), lambda i, j: (0, i, 0))
        out_shape = jax.ShapeDtypeStruct((3 * N_ATT_HEADS, t, HEAD_DIM), BF16)
    else:
        out_spec = pl.BlockSpec((tm, tn), lambda i, j: (i, j))
        out_shape = jax.ShapeDtypeStruct((t, ncols), BF16)
    cast_specs, cast_out_specs, cast_out_shapes = _cast_specs(casts, (t // tm) * n_j, lambda i, j: i * n_j + j)
    res = pl.pallas_call(
        functools.partial(_proj_kernel, epilogue=epilogue, cast_meta=_cast_meta(casts)),
        grid=(t // tm, n_j),
        in_specs=in_specs + cast_specs,
        out_specs=[out_spec] + cast_out_specs,
        out_shape=[out_shape] + cast_out_shapes,
        compiler_params=_params(("arbitrary", "arbitrary")),
        name="proj_" + epilogue,
    )(hn, w, *tables, *[c[0] for c in casts])
    return res if casts else res[0]


def _attn_kernel(q_ref, k_ref, v_ref, o_ref, st_ref):
    group = pl.program_id(1)
    st_ref[...] = jnp.zeros_like(st_ref)
    for g, dil in enumerate(DILATIONS):
        @pl.when(group == g)
        def _(dil=dil):
            for hh in range(HEADS_PER_GROUP):
                _attn_head(q_ref.at[hh], k_ref.at[hh], v_ref.at[hh], o_ref.at[hh], st_ref, hh, dil)


def _attn_head(q_ref, k_ref, v_ref, o_ref, st_ref, head, dil):
    n_qb = SEQ // (ATT_BLOCK * dil)

    qi = lax.broadcasted_iota(jnp.int32, (ATT_BLOCK, 2 * ATT_BLOCK), 0)
    kj = lax.broadcasted_iota(jnp.int32, (ATT_BLOCK, 2 * ATT_BLOCK), 1)
    diff = qi + ATT_BLOCK - kj
    band = (diff >= 0) & (diff <= ATT_BLOCK)
    band_first = band & (kj >= ATT_BLOCK)
    lane = lax.broadcasted_iota(jnp.int32, (ATT_BLOCK, V7X_LANES), 1)
    ones = jnp.ones((2 * ATT_BLOCK, HEAD_DIM), BF16)
    zeros = jnp.zeros((ATT_BLOCK, HEAD_DIM), BF16)

    rows_per = QKV_TILE // dil
    run = min(ATT_BLOCK, rows_per)

    def load(ref, qb, r):
        parts = []
        for idx in range(qb * ATT_BLOCK, (qb + 1) * ATT_BLOCK, run):
            tile, local = divmod(idx, rows_per)
            parts.append(ref[pl.ds(tile * QKV_TILE + r * rows_per + local, run), :])
        return parts[0] if len(parts) == 1 else jnp.concatenate(parts, axis=0)

    interleave = dil > 1

    def put(ref, qb, blocks, merge=None):
        if interleave:
            rows = pl.ds(qb * ATT_BLOCK * dil, ATT_BLOCK * dil)
            val = pltpu.einshape("rld->lrd", jnp.stack(blocks)).reshape(ATT_BLOCK * dil, V7X_LANES)
            ref[rows, :] = (val if merge is None else merge(val, ref[rows, :])).astype(ref.dtype)
            return
        for r, val in enumerate(blocks):
            rows = (pl.ds(qb * ATT_BLOCK * dil + r, ATT_BLOCK, stride=dil) if dil > 1
                    else pl.ds(qb * ATT_BLOCK, ATT_BLOCK))
            ref[rows, :] = (val if merge is None else merge(val, ref[rows, :])).astype(ref.dtype)

    def keep_other_heads(new, old):
        own = lax.broadcasted_iota(jnp.int32, new.shape, 1)
        return jnp.where((own == head) | (own == HEADS_PER_GROUP + head), new, old)

    k_prev, v_prev = [zeros] * dil, [zeros] * dil
    for qb in range(n_qb):
        outs, stats = [], []
        for r in range(dil):
            q, k_cur, v_cur = load(q_ref, qb, r), load(k_ref, qb, r), load(v_ref, qb, r)
            k_win = jnp.concatenate([k_prev[r], k_cur], axis=0)
            v_win = jnp.concatenate([v_prev[r], v_cur], axis=0)
            s = lax.dot_general(q, k_win, (((1,), (1,)), ((), ())), preferred_element_type=F32)
            s = jnp.where(band_first if qb == 0 else band, s, -jnp.inf)
            m = jnp.max(s, axis=-1, keepdims=True)
            p = jnp.exp(s - m).astype(BF16)
            pv = jnp.dot(p, jnp.concatenate([v_win, ones], axis=1), preferred_element_type=F32)
            l_b = pv[:, HEAD_DIM:]
            outs.append(pv[:, :HEAD_DIM] / l_b)
            stats.append(jnp.where(lane == head, m, l_b))
            k_prev[r], v_prev[r] = k_cur, v_cur
        put(o_ref, qb, outs)
        put(st_ref, qb, stats, merge=keep_other_heads)


def _attention(qkv, batch):
    t = qkv.shape[1]

    def heads(kind):
        return pl.BlockSpec((HEADS_PER_GROUP, SEQ, HEAD_DIM), lambda b, g: (kind * N_GROUPS + g, b, 0))

    return pl.pallas_call(
        _attn_kernel,
        grid=(batch, N_GROUPS),
        in_specs=[heads(0), heads(1), heads(2)],
        out_specs=[
            pl.BlockSpec((HEADS_PER_GROUP, SEQ, HEAD_DIM), lambda b, g: (g, b, 0)),
            pl.BlockSpec((None, SEQ, V7X_LANES), lambda b, g: (g, b, 0)),
        ],
        out_shape=[
            jax.ShapeDtypeStruct((N_ATT_HEADS, t, HEAD_DIM), BF16),
            jax.ShapeDtypeStruct((N_GROUPS, t, V7X_LANES), F32),
        ],
        compiler_params=_params(("parallel", "arbitrary")),
        name="attn",
    )(qkv, qkv, qkv)


def _mix_prepare(o_refs, stats, u_ref, vs_ref, lng_ref, lnb_ref, sgw_ref, sgb_ref, oatt_ref, gated_ref):
    tm = oatt_ref.shape[0]
    for h in range(HEADS_PER_GROUP):
        ms = [s[:, h:h + 1] for s in stats]
        ls = [s[:, HEADS_PER_GROUP + h:HEADS_PER_GROUP + h + 1] for s in stats]
        m_all = jnp.maximum(jnp.maximum(ms[0], ms[1]), ms[2])
        ws = [l * jnp.exp(m - m_all) for l, m in zip(ls, ms)]
        num = sum(w * o_ref[h].astype(F32) for w, o_ref in zip(ws, o_refs))
        oatt_ref[:, h * HEAD_DIM:(h + 1) * HEAD_DIM] = (num / (ws[0] + ws[1] + ws[2])).astype(BF16)

    vs = vs_ref[...].astype(F32)
    mu = jnp.mean(vs, axis=-1, keepdims=True)
    var = jnp.mean(jnp.square(vs - mu), axis=-1, keepdims=True)
    vn = ((vs - mu) * lax.rsqrt(var + LN_EPS) * lng_ref[...] + lnb_ref[...]).astype(BF16)
    ti = lax.broadcasted_iota(jnp.int32, (SG_CHUNK, SG_CHUNK), 0)
    si = lax.broadcasted_iota(jnp.int32, (SG_CHUNK, SG_CHUNK), 1)
    causal = si <= ti
    for g in range(SG_GROUPS):
        cols = slice(g * SG_GROUP_DIM, (g + 1) * SG_GROUP_DIM)
        w_sp = jnp.where(causal, sgw_ref[g], 0.0).astype(BF16)
        bias = sgb_ref[:, g:g + 1]
        for c in range(tm // SG_CHUNK):
            rows = slice(c * SG_CHUNK, (c + 1) * SG_CHUNK)
            spatial = jnp.dot(w_sp, vn[rows, cols], preferred_element_type=F32) + bias
            gated_ref[rows, cols] = (u_ref[rows, cols].astype(F32) * spatial).astype(BF16)


def _mix_kernel(o0_ref, o1_ref, o2_ref, s0_ref, s1_ref, s2_ref, u_ref, vs_ref, ga_ref, gs_ref,
                lng_ref, lnb_ref, sgw_ref, sgb_ref, wa_ref, ws_ref, out_ref,
                oatt_a, gated_a, oatt_b, gated_b):
    i = pl.program_id(0)

    @pl.when(i == 0)
    def _():
        oatt_b[...] = jnp.zeros_like(oatt_b)
        gated_b[...] = jnp.zeros_like(gated_b)

    def step(oatt_w, gated_w, oatt_r, gated_r):
        y_att = jnp.dot(oatt_r[...], wa_ref[...], preferred_element_type=F32)
        y_sg = jnp.dot(gated_r[...], ws_ref[...], preferred_element_type=F32)
        merged = ga_ref[...].astype(F32) * y_att + gs_ref[...].astype(F32) * y_sg
        out_ref[...] = merged.astype(out_ref.dtype)
        _mix_prepare((o0_ref, o1_ref, o2_ref), (s0_ref[...], s1_ref[...], s2_ref[...]), u_ref, vs_ref,
                     lng_ref, lnb_ref, sgw_ref, sgb_ref, oatt_w, gated_w)

    @pl.when(i % 2 == 0)
    def _():
        step(oatt_a, gated_a, oatt_b, gated_b)

    @pl.when(i % 2 == 1)
    def _():
        step(oatt_b, gated_b, oatt_a, gated_a)


def _mix(o_all, st_all, uv, gates, ln_g, ln_b, sg_w, sg_b_t, w_att, w_sg, *, tm=512):
    t = uv.shape[0]
    n = t // tm
    cur = lambda i: jnp.minimum(i, n - 1)
    lag = lambda i: jnp.maximum(i - 1, 0)
    prep_row = lambda width, blk: pl.BlockSpec((tm, width), lambda i: (cur(i), blk))
    lag_row = lambda width, blk: pl.BlockSpec((tm, width), lambda i: (lag(i), blk))
    const2 = lambda shape: pl.BlockSpec(shape, lambda i: (0, 0), pipeline_mode=pl.Buffered(1))
    in_specs = (
        [pl.BlockSpec((HEADS_PER_GROUP, tm, HEAD_DIM), lambda i, g=g: (g, cur(i), 0)) for g in range(N_GROUPS)]
        + [pl.BlockSpec((None, tm, V7X_LANES), lambda i, g=g: (g, cur(i), 0)) for g in range(N_GROUPS)]
        + [prep_row(SG_WIDTH, 0), prep_row(SG_WIDTH, 1), lag_row(D_MODEL, 0), lag_row(D_MODEL, 1)]
        + [const2((1, SG_WIDTH)), const2((1, SG_WIDTH)),
           pl.BlockSpec(sg_w.shape, lambda i: (0, 0, 0), pipeline_mode=pl.Buffered(1)),
           const2(sg_b_t.shape), const2(w_att.shape), const2(w_sg.shape)]
    )
    att_w = HEADS_PER_GROUP * HEAD_DIM
    return pl.pallas_call(
        _mix_kernel,
        grid=(n + 1,),
        in_specs=in_specs,
        out_specs=pl.BlockSpec((tm, D_MODEL), lambda i: (lag(i), 0)),
        out_shape=jax.ShapeDtypeStruct((t, D_MODEL), BF16),
        scratch_shapes=[pltpu.VMEM((tm, att_w), BF16), pltpu.VMEM((tm, SG_WIDTH), BF16),
                        pltpu.VMEM((tm, att_w), BF16), pltpu.VMEM((tm, SG_WIDTH), BF16)],
        compiler_params=_params(("arbitrary",)),
        name="mix",
    )(*[o_all] * N_GROUPS, *[st_all] * N_GROUPS, uv, uv, gates, gates, ln_g, ln_b, sg_w, sg_b_t, w_att, w_sg)


def _out_proj_kernel(m_ref, w_ref, x_ref, o_ref):
    o_ref[...] = x_ref[...] + jnp.dot(m_ref[...], w_ref[...], preferred_element_type=F32)


def _out_proj(merged, w, x, *, tm=1024):
    t, d = merged.shape
    n = w.shape[1]
    return pl.pallas_call(
        _out_proj_kernel,
        grid=(t // tm,),
        in_specs=[
            pl.BlockSpec((tm, d), lambda i: (i, 0)),
            pl.BlockSpec((d, n), lambda i: (0, 0), pipeline_mode=pl.Buffered(1)),
            pl.BlockSpec((tm, n), lambda i: (i, 0)),
        ],
        out_specs=pl.BlockSpec((tm, n), lambda i: (i, 0)),
        out_shape=jax.ShapeDtypeStruct((t, n), F32),
        compiler_params=_params(("parallel",)),
        name="out_proj",
    )(merged, w, x)


def kernel(x, ffn1_norm, ffn1_w_gate, ffn1_w_up, ffn1_w_down, mix_norm, w_in, sg_ln_g, sg_ln_b, sg_w, sg_b, w_att_out, w_sg_out, w_out, ffn2_norm, ffn2_w_gate, ffn2_w_up, ffn2_w_down, final_norm):
    batch, seq, d = x.shape
    assert (seq, d) == (SEQ, D_MODEL)
    t = batch * seq
    depth = ffn1_norm.shape[0]
    xt = x.reshape(t, d)
    tables = _rope_tables()
    bf = lambda a: a.astype(BF16)
    for i in range(depth):
        last = i == depth - 1
        c_v, c_u, c_g = 2 * ATT_WIDTH, 3 * ATT_WIDTH, 3 * ATT_WIDTH + 2 * SG_WIDTH
        whole = lambda w: (w, (0, w.shape[1]), 0)
        x1, hn, w_in_b, w_gates_b = _ffn(
            xt, ffn1_norm[i][None], bf(ffn1_w_gate[i]), bf(ffn1_w_up[i]), ffn1_w_down[i],
            mix_norm[i][None], post="hn", casts=((w_in[i], (0, c_g, w_in.shape[2]), c_v),))
        qkv, w_out_b, w_sg_b, w_att_b = _proj(
            hn, w_in_b, 0, c_u, epilogue="qkv", tn=c_u, tm=QKV_TILE, tables=tables,
            casts=(whole(w_out[i]), whole(w_sg_out[i]), whole(w_att_out[i])))
        uv, w2_gate, w2_up = _proj(hn, w_in_b, c_u, 2 * SG_WIDTH, epilogue="gelu", tn=SG_WIDTH, tm=1024,
                                   casts=(whole(ffn2_w_gate[i]), whole(ffn2_w_up[i])))
        gates, w2_down = _proj(hn, w_gates_b, 0, 2 * D_MODEL, epilogue="sigmoid", tn=D_MODEL, tm=1024,
                               casts=(whole(ffn2_w_down[i]),))
        o_all, st_all = _attention(qkv, batch)
        merged = _mix(o_all, st_all, uv, gates, sg_ln_g[i][None], sg_ln_b[i][None], sg_w[i], sg_b[i].T,
                      w_att_b, w_sg_b)
        x2 = _out_proj(merged, w_out_b, x1)
        (xt,) = _ffn(x2, ffn2_norm[i][None], w2_gate, w2_up, w2_down,
                     final_norm[None], post="final" if last else "plain")
    return xt.reshape(batch, seq, d)
```
